```python
import math
import jax
import jax.numpy as jnp
from jax import lax
import numpy as np

D_MODEL = 1024
BATCH = 4
SEQ = 4096
DEPTH = 4
DEC_BATCH = 32
DEC_SEQ = 8
PAST_LEN = 8192
PAGE_SIZE = 128

N_MIXERS = 3
N_A = (DEPTH + 2) // 3
N_B = (DEPTH + 1) // 3
N_C = DEPTH // 3
D_FF = 4 * D_MODEL
NORM_EPS = 1e-6
L2_EPS = 1e-6
A_HEADS = 16
A_HEAD_DIM = D_MODEL // A_HEADS
Q_BLOCK = 128
A_LOGIT_BIAS_INIT = -7.0
B_HEADS = 8
B_HEAD_DIM = D_MODEL // B_HEADS
B_CONV = 4
B_CHUNK = 64
B_CONV_CH = 3 * B_HEADS * B_HEAD_DIM
B_IN = B_CONV_CH + B_HEADS * B_HEAD_DIM + 2 * B_HEADS
C_HEAD_DIM = 64
C_HEADS = D_MODEL // C_HEAD_DIM
C_DECAY_LORA = 64
C_AAA_LORA = 64
C_GATE_LORA = 128
C_GN_EPS = 64e-5

kernel_name = 'hybrid_sb_gdn_rwkv7_adaln_step'


def rms_norm(x, g, eps=NORM_EPS):
    xf = x.astype(jnp.float32)
    y = xf * lax.rsqrt(jnp.mean(xf * xf, axis=-1, keepdims=True) + eps)
    return (y * g.astype(jnp.float32)).astype(x.dtype)


def l2_normalize(x):
    xf = x.astype(jnp.float32)
    return xf * lax.rsqrt(jnp.sum(xf * xf, axis=-1, keepdims=True) + L2_EPS)


def squared_relu_mlp(h, w_up, w_down):
    u = jnp.maximum(h @ w_up, 0)
    return (u * u) @ w_down


def gather_pages(pool, page_table):
    b, n = page_table.shape
    pages = pool[page_table]
    return pages.reshape(b, n * pool.shape[1], pool.shape[2], pool.shape[3])


def stick_breaking_attention(q, k, v, logit_bias):
    bsz, t_len, n_h, d_h = q.shape
    s_len = k.shape[1]
    offset = s_len - t_len
    qb = min(Q_BLOCK, t_len)
    n_blk = -(-t_len // qb)
    pad = n_blk * qb - t_len
    qf = jnp.pad(q.astype(jnp.float32) * (d_h ** -0.5), ((0, 0), (0, pad), (0, 0), (0, 0)))
    qf = qf.reshape(bsz, n_blk, qb, n_h, d_h).transpose(1, 0, 2, 3, 4)
    q_pos = (offset + jnp.arange(n_blk * qb, dtype=jnp.int32)).reshape(n_blk, qb)
    k_pos = jnp.arange(s_len, dtype=jnp.int32)
    kf = k.astype(jnp.float32)
    vf = v.astype(jnp.float32)
    bias = logit_bias.astype(jnp.float32)[None, :, None, None]

    def block(args):
        q_blk, pos = args
        z = jnp.einsum('bqhd,bshd->bhqs', q_blk, kf) + bias
        earlier = k_pos[None, :] < pos[:, None]
        log_keep = jnp.where(earlier, -jax.nn.softplus(z), 0.0)
        between = lax.cumsum(log_keep, axis=3, reverse=True) - log_keep
        w = jnp.where(earlier, jnp.exp(jax.nn.log_sigmoid(z) + between), 0.0)
        return jnp.einsum('bhqs,bshd->bqhd', w, vf)

    o = lax.map(block, (qf, q_pos))
    return o.transpose(1, 0, 2, 3, 4).reshape(bsz, n_blk * qb, n_h, d_h)[:, :t_len]


def stick_breaking_mixer(h, w_qkv, q_g, k_g, logit_bias, w_o, past):
    b, t, _ = h.shape
    qkv = (h @ w_qkv).reshape(b, t, 3, A_HEADS, A_HEAD_DIM)
    q = rms_norm(qkv[:, :, 0], q_g)
    k = rms_norm(qkv[:, :, 1], k_g)
    v = qkv[:, :, 2]
    if past is None:
        k_all, v_all = k, v
    else:
        k_all = jnp.concatenate([past[0].astype(k.dtype), k], axis=1)
        v_all = jnp.concatenate([past[1].astype(v.dtype), v], axis=1)
    o = stick_breaking_attention(q, k_all, v_all, logit_bias)
    out = o.reshape(b, t, A_HEADS * A_HEAD_DIM).astype(h.dtype) @ w_o
    return out, k, v


def gated_delta_chunked(q, k, v, beta, g, s0):
    bsz, t_len, n_h, _ = k.shape
    c = min(B_CHUNK, t_len)
    n_c = -(-t_len // c)
    pad = n_c * c - t_len

    def chunk4(a):
        a = jnp.pad(a, ((0, 0), (0, pad), (0, 0), (0, 0)))
        return a.reshape(bsz, n_c, c, n_h, a.shape[-1]).transpose(1, 0, 3, 2, 4)

    def chunk3(a):
        a = jnp.pad(a, ((0, 0), (0, pad), (0, 0)))
        return a.reshape(bsz, n_c, c, n_h).transpose(1, 0, 3, 2)

    qc, kc, vc = chunk4(q), chunk4(k), chunk4(v)
    bc, gc = chunk3(beta), chunk3(g)
    G = jnp.cumsum(gc, axis=-1)
    idx = jnp.arange(c)
    lower = idx[:, None] >= idx[None, :]
    strict = idx[:, None] > idx[None, :]
    dG = G[..., :, None] - G[..., None, :]
    decay = jnp.where(lower, jnp.exp(jnp.where(lower, dG, 0.0)), 0.0)
    kk = jnp.einsum('nbhid,nbhjd->nbhij', kc, kc)
    amat = jnp.where(strict, bc[..., :, None] * decay * kk, 0.0) + jnp.eye(c, dtype=jnp.float32)
    eG = jnp.exp(G)
    w_mat = lax.linalg.triangular_solve(amat, (bc * eG)[..., None] * kc, left_side=True, lower=True, unit_diagonal=True)
    u0 = lax.linalg.triangular_solve(amat, bc[..., None] * vc, left_side=True, lower=True, unit_diagonal=True)
    qk = jnp.einsum('nbhid,nbhjd->nbhij', qc, kc) * decay
    q_dec = eG[..., None] * qc
    g_last = G[..., -1]
    k_end = jnp.exp(g_last[..., None] - G)[..., None] * kc

    def step(s, xs):
        w_c, u0_c, qk_c, qd_c, ke_c, gl_c = xs
        u = u0_c - jnp.einsum('bhck,bhkv->bhcv', w_c, s)
        o = jnp.einsum('bhck,bhkv->bhcv', qd_c, s) + jnp.einsum('bhij,bhjv->bhiv', qk_c, u)
        s = jnp.exp(gl_c)[..., None, None] * s + jnp.einsum('bhck,bhcv->bhkv', ke_c, u)
        return s, o

    s_fin, o = lax.scan(step, s0, (w_mat, u0, qk, q_dec, k_end, g_last))
    o = o.transpose(1, 0, 3, 2, 4).reshape(bsz, n_c * c, n_h, v.shape[-1])[:, :t_len]
    return o, s_fin


def gated_deltanet_mixer(h, w_in, conv_w, a_log, dt_bias, o_norm, w_o, conv0, ssm0):
    b, t, _ = h.shape
    f32 = jnp.float32
    hd = B_HEADS * B_HEAD_DIM
    proj = h @ w_in
    xqkv = proj[..., :B_CONV_CH]
    z = proj[..., B_CONV_CH:B_CONV_CH + hd]
    b_raw = proj[..., B_CONV_CH + hd:B_CONV_CH + hd + B_HEADS]
    a_raw = proj[..., B_CONV_CH + hd + B_HEADS:]
    xc = jnp.concatenate([conv0.astype(xqkv.dtype), xqkv], axis=1)
    conv = xc[:, 0:t].astype(f32) * conv_w[0].astype(f32)
    for i in range(1, B_CONV):
        conv = conv + xc[:, i:i + t].astype(f32) * conv_w[i].astype(f32)
    new_conv = xc[:, t:]
    act = jax.nn.silu(conv)
    q = l2_normalize(act[..., :hd].reshape(b, t, B_HEADS, B_HEAD_DIM)) * (B_HEAD_DIM ** -0.5)
    k = l2_normalize(act[..., hd:2 * hd].reshape(b, t, B_HEADS, B_HEAD_DIM))
    v = act[..., 2 * hd:].reshape(b, t, B_HEADS, B_HEAD_DIM)
    beta = jax.nn.sigmoid(b_raw.astype(f32))
    g = -jnp.exp(a_log.astype(f32)) * jax.nn.softplus(a_raw.astype(f32) + dt_bias.astype(f32))
    o, s_new = gated_delta_chunked(q, k, v, beta, g, ssm0.astype(f32))
    o = rms_norm(o, o_norm) * jax.nn.silu(z.astype(f32).reshape(b, t, B_HEADS, B_HEAD_DIM))
    out = o.reshape(b, t, hd).astype(h.dtype) @ w_o
    return out, new_conv, s_new


def rwkv7_mixer(h, mu, w_rkv, w0, w1, w2, a0, a1, a2, g1, g2, k_k, k_a, r_k, ln_g, ln_b, w_o, shift0, wkv0):
    b, t, d = h.shape
    f32 = jnp.float32
    prev = jnp.concatenate([shift0[:, None, :].astype(h.dtype), h[:, :-1]], axis=1)
    xx = prev - h
    mix = lambda i: h + xx * mu[i]
    r = (mix(0) @ w_rkv[0]).astype(f32)
    k = (mix(1) @ w_rkv[1]).astype(f32)
    v = (mix(2) @ w_rkv[2]).astype(f32)
    w_log = -jax.nn.softplus(-(w0 + jnp.tanh(mix(3) @ w1) @ w2).astype(f32)) - 0.5
    decay = jnp.exp(-jnp.exp(w_log))
    a = jax.nn.sigmoid((a0 + (mix(4) @ a1) @ a2).astype(f32))
    gate = (jax.nn.sigmoid(mix(5) @ g1) @ g2).astype(f32)
    heads = lambda z: z.reshape(b, t, C_HEADS, C_HEAD_DIM)
    kk = l2_normalize(heads(k * k_k.astype(f32)))
    k = k * (1.0 + (a - 1.0) * k_a.astype(f32))
    r_h, w_h, k_h, v_h, a_h = heads(r), heads(decay), heads(k), heads(v), heads(a)

    def step(s, inp):
        r_t, w_t, k_t, v_t, kk_t, a_t = inp
        s_kk = jnp.einsum('bhij,bhj->bhi', s, kk_t)
        s = s * w_t[:, :, None, :] - s_kk[..., None] * (kk_t * a_t)[:, :, None, :] + v_t[..., None] * k_t[:, :, None, :]
        return s, jnp.einsum('bhij,bhj->bhi', s, r_t)

    seq = (r_h, w_h, k_h, v_h, kk, a_h)
    wkv, y = lax.scan(step, wkv0.astype(f32), tuple(jnp.moveaxis(z, 1, 0) for z in seq))
    y = jnp.moveaxis(y, 0, 1)
    yc = y - jnp.mean(y, axis=-1, keepdims=True)
    yn = yc * lax.rsqrt(jnp.mean(yc * yc, axis=-1, keepdims=True) + C_GN_EPS)
    yn = yn.reshape(b, t, d) * ln_g.astype(f32) + ln_b.astype(f32)
    bonus = jnp.sum(r_h * k_h * r_k.astype(f32), axis=-1, keepdims=True) * v_h
    y = yn + bonus.reshape(b, t, d)
    out = (y * gate).astype(h.dtype) @ w_o
    return out, h[:, -1], wkv


def run_trunk(x, c, cache_k, cache_v, page_table, b_conv0, b_ssm0, c_shift0, c_wkv0, P):
    c_act = jax.nn.silu(c.astype(jnp.float32)).astype(x.dtype)
    ks, vs, bconv, bssm, cshift, cwkv = [], [], [], [], [], []
    for l in range(DEPTH):
        kind, j = l % N_MIXERS, l // N_MIXERS
        mod = (c_act @ P['w_ada'][l] + P['b_ada'][l])[:, None, :]
        sh1, sc1, gt1, sh2, sc2, gt2 = jnp.split(mod, 6, axis=-1)
        h = rms_norm(x, P['ln_mix'][l]) * (1 + sc1) + sh1
        if kind == 0:
            past = None
            if cache_k is not None:
                past = (gather_pages(cache_k[j], page_table), gather_pages(cache_v[j], page_table))
            out, k_new, v_new = stick_breaking_mixer(h, P['a_w_qkv'][j], P['a_q_norm'][j], P['a_k_norm'][j], P['a_logit_bias'][j], P['a_w_o'][j], past)
            ks.append(k_new.astype(x.dtype))
            vs.append(v_new.astype(x.dtype))
        elif kind == 1:
            out, conv_new, ssm_new = gated_deltanet_mixer(h, P['b_w_in'][j], P['b_conv'][j], P['b_a_log'][j], P['b_dt_bias'][j], P['b_o_norm'][j], P['b_w_o'][j], b_conv0[j], b_ssm0[j])
            bconv.append(conv_new.astype(x.dtype))
            bssm.append(ssm_new.astype(x.dtype))
        else:
            out, shift_new, wkv_new = rwkv7_mixer(h, P['c_mu'][j], P['c_w_rkv'][j], P['c_w0'][j], P['c_w1'][j], P['c_w2'][j], P['c_a0'][j], P['c_a1'][j], P['c_a2'][j], P['c_g1'][j], P['c_g2'][j], P['c_k_k'][j], P['c_k_a'][j], P['c_r_k'][j], P['c_ln_g'][j], P['c_ln_b'][j], P['c_w_o'][j], c_shift0[j], c_wkv0[j])
            cshift.append(shift_new.astype(x.dtype))
            cwkv.append(wkv_new.astype(x.dtype))
        x = x + gt1 * out
        h2 = rms_norm(x, P['ln_mlp'][l]) * (1 + sc2) + sh2
        x = x + gt2 * squared_relu_mlp(h2, P['w_up'][l], P['w_down'][l])
    return (x, jnp.stack(ks), jnp.stack(vs), jnp.stack(bconv), jnp.stack(bssm), jnp.stack(cshift), jnp.stack(cwkv))


def setup_inputs(seed: int = 0) -> dict:
    key = jax.random.key(seed)
    keys = iter(jax.random.split(key, 64))
    f32 = jnp.float32
    d = D_MODEL

    def nrm(shape, scale=1.0):
        return jax.random.normal(next(keys), shape, f32) * scale

    def gain(shape):
        return 1.0 + nrm(shape, 0.02)

    n_pages = PAST_LEN // PAGE_SIZE
    n_pool = (DEC_BATCH * n_pages * 5) // 4
    perm = jax.random.permutation(next(keys), n_pool)
    page_table = perm[:DEC_BATCH * n_pages].reshape(DEC_BATCH, n_pages).astype(jnp.int32)
    dt = jnp.exp(jax.random.uniform(next(keys), (N_B, B_HEADS), f32, math.log(1e-3), math.log(1e-1)))
    a_log = jnp.log(jax.random.uniform(next(keys), (N_B, B_HEADS), f32, 1.0, 16.0))
    return {
        'x_prompt': nrm((BATCH, SEQ, d)),
        'x_sample': nrm((DEC_BATCH, DEC_SEQ, d)),
        'c_prompt': nrm((BATCH, d)),
        'c_sample': nrm((DEC_BATCH, d)),
        'cache_k': nrm((N_A, n_pool, PAGE_SIZE, A_HEADS, A_HEAD_DIM)),
        'cache_v': nrm((N_A, n_pool, PAGE_SIZE, A_HEADS, A_HEAD_DIM)),
        'page_table': page_table,
        'state_b_conv': nrm((N_B, DEC_BATCH, B_CONV - 1, B_CONV_CH)),
        'state_b_ssm': nrm((N_B, DEC_BATCH, B_HEADS, B_HEAD_DIM, B_HEAD_DIM), 0.1),
        'state_c_shift': nrm((N_C, DEC_BATCH, d)),
        'state_c_wkv': nrm((N_C, DEC_BATCH, C_HEADS, C_HEAD_DIM, C_HEAD_DIM), 0.1),
        'ln_mix': gain((DEPTH, d)),
        'ln_mlp': gain((DEPTH, d)),
        'w_ada': nrm((DEPTH, d, 6 * d), 0.5 * d ** -0.5),
        'b_ada': nrm((DEPTH, 6 * d), 0.02),
        'w_up': nrm((DEPTH, d, D_FF), d ** -0.5),
        'w_down': nrm((DEPTH, D_FF, d), D_FF ** -0.5),
        'a_w_qkv': nrm((N_A, d, 3 * A_HEADS * A_HEAD_DIM), d ** -0.5),
        'a_q_norm': gain((N_A, A_HEAD_DIM)),
        'a_k_norm': gain((N_A, A_HEAD_DIM)),
        'a_logit_bias': A_LOGIT_BIAS_INIT + nrm((N_A, A_HEADS), 0.1),
        'a_w_o': nrm((N_A, A_HEADS * A_HEAD_DIM, d), d ** -0.5),
        'b_w_in': nrm((N_B, d, B_IN), d ** -0.5),
        'b_conv': nrm((N_B, B_CONV, B_CONV_CH), B_CONV ** -0.5),
        'b_a_log': a_log,
        'b_dt_bias': dt + jnp.log(-jnp.expm1(-dt)),
        'b_o_norm': gain((N_B, B_HEAD_DIM)),
        'b_w_o': nrm((N_B, B_HEADS * B_HEAD_DIM, d), d ** -0.5),
        'c_mu': jax.random.uniform(next(keys), (N_C, 6, d), f32),
        'c_w_rkv': nrm((N_C, 3, d, d), d ** -0.5),
        'c_w0': jax.random.uniform(next(keys), (N_C, d), f32, -4.0, 1.0),
        'c_w1': nrm((N_C, d, C_DECAY_LORA), d ** -0.5),
        'c_w2': nrm((N_C, C_DECAY_LORA, d), 0.1 * C_DECAY_LORA ** -0.5),
        'c_a0': nrm((N_C, d), 0.1),
        'c_a1': nrm((N_C, d, C_AAA_LORA), d ** -0.5),
        'c_a2': nrm((N_C, C_AAA_LORA, d), 0.1 * C_AAA_LORA ** -0.5),
        'c_g1': nrm((N_C, d, C_GATE_LORA), d ** -0.5),
        'c_g2': nrm((N_C, C_GATE_LORA, d), C_GATE_LORA ** -0.5),
        'c_k_k': 0.85 + nrm((N_C, d), 0.02),
        'c_k_a': gain((N_C, d)),
        'c_r_k': nrm((N_C, C_HEADS, C_HEAD_DIM), 0.1),
        'c_ln_g': gain((N_C, d)),
        'c_ln_b': nrm((N_C, d), 0.02),
        'c_w_o': nrm((N_C, d, d), d ** -0.5),
    }


def reference(x_prompt, x_sample, c_prompt, c_sample, cache_k, cache_v, page_table, state_b_conv, state_b_ssm, state_c_shift, state_c_wkv, ln_mix, ln_mlp, w_ada, b_ada, w_up, w_down, a_w_qkv, a_q_norm, a_k_norm, a_logit_bias, a_w_o, b_w_in, b_conv, b_a_log, b_dt_bias, b_o_norm, b_w_o, c_mu, c_w_rkv, c_w0, c_w1, c_w2, c_a0, c_a1, c_a2, c_g1, c_g2, c_k_k, c_k_a, c_r_k, c_ln_g, c_ln_b, c_w_o):
    P = {
        'ln_mix': ln_mix, 'ln_mlp': ln_mlp, 'w_ada': w_ada, 'b_ada': b_ada, 'w_up': w_up, 'w_down': w_down,
        'a_w_qkv': a_w_qkv, 'a_q_norm': a_q_norm, 'a_k_norm': a_k_norm, 'a_logit_bias': a_logit_bias, 'a_w_o': a_w_o,
        'b_w_in': b_w_in, 'b_conv': b_conv, 'b_a_log': b_a_log, 'b_dt_bias': b_dt_bias, 'b_o_norm': b_o_norm, 'b_w_o': b_w_o,
        'c_mu': c_mu, 'c_w_rkv': c_w_rkv, 'c_w0': c_w0, 'c_w1': c_w1, 'c_w2': c_w2, 'c_a0': c_a0, 'c_a1': c_a1, 'c_a2': c_a2,
        'c_g1': c_g1, 'c_g2': c_g2, 'c_k_k': c_k_k, 'c_k_a': c_k_a, 'c_r_k': c_r_k, 'c_ln_g': c_ln_g, 'c_ln_b': c_ln_b, 'c_w_o': c_w_o,
    }
    bp = x_prompt.shape[0]
    dt = x_prompt.dtype
    b_conv0 = jnp.zeros((N_B, bp, B_CONV - 1, B_CONV_CH), dt)
    b_ssm0 = jnp.zeros((N_B, bp, B_HEADS, B_HEAD_DIM, B_HEAD_DIM), dt)
    c_shift0 = jnp.zeros((N_C, bp, D_MODEL), dt)
    c_wkv0 = jnp.zeros((N_C, bp, C_HEADS, C_HEAD_DIM, C_HEAD_DIM), dt)
    y_p, k_p, v_p, bconv_p, bssm_p, cshift_p, cwkv_p = run_trunk(x_prompt, c_prompt, None, None, None, b_conv0, b_ssm0, c_shift0, c_wkv0, P)
    y_s, k_s, v_s, bconv_s, bssm_s, cshift_s, cwkv_s = run_trunk(x_sample, c_sample, cache_k, cache_v, page_table, state_b_conv, state_b_ssm, state_c_shift, state_c_wkv, P)
    return (y_p, y_s, k_p, v_p, k_s, v_s, bconv_p, bssm_p, bconv_s, bssm_s, cshift_p, cwkv_p, cshift_s, cwkv_s)
```

```python
import functools

import jax
import jax.numpy as jnp
from jax import lax
from jax.experimental import pallas as pl
from jax.experimental.pallas import tpu as pltpu

F32 = jnp.float32
BF16 = jnp.bfloat16

LANES = 128
MXU_DIM = 256
VMEM_LIMIT_BYTES = 56 * 1024 * 1024

NORM_EPS = 1e-6
L2_EPS = 1e-6
GN_EPS = 64e-5
A_HEAD_DIM = 64
B_HEAD_DIM = 128
C_HEAD_DIM = 64
B_CONV = 4
CHUNK = 64


def _params(*sem):
    return pltpu.CompilerParams(dimension_semantics=sem, vmem_limit_bytes=VMEM_LIMIT_BYTES)


def _bdot(a, b):
    return jnp.dot(a.astype(BF16), b.astype(BF16), preferred_element_type=F32)


def _bdot_nt(a, b):
    return lax.dot_general(a.astype(BF16), b.astype(BF16), (((1,), (1,)), ((), ())),
                           preferred_element_type=F32)


def _bdot_tn(a, b):
    return lax.dot_general(a.astype(BF16), b.astype(BF16), (((0,), (0,)), ((), ())),
                           preferred_element_type=F32)


def _split(x):
    hi = x.astype(BF16)
    lo = (x - hi.astype(F32)).astype(BF16)
    return hi, lo


def _dot_xe(x, e):
    hi, lo = _split(x)
    return (jnp.dot(hi, e, preferred_element_type=F32)
            + jnp.dot(lo, e, preferred_element_type=F32))


def _dot_ex(e, x):
    hi, lo = _split(x)
    return (jnp.dot(e, hi, preferred_element_type=F32)
            + jnp.dot(e, lo, preferred_element_type=F32))


def _dot3(a, b):
    ah, al = _split(a)
    bh, bl = _split(b)
    return (jnp.dot(ah, bh, preferred_element_type=F32)
            + jnp.dot(ah, bl, preferred_element_type=F32)
            + jnp.dot(al, bh, preferred_element_type=F32))


def _sigmoid(x):
    return 1.0 / (1.0 + jnp.exp(-x))


def _silu(x):
    return x * _sigmoid(x)


def _softplus(x):
    return jnp.maximum(x, 0.0) + jnp.log(1.0 + jnp.exp(-jnp.abs(x)))


def _normmod(x, g, sc, sh):
    ms = jnp.mean(x * x, axis=-1, keepdims=True)
    return x * lax.rsqrt(ms + NORM_EPS) * g * (1.0 + sc) + sh


def _group_matrix(n, group, value):
    r = lax.broadcasted_iota(jnp.int32, (n, n), 0) // group
    c = lax.broadcasted_iota(jnp.int32, (n, n), 1) // group
    return jnp.where(r == c, value, 0.0).astype(BF16)


def _group_reduce(x, gmat):
    n = gmat.shape[0]
    parts = [_dot_xe(x[:, s:s + n], gmat) for s in range(0, x.shape[1], n)]
    return parts[0] if len(parts) == 1 else jnp.concatenate(parts, axis=1)


def _tri_inverse(a, n_steps):
    n = a.shape[0]
    eye = (lax.broadcasted_iota(jnp.int32, (n, n), 0)
           == lax.broadcasted_iota(jnp.int32, (n, n), 1)).astype(F32)
    p = -a
    t = eye + p
    for _ in range(n_steps):
        p = _dot3(p, p)
        t = t + _dot3(t, p)
    return t


def _ada_kernel(c_ref, w_ref, b_ref, o_ref):
    ca = _silu(c_ref[...])
    o_ref[0] = _bdot(ca, w_ref[0]) + b_ref[0]


def _ada(c_all, w_ada, b_ada):
    depth, d, n = w_ada.shape
    rows = c_all.shape[0]
    tn = 1536
    return pl.pallas_call(
        _ada_kernel,
        grid=(depth, n // tn),
        in_specs=[pl.BlockSpec((rows, d), lambda l, j: (0, 0)),
                  pl.BlockSpec((1, d, tn), lambda l, j: (l, 0, j)),
                  pl.BlockSpec((1, 1, tn), lambda l, j: (l, 0, j))],
        out_specs=pl.BlockSpec((1, rows, tn), lambda l, j: (l, 0, j)),
        out_shape=jax.ShapeDtypeStruct((depth, rows, n), F32),
        compiler_params=_params("parallel", "parallel"),
        name="ada",
    )(c_all, w_ada, b_ada.reshape(depth, 1, n))


def _tok_spec(tm, width):
    return pl.BlockSpec((1, tm, width), lambda b, i: (b, i, 0))


def _mod_spec(mod, tm):
    if mod.shape[1] == 1:
        return pl.BlockSpec((1, 1, mod.shape[2]), lambda b, i: (b, 0, 0))
    return pl.BlockSpec((1, tm, mod.shape[2]), lambda b, i: (b, i, 0))


def _const_spec(arr):
    nd = arr.ndim
    return pl.BlockSpec(arr.shape, lambda b, i: (0,) * nd)


def _prev8_spec(tm, width):
    blocks = tm // 8
    return pl.BlockSpec((1, 8, width), lambda b, i: (b, jnp.maximum(i * blocks - 1, 0), 0))


def _mlp_kernel(x_ref, g_ref, sc_ref, sh_ref, gt_ref, wu_ref, wd_ref, o_ref, h_ref, acc_ref):
    f = pl.program_id(2)

    @pl.when(f == 0)
    def _():
        h_ref[...] = _normmod(x_ref[0], g_ref[...], sc_ref[0], sh_ref[0]).astype(BF16)
        acc_ref[...] = jnp.zeros_like(acc_ref)

    u = jnp.maximum(jnp.dot(h_ref[...], wu_ref[...], preferred_element_type=F32), 0.0)
    acc_ref[...] += jnp.dot((u * u).astype(BF16), wd_ref[...], preferred_element_type=F32)

    @pl.when(f == pl.num_programs(2) - 1)
    def _():
        o_ref[0] = x_ref[0] + gt_ref[0] * acc_ref[...]


def _mlp(x, g, sc, sh, gt, w_up, w_down, tm):
    bsz, t, d = x.shape
    ff = w_up.shape[1]
    tf = 1024

    def tok(b, i, f):
        return (b, i, 0)

    def mod_spec(m):
        if m.shape[1] == 1:
            return pl.BlockSpec((1, 1, d), lambda b, i, f: (b, 0, 0))
        return pl.BlockSpec((1, tm, d), tok)

    return pl.pallas_call(
        _mlp_kernel,
        grid=(bsz, t // tm, ff // tf),
        in_specs=[pl.BlockSpec((1, tm, d), tok),
                  pl.BlockSpec((1, d), lambda b, i, f: (0, 0)),
                  mod_spec(sc), mod_spec(sh), mod_spec(gt),
                  pl.BlockSpec((d, tf), lambda b, i, f: (0, f)),
                  pl.BlockSpec((tf, d), lambda b, i, f: (f, 0))],
        out_specs=pl.BlockSpec((1, tm, d), tok),
        out_shape=jax.ShapeDtypeStruct(x.shape, F32),
        scratch_shapes=[pltpu.VMEM((tm, d), BF16), pltpu.VMEM((tm, d), F32)],
        compiler_params=_params("parallel", "parallel", "arbitrary"),
        name="mlp",
    )(x, g, sc, sh, gt, w_up, w_down)


def _qkv_kernel(x_ref, g_ref, sc_ref, sh_ref, w_ref, qg_ref, kg_ref, q_ref, k_ref, v_ref):
    d = x_ref.shape[2]
    h = _normmod(x_ref[0], g_ref[...], sc_ref[0], sh_ref[0]).astype(BF16)
    gmat = _group_matrix(MXU_DIM, A_HEAD_DIM, 1.0 / A_HEAD_DIM)
    q_scale = A_HEAD_DIM ** -0.5
    for s in range(0, d, MXU_DIM):
        sl = slice(s, s + MXU_DIM)
        q = jnp.dot(h, w_ref[:, s:s + MXU_DIM], preferred_element_type=F32)
        q_ref[0, :, sl] = q * lax.rsqrt(_dot_xe(q * q, gmat) + NORM_EPS) * qg_ref[...] * q_scale
        k = jnp.dot(h, w_ref[:, d + s:d + s + MXU_DIM], preferred_element_type=F32)
        k_ref[0, :, sl] = k * lax.rsqrt(_dot_xe(k * k, gmat) + NORM_EPS) * kg_ref[...]
        v_ref[0, :, sl] = jnp.dot(h, w_ref[:, 2 * d + s:2 * d + s + MXU_DIM],
                                  preferred_element_type=F32)


def _qkv(x, g, sc, sh, w, q_gain, k_gain, tm):
    bsz, t, d = x.shape
    qg = jnp.tile(q_gain, MXU_DIM // A_HEAD_DIM).reshape(1, MXU_DIM)
    kg = jnp.tile(k_gain, MXU_DIM // A_HEAD_DIM).reshape(1, MXU_DIM)
    shp = jax.ShapeDtypeStruct(x.shape, F32)
    return pl.pallas_call(
        _qkv_kernel,
        grid=(bsz, t // tm),
        in_specs=[_tok_spec(tm, d), _const_spec(g), _mod_spec(sc, tm), _mod_spec(sh, tm),
                  _const_spec(w), _const_spec(qg), _const_spec(kg)],
        out_specs=[_tok_spec(tm, d)] * 3,
        out_shape=[shp] * 3,
        compiler_params=_params("parallel", "parallel"),
        name="qkv",
    )(x, g, sc, sh, w, qg, kg)


def _sb_prompt_kernel(bias_ref, q_ref, k_ref, v_ref, o_ref, *, qb):
    hp = pl.program_id(1)
    i = pl.program_id(2)
    q = q_ref[0]
    lane = lax.broadcasted_iota(jnp.int32, (1, LANES), 1)
    row = lax.broadcasted_iota(jnp.int32, (qb, qb), 0)
    col = lax.broadcasted_iota(jnp.int32, (qb, qb), 1)
    suffix = (row >= col).astype(BF16)
    earlier = col < row

    def block(j, carry, masked, qm, bias):
        c_sum, acc = carry
        start = pl.multiple_of(j * qb, qb)
        kb = k_ref[0, pl.ds(start, qb), :].astype(BF16)
        vb = v_ref[0, pl.ds(start, qb), :].astype(BF16)
        z = lax.dot_general(qm, kb, (((1,), (1,)), ((), ())), preferred_element_type=F32) + bias
        sp = _softplus(z)
        if masked:
            sp = jnp.where(earlier, sp, 0.0)
        s_incl = _dot_xe(sp, suffix)
        w = jnp.exp(z - s_incl - c_sum)
        if masked:
            w = jnp.where(earlier, w, 0.0)
        acc = acc + jnp.dot(w.astype(BF16), vb, preferred_element_type=F32)
        return c_sum + s_incl[:, 0:1], acc

    outs = []
    for e in range(2):
        qm = jnp.where(lane // A_HEAD_DIM == e, q, 0.0).astype(BF16)
        bias = bias_ref[2 * hp + e]
        carry = (jnp.zeros((qb, 1), F32), jnp.zeros((qb, LANES), F32))
        carry = block(i, carry, True, qm, bias)
        carry = lax.fori_loop(
            0, i, lambda jj, c, qm=qm, bias=bias: block(i - 1 - jj, c, False, qm, bias), carry)
        outs.append(carry[1])
    o_ref[0] = jnp.where(lane < A_HEAD_DIM, outs[0], outs[1])


def _sb_prompt(q, k, v, logit_bias, qb):
    bsz, t, d = q.shape
    n_pairs = d // LANES
    grid_spec = pltpu.PrefetchScalarGridSpec(
        num_scalar_prefetch=1,
        grid=(bsz, n_pairs, t // qb),
        in_specs=[pl.BlockSpec((1, qb, LANES), lambda b, p, i, bias: (b, i, p)),
                  pl.BlockSpec((1, t, LANES), lambda b, p, i, bias: (b, 0, p)),
                  pl.BlockSpec((1, t, LANES), lambda b, p, i, bias: (b, 0, p))],
        out_specs=pl.BlockSpec((1, qb, LANES), lambda b, p, i, bias: (b, i, p)),
    )
    return pl.pallas_call(
        functools.partial(_sb_prompt_kernel, qb=qb),
        grid_spec=grid_spec,
        out_shape=jax.ShapeDtypeStruct(q.shape, F32),
        compiler_params=_params("parallel", "parallel", "arbitrary"),
        name="sb_prompt",
    )(logit_bias, q, k, v)


def _sb_decode_kernel(pt_ref, qbd_ref, bias_ref, kn_ref, vn_ref, *rest, pages_per_step, page, t_new):
    k_refs = rest[:pages_per_step]
    v_refs = rest[pages_per_step:2 * pages_per_step]
    o_ref = rest[2 * pages_per_step]
    acc_ref, c_ref = rest[2 * pages_per_step + 1:]
    s = pl.program_id(1)
    d = kn_ref.shape[2]
    n_rows = qbd_ref.shape[1]
    qbd = qbd_ref[0]
    bias = bias_ref[...]
    row = lax.broadcasted_iota(jnp.int32, (page, page), 0)
    col = lax.broadcasted_iota(jnp.int32, (page, page), 1)
    suffix = (row >= col).astype(BF16)

    def block(z, pv, mask):
        z = z + bias
        sp = _softplus(z)
        if mask is not None:
            sp = jnp.where(mask, sp, 0.0)
        s_incl = _dot_xe(sp, suffix)
        w = jnp.exp(z - s_incl - c_ref[...])
        if mask is not None:
            w = jnp.where(mask, w, 0.0)
        acc_ref[...] += pv(w.astype(BF16))
        c_ref[...] += s_incl[:, 0:1]

    @pl.when(s == 0)
    def _():
        acc_ref[...] = jnp.zeros_like(acc_ref)
        c_ref[...] = jnp.zeros_like(c_ref)
        pad = jnp.zeros((page - t_new, d), F32)
        k_pad = jnp.concatenate([kn_ref[0], pad], axis=0).astype(BF16)
        v_pad = jnp.concatenate([vn_ref[0], pad], axis=0).astype(BF16)
        q_t = lax.broadcasted_iota(jnp.int32, (n_rows, page), 0) % t_new
        key_i = lax.broadcasted_iota(jnp.int32, (n_rows, page), 1)
        block(_bdot_nt(qbd, k_pad),
              lambda w: jnp.dot(w, v_pad, preferred_element_type=F32), key_i < q_t)

    for r in range(pages_per_step):
        kt = k_refs[r][0, 0].astype(BF16)
        vt = v_refs[r][0, 0].astype(BF16)
        block(jnp.dot(qbd, kt, preferred_element_type=F32),
              lambda w, vt=vt: _bdot_nt(w, vt), None)

    @pl.when(s == pl.num_programs(1) - 1)
    def _():
        n_heads = d // A_HEAD_DIM
        r_head = lax.broadcasted_iota(jnp.int32, (n_rows, d), 0) // t_new
        c_head = lax.broadcasted_iota(jnp.int32, (n_rows, d), 1) // A_HEAD_DIM
        diag = jnp.where(r_head == c_head, acc_ref[...], 0.0)
        out = diag[0:t_new]
        for h in range(1, n_heads):
            out = out + diag[h * t_new:(h + 1) * t_new]
        o_ref[0] = out


def _sb_decode(q, k_new, v_new, cache_k, cache_v, layer, page_table, logit_bias, pages_per_step):
    bsz, t_new, d = q.shape
    n_heads = d // A_HEAD_DIM
    n_layers, n_pool, page = cache_k.shape[:3]
    n_pages = page_table.shape[1]
    n_rows = n_heads * t_new
    ck = cache_k.transpose(0, 1, 3, 4, 2).reshape(n_layers, n_pool, d, page)
    cv = cache_v.transpose(0, 1, 3, 4, 2).reshape(n_layers, n_pool, d, page)
    qh = q.reshape(bsz, t_new, n_heads, A_HEAD_DIM)
    eye = jnp.eye(n_heads, dtype=F32)
    qbd = (qh.transpose(0, 2, 1, 3)[:, :, :, None, :] * eye[None, :, None, :, None])
    qbd = qbd.reshape(bsz, n_rows, d).astype(BF16)
    bias_rows = jnp.broadcast_to(jnp.repeat(logit_bias, t_new)[:, None], (n_rows, page))
    n_steps = n_pages // pages_per_step

    def page_map(r):
        def index(b, s, pt):
            return (layer, pt[b, n_pages - 1 - (s * pages_per_step + r)], 0, 0)
        return index

    kv_specs = [pl.BlockSpec((1, 1, d, page), page_map(r)) for r in range(pages_per_step)]
    grid_spec = pltpu.PrefetchScalarGridSpec(
        num_scalar_prefetch=1,
        grid=(bsz, n_steps),
        in_specs=[pl.BlockSpec((1, n_rows, d), lambda b, s, pt: (b, 0, 0)),
                  pl.BlockSpec((n_rows, page), lambda b, s, pt: (0, 0)),
                  pl.BlockSpec((1, t_new, d), lambda b, s, pt: (b, 0, 0)),
                  pl.BlockSpec((1, t_new, d), lambda b, s, pt: (b, 0, 0))] + kv_specs + kv_specs,
        out_specs=pl.BlockSpec((1, t_new, d), lambda b, s, pt: (b, 0, 0)),
        scratch_shapes=[pltpu.VMEM((n_rows, d), F32), pltpu.VMEM((n_rows, 1), F32)],
    )
    return pl.pallas_call(
        functools.partial(_sb_decode_kernel, pages_per_step=pages_per_step, page=page, t_new=t_new),
        grid_spec=grid_spec,
        out_shape=jax.ShapeDtypeStruct(q.shape, F32),
        compiler_params=_params("parallel", "arbitrary"),
        name="sb_decode",
    )(page_table, qbd, bias_rows, k_new, v_new, *([ck] * pages_per_step), *([cv] * pages_per_step))


def _proj_res_kernel(a_ref, w_ref, x_ref, gt_ref, o_ref):
    o_ref[0] = x_ref[0] + gt_ref[0] * _bdot(a_ref[0], w_ref[...])


def _proj_res(a, w, x, gt, tm):
    bsz, t, d = x.shape
    return pl.pallas_call(
        _proj_res_kernel,
        grid=(bsz, t // tm),
        in_specs=[_tok_spec(tm, d), _const_spec(w), _tok_spec(tm, d), _mod_spec(gt, tm)],
        out_specs=_tok_spec(tm, d),
        out_shape=jax.ShapeDtypeStruct(x.shape, F32),
        compiler_params=_params("parallel", "parallel"),
        name="proj_res",
    )(a, w, x, gt)


def _gdn_out_kernel(o_ref_in, z_ref, on_ref, w_ref, x_ref, gt_ref, o_ref):
    o = o_ref_in[0]
    gmat = _group_matrix(MXU_DIM, B_HEAD_DIM, 1.0 / B_HEAD_DIM)
    ms = _group_reduce(o * o, gmat)
    a = o * lax.rsqrt(ms + NORM_EPS) * on_ref[...] * _silu(z_ref[0])
    o_ref[0] = x_ref[0] + gt_ref[0] * _bdot(a, w_ref[...])


def _gdn_out(o, z, o_norm, w, x, gt, tm):
    bsz, t, d = x.shape
    on = jnp.tile(o_norm, d // B_HEAD_DIM).reshape(1, d)
    return pl.pallas_call(
        _gdn_out_kernel,
        grid=(bsz, t // tm),
        in_specs=[_tok_spec(tm, d), _tok_spec(tm, d), _const_spec(on), _const_spec(w),
                  _tok_spec(tm, d), _mod_spec(gt, tm)],
        out_specs=_tok_spec(tm, d),
        out_shape=jax.ShapeDtypeStruct(x.shape, F32),
        compiler_params=_params("parallel", "parallel"),
        name="gdn_out",
    )(o, z, on, w, x, gt)


def _rwkv_out_kernel(y_ref, r_ref, k_ref, v_ref, gate_ref, rk_ref, lg_ref, lb_ref, w_ref,
                     x_ref, gt_ref, o_ref):
    y = y_ref[0]
    mean_mat = _group_matrix(MXU_DIM, C_HEAD_DIM, 1.0 / C_HEAD_DIM)
    sum_mat = _group_matrix(MXU_DIM, C_HEAD_DIM, 1.0)
    yc = y - _group_reduce(y, mean_mat)
    var = _group_reduce(yc * yc, mean_mat)
    yn = yc * lax.rsqrt(var + GN_EPS) * lg_ref[...] + lb_ref[...]
    bonus = _group_reduce(r_ref[0] * k_ref[0] * rk_ref[...], sum_mat) * v_ref[0]
    a = (yn + bonus) * gate_ref[0]
    o_ref[0] = x_ref[0] + gt_ref[0] * _bdot(a, w_ref[...])


def _rwkv_out(y, r, k, v, gate, r_k, ln_g, ln_b, w, x, gt, tm):
    bsz, t, d = x.shape
    rk = r_k.reshape(1, d)
    lg = ln_g.reshape(1, d)
    lb = ln_b.reshape(1, d)
    tok = _tok_spec(tm, d)
    return pl.pallas_call(
        _rwkv_out_kernel,
        grid=(bsz, t // tm),
        in_specs=[tok, tok, tok, tok, tok, _const_spec(rk), _const_spec(lg), _const_spec(lb),
                  _const_spec(w), tok, _mod_spec(gt, tm)],
        out_specs=tok,
        out_shape=jax.ShapeDtypeStruct(x.shape, F32),
        compiler_params=_params("parallel", "parallel"),
        name="rwkv_out",
    )(y, r, k, v, gate, rk, lg, lb, w, x, gt)


def _gdn_in_kernel(x_ref, g_ref, sc_ref, sh_ref, w_ref, xqkv_ref, z_ref, ba_ref):
    n_conv = xqkv_ref.shape[2]
    n_z = z_ref.shape[2]
    h = _normmod(x_ref[0], g_ref[...], sc_ref[0], sh_ref[0]).astype(BF16)
    for s in range(0, n_conv, MXU_DIM):
        xqkv_ref[0, :, s:s + MXU_DIM] = jnp.dot(h, w_ref[:, s:s + MXU_DIM],
                                                preferred_element_type=F32)
    for s in range(0, n_z, MXU_DIM):
        z_ref[0, :, s:s + MXU_DIM] = jnp.dot(h, w_ref[:, n_conv + s:n_conv + s + MXU_DIM],
                                             preferred_element_type=F32)
    ba_ref[0] = jnp.dot(h, w_ref[:, n_conv + n_z:], preferred_element_type=F32)


def _gdn_in(x, g, sc, sh, w_pad, n_conv, tm):
    bsz, t, d = x.shape
    return pl.pallas_call(
        _gdn_in_kernel,
        grid=(bsz, t // tm),
        in_specs=[_tok_spec(tm, d), _const_spec(g), _mod_spec(sc, tm), _mod_spec(sh, tm),
                  _const_spec(w_pad)],
        out_specs=[_tok_spec(tm, n_conv), _tok_spec(tm, d), _tok_spec(tm, LANES)],
        out_shape=[jax.ShapeDtypeStruct((bsz, t, n_conv), F32),
                   jax.ShapeDtypeStruct((bsz, t, d), F32),
                   jax.ShapeDtypeStruct((bsz, t, LANES), F32)],
        compiler_params=_params("parallel", "parallel"),
        name="gdn_in",
    )(x, g, sc, sh, w_pad)


def _gdn_conv_kernel(*refs, pre_shifted):
    if pre_shifted:
        x_ref, xs_ref, cw_ref, ba_ref, alog_ref, dtb_ref = refs[:6]
    else:
        x_ref, p8_ref, cw_ref, ba_ref, alog_ref, dtb_ref = refs[:6]
    q_ref, k_ref, kb_ref, vb_ref, g_ref = refs[6:]
    i = pl.program_id(1)
    tm = x_ref.shape[1]
    hd = q_ref.shape[2]
    n_heads = hd // B_HEAD_DIM
    x = x_ref[0]
    conv = x * cw_ref[B_CONV - 1:B_CONV, :]
    if pre_shifted:
        for s in range(1, B_CONV):
            conv = conv + xs_ref[s - 1, 0] * cw_ref[B_CONV - 1 - s:B_CONV - s, :]
    else:
        full = jnp.concatenate([p8_ref[0], x], axis=0)
        t_glob = i * tm + lax.broadcasted_iota(jnp.int32, (tm, 1), 0)
        for s in range(1, B_CONV):
            xs = pltpu.roll(full, s, 0)[8:]
            xs = jnp.where(t_glob >= s, xs, 0.0)
            conv = conv + xs * cw_ref[B_CONV - 1 - s:B_CONV - s, :]
    act = _silu(conv)
    r = lax.broadcasted_iota(jnp.int32, (LANES, hd), 0)
    c_head = lax.broadcasted_iota(jnp.int32, (LANES, hd), 1) // B_HEAD_DIM
    e_b = (r == c_head).astype(BF16)
    e_a = (r == c_head + n_heads).astype(BF16)
    ba = ba_ref[0]
    beta = _sigmoid(_dot_xe(ba, e_b))
    g = -jnp.exp(alog_ref[...]) * _softplus(_dot_xe(ba, e_a) + dtb_ref[...])
    g_ref[0] = g
    ones = _group_matrix(MXU_DIM, B_HEAD_DIM, 1.0)
    aq = act[:, :hd]
    ak = act[:, hd:2 * hd]
    q_ref[0] = aq * lax.rsqrt(_group_reduce(aq * aq, ones) + L2_EPS) * (B_HEAD_DIM ** -0.5)
    k = ak * lax.rsqrt(_group_reduce(ak * ak, ones) + L2_EPS)
    k_ref[0] = k
    kb_ref[0] = k * beta
    vb_ref[0] = act[:, 2 * hd:] * beta


def _gdn_conv(xqkv, xs, conv_w, ba, a_log, dt_bias, hd, tm):
    bsz, t, n_conv = xqkv.shape
    alog = jnp.repeat(a_log, B_HEAD_DIM).reshape(1, hd)
    dtb = jnp.repeat(dt_bias, B_HEAD_DIM).reshape(1, hd)
    pre_shifted = xs is not None
    if pre_shifted:
        second = xs
        second_spec = pl.BlockSpec((B_CONV - 1, 1, tm, n_conv), lambda b, i: (0, b, i, 0))
    else:
        second = xqkv
        second_spec = _prev8_spec(tm, n_conv)
    shp = jax.ShapeDtypeStruct((bsz, t, hd), F32)
    return pl.pallas_call(
        functools.partial(_gdn_conv_kernel, pre_shifted=pre_shifted),
        grid=(bsz, t // tm),
        in_specs=[_tok_spec(tm, n_conv), second_spec, _const_spec(conv_w), _tok_spec(tm, LANES),
                  _const_spec(alog), _const_spec(dtb)],
        out_specs=[_tok_spec(tm, hd)] * 5,
        out_shape=[shp] * 5,
        compiler_params=_params("parallel", "parallel"),
        name="gdn_conv",
    )(xqkv, second, conv_w, ba, alog, dtb)


def _gdn_chunk_kernel(q_ref, k_ref, kb_ref, vb_ref, g_ref, s0_ref, o_ref, s_out_ref, s_ref, *, n_chunks):
    i = pl.program_id(2)
    c = CHUNK

    @pl.when(i == 0)
    def _():
        s_ref[...] = s0_ref[0, 0]

    row = lax.broadcasted_iota(jnp.int32, (c, c), 0)
    col = lax.broadcasted_iota(jnp.int32, (c, c), 1)
    lower = row >= col
    strict = row > col
    lower_b = lower.astype(BF16)
    ones_avg = jnp.full((c, LANES), 1.0 / LANES, BF16)

    for n in range(n_chunks):
        sl = slice(n * c, (n + 1) * c)
        q = q_ref[0, sl, :]
        k = k_ref[0, sl, :]
        kb = kb_ref[0, sl, :]
        vb = vb_ref[0, sl, :]
        g_cb = _dot_ex(lower_b, g_ref[0, sl, :])
        gh, gl = _split(g_cb)
        g_row = (lax.dot_general(ones_avg, gh, (((1,), (1,)), ((), ())), preferred_element_type=F32)
                 + lax.dot_general(ones_avg, gl, (((1,), (1,)), ((), ())), preferred_element_type=F32))
        d_g = g_cb[:, :c] - g_row
        decay = jnp.where(lower, jnp.exp(jnp.where(lower, d_g, 0.0)), 0.0)
        a = jnp.where(strict, decay * _bdot_nt(kb, k), 0.0)
        t_inv = _tri_inverse(a, 5)
        e_g = jnp.exp(g_cb)
        w_mat = _dot3(t_inv, kb * e_g)
        u0 = _dot3(t_inv, vb)
        qk = decay * _bdot_nt(q, k)
        g_last = g_cb[c - 1:c, :]
        k_end = jnp.exp(g_last - g_cb) * k
        s = s_ref[...]
        u = u0 - _bdot(w_mat, s)
        o_ref[0, sl, :] = _bdot(q * e_g, s) + _bdot(qk, u)
        s_ref[...] = jnp.exp(g_last) * s + _bdot_tn(k_end, u)

    s_out_ref[0, 0] = s_ref[...]


def _gdn_chunk(q, k, kb, vb, g, s0, ct):
    bsz, t, hd = q.shape
    n_heads = hd // B_HEAD_DIM
    blk = pl.BlockSpec((1, ct, B_HEAD_DIM), lambda b, h, i: (b, i, h))
    st = pl.BlockSpec((1, 1, B_HEAD_DIM, B_HEAD_DIM), lambda b, h, i: (b, h, 0, 0))
    return pl.pallas_call(
        functools.partial(_gdn_chunk_kernel, n_chunks=ct // CHUNK),
        grid=(bsz, n_heads, t // ct),
        in_specs=[blk] * 5 + [st],
        out_specs=[blk, st],
        out_shape=[jax.ShapeDtypeStruct(q.shape, F32), jax.ShapeDtypeStruct(s0.shape, F32)],
        scratch_shapes=[pltpu.VMEM((B_HEAD_DIM, B_HEAD_DIM), F32)],
        compiler_params=_params("parallel", "parallel", "arbitrary"),
        name="gdn_chunk",
    )(q, k, kb, vb, g, s0)


def _rwkv_in_kernel(*refs, seq_len, per_row_shift):
    (x_ref, p8_ref, g_ref, sc_ref, sh_ref, sh0_ref, mu_ref, wrkv_ref, w1_ref, w2_ref, a1_ref, a2_ref,
     g1_ref, g2_ref, w0_ref, a0_ref, kk_ref, ka_ref) = refs[:18]
    r_ref, lw_ref, k_ref, v_ref, kn_ref, bb_ref, gate_ref, hl_ref = refs[18:]
    i = pl.program_id(1)
    tm = x_ref.shape[1]
    g = g_ref[...]
    h = _normmod(x_ref[0], g, sc_ref[0], sh_ref[0])
    if per_row_shift:
        sc8, sh8 = sc_ref[0, 0:8], sh_ref[0, 0:8]
    else:
        sc8, sh8 = sc_ref[0], sh_ref[0]
    h8 = _normmod(p8_ref[0], g, sc8, sh8)
    prev = pltpu.roll(jnp.concatenate([h8, h], axis=0), 1, 0)[8:]
    t_glob = i * tm + lax.broadcasted_iota(jnp.int32, (tm, 1), 0)
    prev = jnp.where(t_glob % seq_len == 0, sh0_ref[0], prev)
    xx = prev - h
    mix = lambda n: (h + xx * mu_ref[n:n + 1, :]).astype(BF16)
    r = jnp.dot(mix(0), wrkv_ref[0], preferred_element_type=F32)
    k = jnp.dot(mix(1), wrkv_ref[1], preferred_element_type=F32)
    v_ref[0] = jnp.dot(mix(2), wrkv_ref[2], preferred_element_type=F32)
    r_ref[0] = r
    wl = w0_ref[...] + _bdot(jnp.tanh(jnp.dot(mix(3), w1_ref[...], preferred_element_type=F32)),
                             w2_ref[...])
    lw_ref[0] = -jnp.exp(-_softplus(-wl) - 0.5)
    a = _sigmoid(a0_ref[...] + _bdot(jnp.dot(mix(4), a1_ref[...], preferred_element_type=F32),
                                     a2_ref[...]))
    gate_ref[0] = _bdot(_sigmoid(jnp.dot(mix(5), g1_ref[...], preferred_element_type=F32)),
                        g2_ref[...])
    kx = k * kk_ref[...]
    ones = _group_matrix(MXU_DIM, C_HEAD_DIM, 1.0)
    kn = kx * lax.rsqrt(_group_reduce(kx * kx, ones) + L2_EPS)
    kn_ref[0] = kn
    bb_ref[0] = kn * a
    k_ref[0] = k * (1.0 + (a - 1.0) * ka_ref[...])
    hl_ref[0, 0] = h[tm - 8:tm]


def _rwkv_in(x, g, sc, sh, shift0, seq_len, P, tm):
    bsz, t, d = x.shape
    per_row = shift0.shape[1] != 1
    consts = [P['mu'], P['w_rkv'], P['w1'], P['w2'], P['a1'], P['a2'], P['g1'], P['g2'],
              P['w0'], P['a0'], P['k_k'], P['k_a']]
    shp = jax.ShapeDtypeStruct(x.shape, F32)
    n_t = t // tm
    outs = pl.pallas_call(
        functools.partial(_rwkv_in_kernel, seq_len=seq_len, per_row_shift=per_row),
        grid=(bsz, n_t),
        in_specs=[_tok_spec(tm, d), _prev8_spec(tm, d), _const_spec(g), _mod_spec(sc, tm),
                  _mod_spec(sh, tm), _mod_spec(shift0, tm)] + [_const_spec(c) for c in consts],
        out_specs=[_tok_spec(tm, d)] * 7 + [pl.BlockSpec((1, 1, 8, d), lambda b, i: (b, i, 0, 0))],
        out_shape=[shp] * 7 + [jax.ShapeDtypeStruct((bsz, n_t, 8, d), F32)],
        compiler_params=_params("parallel", "parallel"),
        name="rwkv_in",
    )(x, x, g, sc, sh, shift0, *consts)
    return outs


def _rwkv_scan_kernel(r_ref, lw_ref, k_ref, v_ref, kn_ref, bb_ref, s0_ref, y_ref, s_out_ref, s_ref,
                      *, n_chunks):
    i = pl.program_id(2)
    c = CHUNK
    n = 2 * c

    @pl.when(i == 0)
    def _():
        s_ref[...] = s0_ref[0, 0]

    lane = lax.broadcasted_iota(jnp.int32, (1, LANES), 1)
    row = lax.broadcasted_iota(jnp.int32, (n, n), 0)
    col = lax.broadcasted_iota(jnp.int32, (n, n), 1)
    same = (row // c) == (col // c)
    incl = same & (row >= col)
    strict = same & (row > col)
    incl_b = incl.astype(BF16)

    def stack(x):
        return jnp.concatenate([jnp.where(lane < C_HEAD_DIM, x, 0.0),
                                jnp.where(lane >= C_HEAD_DIM, x, 0.0)], axis=0)

    for m in range(n_chunks):
        sl = slice(m * c, (m + 1) * c)
        lw = stack(lw_ref[0, sl, :])
        lc = _dot_ex(incl_b, lw)
        lc_last = lc[c - 1:c, :] + lc[n - 1:n, :]
        p_in = jnp.exp(lc)
        p_inv = jnp.exp(-lc)
        p_end = jnp.exp(lc_last - lc)
        kn_t = stack(kn_ref[0, sl, :]) * jnp.exp(lc - lw)
        k_s = stack(k_ref[0, sl, :])
        b_s = stack(bb_ref[0, sl, :])
        v_s = stack(v_ref[0, sl, :])
        k_t = k_s * p_inv
        b_t = b_s * p_inv
        r_t = stack(r_ref[0, sl, :]) * p_in
        a_kb = jnp.where(strict, _bdot_nt(kn_t, b_t), 0.0)
        a_kk = jnp.where(strict, _bdot_nt(kn_t, k_t), 0.0)
        a_rk = jnp.where(incl, _bdot_nt(r_t, k_t), 0.0)
        a_rb = jnp.where(incl, _bdot_nt(r_t, b_t), 0.0)
        t_inv = _tri_inverse(a_kb, 5)
        s = s_ref[...]
        u = _dot3(t_inv, _bdot_nt(kn_t, s) + _bdot(a_kk, v_s))
        y = _bdot_nt(r_t, s) + _bdot(a_rk, v_s) - _bdot(a_rb, u)
        y_ref[0, sl, :] = y[:c] + y[c:]
        s_ref[...] = s * jnp.exp(lc_last) + _bdot_tn(v_s, k_s * p_end) - _bdot_tn(u, b_s * p_end)

    s_out_ref[0, 0] = s_ref[...]


def _rwkv_scan(r, lw, k, v, kn, bb, s0_pairs, ct):
    bsz, t, d = r.shape
    n_pairs = d // LANES
    blk = pl.BlockSpec((1, ct, LANES), lambda b, p, i: (b, i, p))
    st = pl.BlockSpec((1, 1, LANES, LANES), lambda b, p, i: (b, p, 0, 0))
    return pl.pallas_call(
        functools.partial(_rwkv_scan_kernel, n_chunks=ct // CHUNK),
        grid=(bsz, n_pairs, t // ct),
        in_specs=[blk] * 6 + [st],
        out_specs=[blk, st],
        out_shape=[jax.ShapeDtypeStruct(r.shape, F32), jax.ShapeDtypeStruct(s0_pairs.shape, F32)],
        scratch_shapes=[pltpu.VMEM((LANES, LANES), F32)],
        compiler_params=_params("parallel", "parallel", "arbitrary"),
        name="rwkv_scan",
    )(r, lw, k, v, kn, bb, s0_pairs)


def _pad_time(a, t_pad):
    return jnp.pad(a, ((0, 0), (0, t_pad - a.shape[1]), (0, 0)))


def _pairs_from_heads(s):
    bsz, n_h, n, _ = s.shape
    sp = s.reshape(bsz, n_h // 2, 2, n, n)
    z = jnp.zeros_like(sp[:, :, 0])
    top = jnp.concatenate([sp[:, :, 0], z], axis=-1)
    bot = jnp.concatenate([z, sp[:, :, 1]], axis=-1)
    return jnp.concatenate([top, bot], axis=-2)


def _heads_from_pairs(sp):
    bsz, n_p, n2, _ = sp.shape
    n = n2 // 2
    return jnp.stack([sp[:, :, :n, :n], sp[:, :, n:, n:]], axis=2).reshape(bsz, 2 * n_p, n, n)


def _run_trunk(x, mods, W, tm, seq_len, cache, state):
    bsz, t, d = x.shape
    n_seq = bsz * t // seq_len
    depth = len(mods)
    ks, vs, bconv, bssm, cshift, cwkv = [], [], [], [], [], []
    for l in range(depth):
        kind, j = l % 3, l // 3
        sh1, sc1, gt1, sh2, sc2, gt2 = mods[l]
        g_mix = W['ln_mix'][l].reshape(1, d)
        if kind == 0:
            q, k, v = _qkv(x, g_mix, sc1, sh1, W['a_w_qkv'][j], W['a_q_norm'][j], W['a_k_norm'][j], tm)
            if cache is None:
                o = _sb_prompt(q, k, v, W['a_logit_bias'][j], qb=256)
            else:
                cache_k, cache_v, page_table = cache
                o = _sb_decode(q.reshape(n_seq, seq_len, d), k.reshape(n_seq, seq_len, d),
                               v.reshape(n_seq, seq_len, d), cache_k, cache_v, j, page_table,
                               W['a_logit_bias'][j], pages_per_step=4).reshape(bsz, t, d)
            ks.append(k.reshape(n_seq, seq_len, d // A_HEAD_DIM, A_HEAD_DIM))
            vs.append(v.reshape(n_seq, seq_len, d // A_HEAD_DIM, A_HEAD_DIM))
            x = _proj_res(o, W['a_w_o'][j], x, gt1, tm)
        elif kind == 1:
            n_conv = 3 * d
            xqkv, z, ba = _gdn_in(x, g_mix, sc1, sh1, W['b_w_in'][j], n_conv, tm)
            xq_seq = xqkv.reshape(n_seq, seq_len, n_conv)
            if state is None:
                xs = None
                bconv.append(xq_seq[:, seq_len - (B_CONV - 1):])
                s0 = jnp.zeros((n_seq, d // B_HEAD_DIM, B_HEAD_DIM, B_HEAD_DIM), F32)
            else:
                xc = jnp.concatenate([state['b_conv'][j], xq_seq], axis=1)
                xs = jnp.stack([xc[:, B_CONV - 1 - s:B_CONV - 1 - s + seq_len].reshape(bsz, t, n_conv)
                                for s in range(1, B_CONV)])
                bconv.append(xc[:, seq_len:])
                s0 = state['b_ssm'][j]
            q, k, kb, vb, g = _gdn_conv(xqkv, xs, W['b_conv'][j], ba, W['b_a_log'][j],
                                        W['b_dt_bias'][j], d, min(tm, 256))
            t_pad = -(-seq_len // CHUNK) * CHUNK
            seqs = [_pad_time(a.reshape(n_seq, seq_len, d), t_pad) for a in (q, k, kb, vb, g)]
            o, s_new = _gdn_chunk(*seqs, s0, ct=min(t_pad, 256))
            bssm.append(s_new)
            o = o[:, :seq_len].reshape(bsz, t, d)
            x = _gdn_out(o, z, W['b_o_norm'][j], W['b_w_o'][j], x, gt1, tm)
        else:
            if state is None:
                shift0 = jnp.zeros((bsz, 1, d), F32)
                s0 = jnp.zeros((n_seq, d // C_HEAD_DIM, C_HEAD_DIM, C_HEAD_DIM), F32)
            else:
                shift0 = jnp.repeat(state['c_shift'][j], seq_len, axis=0).reshape(bsz, t, d)
                s0 = state['c_wkv'][j]
            tm_c = min(tm, 256)
            r, lw, k, v, kn, bb, gate, h_last = _rwkv_in(x, g_mix, sc1, sh1, shift0, seq_len,
                                                        W['c'][j], tm_c)
            if state is None:
                cshift.append(h_last[:, -1, 7])
            else:
                assert t == tm_c and seq_len == 8
                cshift.append(_last_rows(x, g_mix, sc1, sh1, seq_len))
            t_pad = -(-seq_len // CHUNK) * CHUNK
            seqs = [_pad_time(a.reshape(n_seq, seq_len, d), t_pad) for a in (r, lw, k, v, kn, bb)]
            y, s_new = _rwkv_scan(*seqs, _pairs_from_heads(s0), ct=min(t_pad, 256))
            cwkv.append(_heads_from_pairs(s_new))
            y = y[:, :seq_len].reshape(bsz, t, d)
            x = _rwkv_out(y, r, k, v, gate, W['c_r_k'][j], W['c_ln_g'][j], W['c_ln_b'][j],
                          W['c_w_o'][j], x, gt1, tm)
        x = _mlp(x, W['ln_mlp'][l].reshape(1, d), sc2, sh2, gt2, W['w_up'][l], W['w_down'][l], tm)
    return x, ks, vs, bconv, bssm, cshift, cwkv


def _hmod_kernel(x_ref, g_ref, sc_ref, sh_ref, o_ref):
    o_ref[0] = _normmod(x_ref[0], g_ref[...], sc_ref[0], sh_ref[0])


def _last_rows(x, g, sc, sh, seq_len):
    bsz, t, d = x.shape
    n_seq = bsz * t // seq_len
    pick = lambda a: a.reshape(n_seq, seq_len, d)[:, seq_len - 1].reshape(1, n_seq, d)
    xs, scs, shs = pick(x), pick(sc), pick(sh)
    return pl.pallas_call(
        _hmod_kernel,
        grid=(1, 1),
        in_specs=[_tok_spec(n_seq, d), _const_spec(g), _tok_spec(n_seq, d), _tok_spec(n_seq, d)],
        out_specs=_tok_spec(n_seq, d),
        out_shape=jax.ShapeDtypeStruct((1, n_seq, d), F32),
        compiler_params=_params("parallel", "parallel"),
        name="last_rows",
    )(xs, g, scs, shs)[0]


def kernel(x_prompt, x_sample, c_prompt, c_sample, cache_k, cache_v, page_table, state_b_conv, state_b_ssm, state_c_shift, state_c_wkv, ln_mix, ln_mlp, w_ada, b_ada, w_up, w_down, a_w_qkv, a_q_norm, a_k_norm, a_logit_bias, a_w_o, b_w_in, b_conv, b_a_log, b_dt_bias, b_o_norm, b_w_o, c_mu, c_w_rkv, c_w0, c_w1, c_w2, c_a0, c_a1, c_a2, c_g1, c_g2, c_k_k, c_k_a, c_r_k, c_ln_g, c_ln_b, c_w_o):
    bp, t_p, d = x_prompt.shape
    bs, t_s, _ = x_sample.shape
    depth = ln_mix.shape[0]
    n_c = c_mu.shape[0]

    def lora_cols(w):
        return jnp.pad(w, ((0, 0), (0, 0), (0, LANES - w.shape[2]))).astype(BF16)

    def lora_rows(w):
        return jnp.pad(w, ((0, 0), (0, LANES - w.shape[1]), (0, 0))).astype(BF16)

    b_in = b_w_in.shape[2]
    b_in_pad = -(-b_in // LANES) * LANES
    W = {
        'ln_mix': ln_mix, 'ln_mlp': ln_mlp,
        'w_up': w_up.astype(BF16), 'w_down': w_down.astype(BF16),
        'a_w_qkv': a_w_qkv.astype(BF16), 'a_q_norm': a_q_norm, 'a_k_norm': a_k_norm,
        'a_logit_bias': a_logit_bias, 'a_w_o': a_w_o.astype(BF16),
        'b_w_in': jnp.pad(b_w_in, ((0, 0), (0, 0), (0, b_in_pad - b_in))).astype(BF16),
        'b_conv': b_conv, 'b_a_log': b_a_log, 'b_dt_bias': b_dt_bias, 'b_o_norm': b_o_norm,
        'b_w_o': b_w_o.astype(BF16),
        'c_r_k': c_r_k.reshape(n_c, d), 'c_ln_g': c_ln_g, 'c_ln_b': c_ln_b, 'c_w_o': c_w_o.astype(BF16),
    }
    w1, w2 = lora_cols(c_w1), lora_rows(c_w2)
    a1, a2 = lora_cols(c_a1), lora_rows(c_a2)
    g1, g2 = c_g1.astype(BF16), c_g2.astype(BF16)
    rkv = c_w_rkv.astype(BF16)
    W['c'] = [{'mu': c_mu[j], 'w_rkv': rkv[j], 'w1': w1[j], 'w2': w2[j], 'a1': a1[j], 'a2': a2[j],
               'g1': g1[j], 'g2': g2[j], 'w0': c_w0[j].reshape(1, d), 'a0': c_a0[j].reshape(1, d),
               'k_k': c_k_k[j].reshape(1, d), 'k_a': c_k_a[j].reshape(1, d)} for j in range(n_c)]

    mod_all = _ada(jnp.concatenate([c_prompt, c_sample], axis=0), w_ada, b_ada)
    mods_p, mods_s = [], []
    for l in range(depth):
        mp = mod_all[l, :bp].reshape(bp, 1, 6, d)
        mods_p.append([mp[:, :, n] for n in range(6)])
        ms = jnp.repeat(mod_all[l, bp:], t_s, axis=0).reshape(1, bs * t_s, 6, d)
        mods_s.append([ms[:, :, n] for n in range(6)])

    y_p, k_p, v_p, bconv_p, bssm_p, cshift_p, cwkv_p = _run_trunk(
        x_prompt, mods_p, W, 512, t_p, None, None)
    state = {'b_conv': state_b_conv, 'b_ssm': state_b_ssm, 'c_shift': state_c_shift,
             'c_wkv': state_c_wkv}
    y_s, k_s, v_s, bconv_s, bssm_s, cshift_s, cwkv_s = _run_trunk(
        x_sample.reshape(1, bs * t_s, d), mods_s, W, bs * t_s, t_s,
        (cache_k, cache_v, page_table), state)
    st = jnp.stack
    return (y_p, y_s.reshape(bs, t_s, d), st(k_p), st(v_p), st(k_s), st(v_s), st(bconv_p), st(bssm_p),
            st(bconv_s), st(bssm_s), st(cshift_p), st(cwkv_p), st(cshift_s), st(cwkv_s))
```

```python
import functools

import jax
import jax.numpy as jnp
from jax import lax
from jax.experimental import pallas as pl
from jax.experimental.pallas import tpu as pltpu

F32 = jnp.float32
BF16 = jnp.bfloat16

LANES = 128
MXU_DIM = 256
VMEM_LIMIT_BYTES = 56 * 1024 * 1024

NORM_EPS = 1e-6
L2_EPS = 1e-6
GN_EPS = 64e-5
LOG2E = 1.4426950408889634
A_HEAD_DIM = 64
B_HEAD_DIM = 128
C_HEAD_DIM = 64
B_CONV = 4
CHUNK = 64
CHAINS_PER_STEP = 8


def _params(*sem):
    return pltpu.CompilerParams(dimension_semantics=sem, vmem_limit_bytes=VMEM_LIMIT_BYTES)


def _bdot(a, b):
    return jnp.dot(a.astype(BF16), b.astype(BF16), preferred_element_type=F32)


def _bdot_nt(a, b):
    return lax.dot_general(a.astype(BF16), b.astype(BF16), (((1,), (1,)), ((), ())),
                           preferred_element_type=F32)


def _bdot_tn(a, b):
    return lax.dot_general(a.astype(BF16), b.astype(BF16), (((0,), (0,)), ((), ())),
                           preferred_element_type=F32)


def _split(x):
    hi = x.astype(BF16)
    lo = (x - hi.astype(F32)).astype(BF16)
    return hi, lo


def _sum_dots(lhs, rhs):
    if lhs[0].shape[1] % LANES == 0:
        return jnp.dot(jnp.concatenate(lhs, axis=1), jnp.concatenate(rhs, axis=0),
                       preferred_element_type=F32)
    out = jnp.dot(lhs[0], rhs[0], preferred_element_type=F32)
    for a, b in zip(lhs[1:], rhs[1:]):
        out = out + jnp.dot(a, b, preferred_element_type=F32)
    return out


def _dot_xe(x, e):
    hi, lo = _split(x)
    return _sum_dots([hi, lo], [e, e])


def _dot_ex(e, x):
    hi, lo = _split(x)
    return _sum_dots([e, e], [hi, lo])


def _dot3(a, b):
    ah, al = _split(a)
    bh, bl = _split(b)
    return _sum_dots([ah, ah, al], [bh, bl, bh])


def _sigmoid(x):
    return 1.0 / (1.0 + jnp.exp(-x))


def _silu(x):
    return x * _sigmoid(x)


def _softplus(x):
    return jnp.maximum(x, 0.0) + jnp.log(1.0 + jnp.exp(-jnp.abs(x)))


def _neg_abs(x):
    bits = lax.bitcast_convert_type(x, jnp.uint32) | jnp.uint32(0x80000000)
    return lax.bitcast_convert_type(bits, F32)


def _sb_suffix(z, suffix2, mask):
    sp = jnp.maximum(z, 0.0) + jnp.log(1.0 + jnp.exp2(_neg_abs(z))) * LOG2E
    if mask is not None:
        sp = jnp.where(mask, sp, 0.0)
    hi, lo = _split(sp)
    return jnp.dot(jnp.concatenate([hi, lo], axis=1), suffix2, preferred_element_type=F32)


def _sb_local(z, s_incl, mask):
    w = jnp.exp2(z - s_incl)
    if mask is not None:
        w = jnp.where(mask, w, 0.0)
    return w.astype(BF16)


def _sb_weights(z, suffix2, mask):
    s_incl = _sb_suffix(z, suffix2, mask)
    return _sb_local(z, s_incl, mask), s_incl[:, 0:1]


def _normmod(x, g, sc, sh):
    ms = jnp.mean(x * x, axis=-1, keepdims=True)
    return x * lax.rsqrt(ms + NORM_EPS) * g * (1.0 + sc) + sh


def _group_matrix(n, group, value):
    r = lax.broadcasted_iota(jnp.int32, (n, n), 0) // group
    c = lax.broadcasted_iota(jnp.int32, (n, n), 1) // group
    return jnp.where(r == c, value, 0.0).astype(BF16)


def _group_reduce(x, gmat):
    n = gmat.shape[0]
    parts = [_dot_xe(x[:, s:s + n], gmat) for s in range(0, x.shape[1], n)]
    return parts[0] if len(parts) == 1 else jnp.concatenate(parts, axis=1)


def _tri_inverses(mats, n_steps):
    n = mats[0].shape[0]
    eye = (lax.broadcasted_iota(jnp.int32, (n, n), 0)
           == lax.broadcasted_iota(jnp.int32, (n, n), 1)).astype(F32)
    ps = [-a for a in mats]
    ts = [eye + p for p in ps]
    for _ in range(n_steps):
        ps = [_dot3(p, p) for p in ps]
        ts = [t + _dot3(t, p) for t, p in zip(ts, ps)]
    return ts


def _ada_kernel(c_ref, w_ref, b_ref, o_ref):
    ca = _silu(c_ref[...])
    o_ref[0] = _bdot(ca, w_ref[0]) + b_ref[0]


def _ada(c_all, w_ada, b_ada):
    depth, d, n = w_ada.shape
    rows = c_all.shape[0]
    tn = 1536
    return pl.pallas_call(
        _ada_kernel,
        grid=(depth, n // tn),
        in_specs=[pl.BlockSpec((rows, d), lambda l, j: (0, 0)),
                  pl.BlockSpec((1, d, tn), lambda l, j: (l, 0, j)),
                  pl.BlockSpec((1, 1, tn), lambda l, j: (l, 0, j))],
        out_specs=pl.BlockSpec((1, rows, tn), lambda l, j: (l, 0, j)),
        out_shape=jax.ShapeDtypeStruct((depth, rows, n), F32),
        compiler_params=_params("parallel", "parallel"),
        name="ada",
    )(c_all, w_ada, b_ada.reshape(depth, 1, n))


def _tok_spec(tm, width):
    return pl.BlockSpec((1, tm, width), lambda b, i: (b, i, 0))


def _mod_spec(mod, tm):
    if mod.shape[1] == 1:
        return pl.BlockSpec((1, 1, mod.shape[2]), lambda b, i: (b, 0, 0))
    return pl.BlockSpec((1, tm, mod.shape[2]), lambda b, i: (b, i, 0))


def _const_spec(arr):
    nd = arr.ndim
    return pl.BlockSpec(arr.shape, lambda b, i: (0,) * nd)


def _prev8_spec(tm, width):
    blocks = tm // 8
    return pl.BlockSpec((1, 8, width), lambda b, i: (b, jnp.maximum(i * blocks - 1, 0), 0))


def _mlp_kernel(x_ref, g_ref, sc_ref, sh_ref, gt_ref, wu_ref, wd_ref, o_ref, h_ref, acc_ref):
    f = pl.program_id(2)

    @pl.when(f == 0)
    def _():
        h_ref[...] = _normmod(x_ref[0], g_ref[...], sc_ref[0], sh_ref[0]).astype(BF16)
        acc_ref[...] = jnp.zeros_like(acc_ref)

    u = jnp.maximum(jnp.dot(h_ref[...], wu_ref[...], preferred_element_type=F32), 0.0)
    acc_ref[...] += jnp.dot((u * u).astype(BF16), wd_ref[...], preferred_element_type=F32)

    @pl.when(f == pl.num_programs(2) - 1)
    def _():
        o_ref[0] = x_ref[0] + gt_ref[0] * acc_ref[...]


def _mlp(x, g, sc, sh, gt, w_up, w_down, tm):
    bsz, t, d = x.shape
    ff = w_up.shape[1]
    tf = 1024

    def tok(b, i, f):
        return (b, i, 0)

    def mod_spec(m):
        if m.shape[1] == 1:
            return pl.BlockSpec((1, 1, d), lambda b, i, f: (b, 0, 0))
        return pl.BlockSpec((1, tm, d), tok)

    return pl.pallas_call(
        _mlp_kernel,
        grid=(bsz, t // tm, ff // tf),
        in_specs=[pl.BlockSpec((1, tm, d), tok),
                  pl.BlockSpec((1, d), lambda b, i, f: (0, 0)),
                  mod_spec(sc), mod_spec(sh), mod_spec(gt),
                  pl.BlockSpec((d, tf), lambda b, i, f: (0, f)),
                  pl.BlockSpec((tf, d), lambda b, i, f: (f, 0))],
        out_specs=pl.BlockSpec((1, tm, d), tok),
        out_shape=jax.ShapeDtypeStruct(x.shape, F32),
        scratch_shapes=[pltpu.VMEM((tm, d), BF16), pltpu.VMEM((tm, d), F32)],
        compiler_params=_params("parallel", "parallel", "arbitrary"),
        name="mlp",
    )(x, g, sc, sh, gt, w_up, w_down)


def _qkv_kernel(x_ref, g_ref, sc_ref, sh_ref, w_ref, qg_ref, kg_ref, q_ref, k_ref, v_ref):
    d = x_ref.shape[2]
    h = _normmod(x_ref[0], g_ref[...], sc_ref[0], sh_ref[0]).astype(BF16)
    gmat = _group_matrix(MXU_DIM, A_HEAD_DIM, 1.0 / A_HEAD_DIM)
    q_scale = A_HEAD_DIM ** -0.5 * LOG2E
    for s in range(0, d, MXU_DIM):
        sl = slice(s, s + MXU_DIM)
        q = jnp.dot(h, w_ref[:, s:s + MXU_DIM], preferred_element_type=F32)
        q_ref[0, :, sl] = q * lax.rsqrt(_dot_xe(q * q, gmat) + NORM_EPS) * qg_ref[...] * q_scale
        k = jnp.dot(h, w_ref[:, d + s:d + s + MXU_DIM], preferred_element_type=F32)
        k_ref[0, :, sl] = k * lax.rsqrt(_dot_xe(k * k, gmat) + NORM_EPS) * kg_ref[...]
        v_ref[0, :, sl] = jnp.dot(h, w_ref[:, 2 * d + s:2 * d + s + MXU_DIM],
                                  preferred_element_type=F32)


def _qkv(x, g, sc, sh, w, q_gain, k_gain, tm):
    bsz, t, d = x.shape
    qg = jnp.tile(q_gain, MXU_DIM // A_HEAD_DIM).reshape(1, MXU_DIM)
    kg = jnp.tile(k_gain, MXU_DIM // A_HEAD_DIM).reshape(1, MXU_DIM)
    shp = jax.ShapeDtypeStruct(x.shape, F32)
    return pl.pallas_call(
        _qkv_kernel,
        grid=(bsz, t // tm),
        in_specs=[_tok_spec(tm, d), _const_spec(g), _mod_spec(sc, tm), _mod_spec(sh, tm),
                  _const_spec(w), _const_spec(qg), _const_spec(kg)],
        out_specs=[_tok_spec(tm, d)] * 3,
        out_shape=[shp] * 3,
        compiler_params=_params("parallel", "parallel"),
        name="qkv",
    )(x, g, sc, sh, w, qg, kg)


def _sb_prompt_kernel(bias_ref, q_ref, k_ref, v_ref, o_ref, z_ref, s_ref, acc_ref, c_ref, *, qb):
    hp = pl.program_id(1)
    i = pl.program_id(2)
    q = q_ref[0]
    lane = lax.broadcasted_iota(jnp.int32, (1, LANES), 1)
    row = lax.broadcasted_iota(jnp.int32, (qb, qb), 0)
    col = lax.broadcasted_iota(jnp.int32, (qb, qb), 1)
    suffix = (row >= col).astype(BF16)
    suffix2 = jnp.concatenate([suffix, suffix], axis=0)
    earlier = col < row
    qm = [jnp.where(lane // A_HEAD_DIM == e, q, 0.0).astype(BF16) for e in range(2)]
    bias = [bias_ref[2 * hp + e] * LOG2E for e in range(2)]

    def rows(t):
        return pl.ds(pl.multiple_of(jnp.maximum(i - t, 0) * qb, qb), qb)

    def logits(t):
        kb = k_ref[0, rows(t), :].astype(BF16)
        return [_bdot_nt(qm[e], kb) + bias[e] for e in range(2)]

    def prepare(zs, slot, mask):
        for e in range(2):
            z_ref[slot, e] = zs[e]
            s_ref[slot, e] = _sb_suffix(zs[e], suffix2, mask)

    def consume(t, slot, mask):
        vb = v_ref[0, rows(t), :].astype(BF16)
        for e in range(2):
            s_incl = s_ref[slot, e]
            p = jnp.dot(_sb_local(z_ref[slot, e], s_incl, mask), vb, preferred_element_type=F32)
            acc_ref[e] += jnp.exp2(-c_ref[e]) * p
            c_ref[e] += s_incl[:, 0:1]

    def trip(t, slot, mask=None, last=False):
        if last:
            consume(t, slot, mask)
            return
        zs = logits(t + 1)
        consume(t, slot, mask)
        prepare(zs, 1 - slot, None)

    acc_ref[...] = jnp.zeros_like(acc_ref)
    c_ref[...] = jnp.zeros_like(c_ref)
    prepare(logits(0), 0, earlier)
    trip(0, 0, earlier)

    def pair(u, _):
        trip(2 * u + 1, 1)
        trip(2 * u + 2, 0)
        return 0

    lax.fori_loop(0, i // 2, pair, 0)

    @pl.when(i % 2 == 1)
    def _():
        trip(i, 1, last=True)

    o_ref[0] = jnp.where(lane < A_HEAD_DIM, acc_ref[0], acc_ref[1])


def _sb_prompt(q, k, v, logit_bias, qb):
    bsz, t, d = q.shape
    n_pairs = d // LANES
    grid_spec = pltpu.PrefetchScalarGridSpec(
        num_scalar_prefetch=1,
        grid=(bsz, n_pairs, t // qb),
        in_specs=[pl.BlockSpec((1, qb, LANES), lambda b, p, i, bias: (b, i, p)),
                  pl.BlockSpec((1, t, LANES), lambda b, p, i, bias: (b, 0, p)),
                  pl.BlockSpec((1, t, LANES), lambda b, p, i, bias: (b, 0, p))],
        out_specs=pl.BlockSpec((1, qb, LANES), lambda b, p, i, bias: (b, i, p)),
        scratch_shapes=[pltpu.VMEM((2, 2, qb, qb), F32), pltpu.VMEM((2, 2, qb, qb), F32),
                        pltpu.VMEM((2, qb, LANES), F32), pltpu.VMEM((2, qb, 1), F32)],
    )
    return pl.pallas_call(
        functools.partial(_sb_prompt_kernel, qb=qb),
        grid_spec=grid_spec,
        out_shape=jax.ShapeDtypeStruct(q.shape, F32),
        compiler_params=_params("parallel", "parallel", "arbitrary"),
        name="sb_prompt",
    )(logit_bias, q, k, v)


def _sb_decode_kernel(pt_ref, qbd_ref, bias_ref, kn_ref, vn_ref, *rest, pages_per_step, page, t_new):
    k_refs = rest[:pages_per_step]
    v_refs = rest[pages_per_step:2 * pages_per_step]
    o_ref = rest[2 * pages_per_step]
    acc_ref, c_ref = rest[2 * pages_per_step + 1:]
    s = pl.program_id(1)
    d = kn_ref.shape[2]
    n_rows = qbd_ref.shape[1]
    qbd = qbd_ref[0]
    bias = bias_ref[...]
    row = lax.broadcasted_iota(jnp.int32, (page, page), 0)
    col = lax.broadcasted_iota(jnp.int32, (page, page), 1)
    suffix = (row >= col).astype(BF16)
    suffix2 = jnp.concatenate([suffix, suffix], axis=0)

    def accumulate(p, tot):
        acc_ref[...] += jnp.exp2(-c_ref[...]) * p
        c_ref[...] += tot

    @pl.when(s == 0)
    def _():
        acc_ref[...] = jnp.zeros_like(acc_ref)
        c_ref[...] = jnp.zeros_like(c_ref)
        pad = jnp.zeros((page - t_new, d), F32)
        k_pad = jnp.concatenate([kn_ref[0], pad], axis=0).astype(BF16)
        v_pad = jnp.concatenate([vn_ref[0], pad], axis=0).astype(BF16)
        q_t = lax.broadcasted_iota(jnp.int32, (n_rows, page), 0) % t_new
        key_i = lax.broadcasted_iota(jnp.int32, (n_rows, page), 1)
        w, tot = _sb_weights(_bdot_nt(qbd, k_pad) + bias, suffix2, key_i < q_t)
        accumulate(jnp.dot(w, v_pad, preferred_element_type=F32), tot)

    zs = [jnp.dot(qbd, k_refs[r][0, 0].astype(BF16), preferred_element_type=F32) + bias
          for r in range(pages_per_step)]
    ws = [_sb_weights(z, suffix2, None) for z in zs]
    ps = [_bdot_nt(w, v_refs[r][0, 0]) for r, (w, _) in enumerate(ws)]
    for p, (_, tot) in zip(ps, ws):
        accumulate(p, tot)

    @pl.when(s == pl.num_programs(1) - 1)
    def _():
        n_heads = d // A_HEAD_DIM
        r_head = lax.broadcasted_iota(jnp.int32, (n_rows, d), 0) // t_new
        c_head = lax.broadcasted_iota(jnp.int32, (n_rows, d), 1) // A_HEAD_DIM
        diag = jnp.where(r_head == c_head, acc_ref[...], 0.0)
        out = diag[0:t_new]
        for h in range(1, n_heads):
            out = out + diag[h * t_new:(h + 1) * t_new]
        o_ref[0] = out


def _sb_decode(q, k_new, v_new, cache_k, cache_v, layer, page_table, logit_bias, pages_per_step):
    bsz, t_new, d = q.shape
    n_heads = d // A_HEAD_DIM
    n_layers, n_pool, page = cache_k.shape[:3]
    n_pages = page_table.shape[1]
    n_rows = n_heads * t_new
    ck = cache_k.transpose(0, 1, 3, 4, 2).reshape(n_layers, n_pool, d, page)
    cv = cache_v.transpose(0, 1, 3, 4, 2).reshape(n_layers, n_pool, d, page)
    qh = q.reshape(bsz, t_new, n_heads, A_HEAD_DIM)
    eye = jnp.eye(n_heads, dtype=F32)
    qbd = (qh.transpose(0, 2, 1, 3)[:, :, :, None, :] * eye[None, :, None, :, None])
    qbd = qbd.reshape(bsz, n_rows, d).astype(BF16)
    bias_rows = jnp.broadcast_to(jnp.repeat(logit_bias * LOG2E, t_new)[:, None], (n_rows, page))
    n_steps = n_pages // pages_per_step

    def page_map(r):
        def index(b, s, pt):
            return (layer, pt[b, n_pages - 1 - (s * pages_per_step + r)], 0, 0)
        return index

    kv_specs = [pl.BlockSpec((1, 1, d, page), page_map(r)) for r in range(pages_per_step)]
    grid_spec = pltpu.PrefetchScalarGridSpec(
        num_scalar_prefetch=1,
        grid=(bsz, n_steps),
        in_specs=[pl.BlockSpec((1, n_rows, d), lambda b, s, pt: (b, 0, 0)),
                  pl.BlockSpec((n_rows, page), lambda b, s, pt: (0, 0)),
                  pl.BlockSpec((1, t_new, d), lambda b, s, pt: (b, 0, 0)),
                  pl.BlockSpec((1, t_new, d), lambda b, s, pt: (b, 0, 0))] + kv_specs + kv_specs,
        out_specs=pl.BlockSpec((1, t_new, d), lambda b, s, pt: (b, 0, 0)),
        scratch_shapes=[pltpu.VMEM((n_rows, d), F32), pltpu.VMEM((n_rows, 1), F32)],
    )
    return pl.pallas_call(
        functools.partial(_sb_decode_kernel, pages_per_step=pages_per_step, page=page, t_new=t_new),
        grid_spec=grid_spec,
        out_shape=jax.ShapeDtypeStruct(q.shape, F32),
        compiler_params=_params("parallel", "arbitrary"),
        name="sb_decode",
    )(page_table, qbd, bias_rows, k_new, v_new, *([ck] * pages_per_step), *([cv] * pages_per_step))


def _proj_res_kernel(a_ref, w_ref, x_ref, gt_ref, o_ref):
    o_ref[0] = x_ref[0] + gt_ref[0] * _bdot(a_ref[0], w_ref[...])


def _proj_res(a, w, x, gt, tm):
    bsz, t, d = x.shape
    return pl.pallas_call(
        _proj_res_kernel,
        grid=(bsz, t // tm),
        in_specs=[_tok_spec(tm, d), _const_spec(w), _tok_spec(tm, d), _mod_spec(gt, tm)],
        out_specs=_tok_spec(tm, d),
        out_shape=jax.ShapeDtypeStruct(x.shape, F32),
        compiler_params=_params("parallel", "parallel"),
        name="proj_res",
    )(a, w, x, gt)


def _gdn_out_kernel(o_ref_in, z_ref, on_ref, w_ref, x_ref, gt_ref, o_ref):
    o = o_ref_in[0]
    gmat = _group_matrix(MXU_DIM, B_HEAD_DIM, 1.0 / B_HEAD_DIM)
    ms = _group_reduce(o * o, gmat)
    a = o * lax.rsqrt(ms + NORM_EPS) * on_ref[...] * _silu(z_ref[0])
    o_ref[0] = x_ref[0] + gt_ref[0] * _bdot(a, w_ref[...])


def _gdn_out(o, z, o_norm, w, x, gt, tm):
    bsz, t, d = x.shape
    on = jnp.tile(o_norm, d // B_HEAD_DIM).reshape(1, d)
    return pl.pallas_call(
        _gdn_out_kernel,
        grid=(bsz, t // tm),
        in_specs=[_tok_spec(tm, d), _tok_spec(tm, d), _const_spec(on), _const_spec(w),
                  _tok_spec(tm, d), _mod_spec(gt, tm)],
        out_specs=_tok_spec(tm, d),
        out_shape=jax.ShapeDtypeStruct(x.shape, F32),
        compiler_params=_params("parallel", "parallel"),
        name="gdn_out",
    )(o, z, on, w, x, gt)


def _rwkv_out_kernel(y_ref, r_ref, k_ref, v_ref, gate_ref, rk_ref, lg_ref, lb_ref, w_ref,
                     x_ref, gt_ref, o_ref):
    y = y_ref[0]
    mean_mat = _group_matrix(MXU_DIM, C_HEAD_DIM, 1.0 / C_HEAD_DIM)
    sum_mat = _group_matrix(MXU_DIM, C_HEAD_DIM, 1.0)
    yc = y - _group_reduce(y, mean_mat)
    var = _group_reduce(yc * yc, mean_mat)
    yn = yc * lax.rsqrt(var + GN_EPS) * lg_ref[...] + lb_ref[...]
    bonus = _group_reduce(r_ref[0] * k_ref[0] * rk_ref[...], sum_mat) * v_ref[0]
    a = (yn + bonus) * gate_ref[0]
    o_ref[0] = x_ref[0] + gt_ref[0] * _bdot(a, w_ref[...])


def _rwkv_out(y, r, k, v, gate, r_k, ln_g, ln_b, w, x, gt, tm):
    bsz, t, d = x.shape
    rk = r_k.reshape(1, d)
    lg = ln_g.reshape(1, d)
    lb = ln_b.reshape(1, d)
    tok = _tok_spec(tm, d)
    return pl.pallas_call(
        _rwkv_out_kernel,
        grid=(bsz, t // tm),
        in_specs=[tok, tok, tok, tok, tok, _const_spec(rk), _const_spec(lg), _const_spec(lb),
                  _const_spec(w), tok, _mod_spec(gt, tm)],
        out_specs=tok,
        out_shape=jax.ShapeDtypeStruct(x.shape, F32),
        compiler_params=_params("parallel", "parallel"),
        name="rwkv_out",
    )(y, r, k, v, gate, rk, lg, lb, w, x, gt)


def _gdn_in_kernel(x_ref, g_ref, sc_ref, sh_ref, w_ref, xqkv_ref, z_ref, ba_ref):
    n_conv = xqkv_ref.shape[2]
    n_z = z_ref.shape[2]
    h = _normmod(x_ref[0], g_ref[...], sc_ref[0], sh_ref[0]).astype(BF16)
    for s in range(0, n_conv, MXU_DIM):
        xqkv_ref[0, :, s:s + MXU_DIM] = jnp.dot(h, w_ref[:, s:s + MXU_DIM],
                                                preferred_element_type=F32)
    for s in range(0, n_z, MXU_DIM):
        z_ref[0, :, s:s + MXU_DIM] = jnp.dot(h, w_ref[:, n_conv + s:n_conv + s + MXU_DIM],
                                             preferred_element_type=F32)
    ba_ref[0] = jnp.dot(h, w_ref[:, n_conv + n_z:], preferred_element_type=F32)


def _gdn_in(x, g, sc, sh, w_pad, n_conv, tm):
    bsz, t, d = x.shape
    return pl.pallas_call(
        _gdn_in_kernel,
        grid=(bsz, t // tm),
        in_specs=[_tok_spec(tm, d), _const_spec(g), _mod_spec(sc, tm), _mod_spec(sh, tm),
                  _const_spec(w_pad)],
        out_specs=[_tok_spec(tm, n_conv), _tok_spec(tm, d), _tok_spec(tm, LANES)],
        out_shape=[jax.ShapeDtypeStruct((bsz, t, n_conv), F32),
                   jax.ShapeDtypeStruct((bsz, t, d), F32),
                   jax.ShapeDtypeStruct((bsz, t, LANES), F32)],
        compiler_params=_params("parallel", "parallel"),
        name="gdn_in",
    )(x, g, sc, sh, w_pad)


def _gdn_conv_kernel(*refs, pre_shifted):
    if pre_shifted:
        x_ref, xs_ref, cw_ref, ba_ref, alog_ref, dtb_ref = refs[:6]
    else:
        x_ref, p8_ref, cw_ref, ba_ref, alog_ref, dtb_ref = refs[:6]
    q_ref, k_ref, kb_ref, vb_ref, g_ref = refs[6:]
    i = pl.program_id(1)
    tm = x_ref.shape[1]
    hd = q_ref.shape[2]
    n_heads = hd // B_HEAD_DIM
    x = x_ref[0]
    conv = x * cw_ref[B_CONV - 1:B_CONV, :]
    if pre_shifted:
        for s in range(1, B_CONV):
            conv = conv + xs_ref[s - 1, 0] * cw_ref[B_CONV - 1 - s:B_CONV - s, :]
    else:
        full = jnp.concatenate([p8_ref[0], x], axis=0)
        t_glob = i * tm + lax.broadcasted_iota(jnp.int32, (tm, 1), 0)
        for s in range(1, B_CONV):
            xs = pltpu.roll(full, s, 0)[8:]
            xs = jnp.where(t_glob >= s, xs, 0.0)
            conv = conv + xs * cw_ref[B_CONV - 1 - s:B_CONV - s, :]
    act = _silu(conv)
    r = lax.broadcasted_iota(jnp.int32, (LANES, hd), 0)
    c_head = lax.broadcasted_iota(jnp.int32, (LANES, hd), 1) // B_HEAD_DIM
    e_b = (r == c_head).astype(BF16)
    e_a = (r == c_head + n_heads).astype(BF16)
    ba = ba_ref[0]
    beta = _sigmoid(_dot_xe(ba, e_b))
    g = -jnp.exp(alog_ref[...]) * _softplus(_dot_xe(ba, e_a) + dtb_ref[...])
    g_ref[0] = g
    ones = _group_matrix(MXU_DIM, B_HEAD_DIM, 1.0)
    aq = act[:, :hd]
    ak = act[:, hd:2 * hd]
    q_ref[0] = aq * lax.rsqrt(_group_reduce(aq * aq, ones) + L2_EPS) * (B_HEAD_DIM ** -0.5)
    k = ak * lax.rsqrt(_group_reduce(ak * ak, ones) + L2_EPS)
    k_ref[0] = k
    kb_ref[0] = k * beta
    vb_ref[0] = act[:, 2 * hd:] * beta


def _gdn_conv(xqkv, xs, conv_w, ba, a_log, dt_bias, hd, tm):
    bsz, t, n_conv = xqkv.shape
    alog = jnp.repeat(a_log, B_HEAD_DIM).reshape(1, hd)
    dtb = jnp.repeat(dt_bias, B_HEAD_DIM).reshape(1, hd)
    pre_shifted = xs is not None
    if pre_shifted:
        second = xs
        second_spec = pl.BlockSpec((B_CONV - 1, 1, tm, n_conv), lambda b, i: (0, b, i, 0))
    else:
        second = xqkv
        second_spec = _prev8_spec(tm, n_conv)
    shp = jax.ShapeDtypeStruct((bsz, t, hd), F32)
    return pl.pallas_call(
        functools.partial(_gdn_conv_kernel, pre_shifted=pre_shifted),
        grid=(bsz, t // tm),
        in_specs=[_tok_spec(tm, n_conv), second_spec, _const_spec(conv_w), _tok_spec(tm, LANES),
                  _const_spec(alog), _const_spec(dtb)],
        out_specs=[_tok_spec(tm, hd)] * 5,
        out_shape=[shp] * 5,
        compiler_params=_params("parallel", "parallel"),
        name="gdn_conv",
    )(xqkv, second, conv_w, ba, alog, dtb)


def _gdn_chunk_kernel(q_ref, k_ref, kb_ref, vb_ref, g_ref, s0_ref, o_ref, s_out_ref, s_ref, *,
                      n_heads, n_chunks):
    i = pl.program_id(2)
    c = CHUNK
    hw = B_HEAD_DIM

    @pl.when(i == 0)
    def _():
        s_ref[...] = s0_ref[0]

    row = lax.broadcasted_iota(jnp.int32, (c, c), 0)
    col = lax.broadcasted_iota(jnp.int32, (c, c), 1)
    lower = row >= col
    strict = row > col
    lower_b = lower.astype(BF16)
    ones_avg = jnp.full((c, LANES), 1.0 / LANES, BF16)

    def row_layout(g_cb):
        gh, gl = _split(g_cb)
        nt = lambda x: lax.dot_general(ones_avg, x, (((1,), (1,)), ((), ())),
                                       preferred_element_type=F32)
        return nt(gh) + nt(gl)

    sls = [(slice(n * c, (n + 1) * c), slice(h * hw, (h + 1) * hw))
           for h in range(n_heads) for n in range(n_chunks)]
    rng = range(len(sls))
    q = [q_ref[0, rs, ls] for rs, ls in sls]
    k = [k_ref[0, rs, ls] for rs, ls in sls]
    kb = [kb_ref[0, rs, ls] for rs, ls in sls]
    vb = [vb_ref[0, rs, ls] for rs, ls in sls]
    g_cb = [_dot_ex(lower_b, g_ref[0, rs, ls]) for rs, ls in sls]
    g_row = [row_layout(x) for x in g_cb]
    decay = [jnp.where(lower, jnp.exp(jnp.where(lower, g_cb[n][:, :c] - g_row[n], 0.0)), 0.0)
             for n in rng]
    a = [jnp.where(strict, decay[n] * _bdot_nt(kb[n], k[n]), 0.0) for n in rng]
    t_inv = _tri_inverses(a, 5)
    e_g = [jnp.exp(x) for x in g_cb]
    w_mat = [_dot3(t_inv[n], kb[n] * e_g[n]) for n in rng]
    u0 = [_dot3(t_inv[n], vb[n]) for n in rng]
    qk = [decay[n] * _bdot_nt(q[n], k[n]) for n in rng]
    g_last = [x[c - 1:c, :] for x in g_cb]
    k_end = [jnp.exp(g_last[n] - g_cb[n]) * k[n] for n in rng]
    q_eff = [q[n] * e_g[n] - _bdot(qk[n], w_mat[n]) for n in rng]
    o0 = [_bdot(qk[n], u0[n]) for n in rng]
    m0 = [_bdot_tn(k_end[n], w_mat[n]) for n in rng]
    n0 = [_bdot_tn(k_end[n], u0[n]) for n in rng]

    for h in range(n_heads):
        s = s_ref[h]
        for n in range(h * n_chunks, (h + 1) * n_chunks):
            rs, ls = sls[n]
            o_ref[0, rs, ls] = _bdot(q_eff[n], s) + o0[n]
            s = jnp.exp(g_last[n]) * s - _bdot(m0[n], s) + n0[n]
        s_ref[h] = s
        s_out_ref[0, h] = s


def _gdn_chunk(q, k, kb, vb, g, s0, ct, heads_per_step):
    bsz, t, hd = q.shape
    hw = B_HEAD_DIM
    blk = pl.BlockSpec((1, ct, heads_per_step * hw), lambda b, h, i: (b, i, h))
    st = pl.BlockSpec((1, heads_per_step, hw, hw), lambda b, h, i: (b, h, 0, 0))
    return pl.pallas_call(
        functools.partial(_gdn_chunk_kernel, n_heads=heads_per_step, n_chunks=ct // CHUNK),
        grid=(bsz, hd // (heads_per_step * hw), t // ct),
        in_specs=[blk] * 5 + [st],
        out_specs=[blk, st],
        out_shape=[jax.ShapeDtypeStruct(q.shape, F32), jax.ShapeDtypeStruct(s0.shape, F32)],
        scratch_shapes=[pltpu.VMEM((heads_per_step, hw, hw), F32)],
        compiler_params=_params("parallel", "parallel", "arbitrary"),
        name="gdn_chunk",
    )(q, k, kb, vb, g, s0)


def _rwkv_in_kernel(*refs, seq_len, per_row_shift):
    (x_ref, p8_ref, g_ref, sc_ref, sh_ref, sh0_ref, mu_ref, wrkv_ref, w1_ref, w2_ref, a1_ref, a2_ref,
     g1_ref, g2_ref, w0_ref, a0_ref, kk_ref, ka_ref) = refs[:18]
    r_ref, lw_ref, k_ref, v_ref, kn_ref, bb_ref, gate_ref, hl_ref = refs[18:]
    i = pl.program_id(1)
    tm = x_ref.shape[1]
    g = g_ref[...]
    h = _normmod(x_ref[0], g, sc_ref[0], sh_ref[0])
    if per_row_shift:
        sc8, sh8 = sc_ref[0, 0:8], sh_ref[0, 0:8]
    else:
        sc8, sh8 = sc_ref[0], sh_ref[0]
    h8 = _normmod(p8_ref[0], g, sc8, sh8)
    prev = pltpu.roll(jnp.concatenate([h8, h], axis=0), 1, 0)[8:]
    t_glob = i * tm + lax.broadcasted_iota(jnp.int32, (tm, 1), 0)
    prev = jnp.where(t_glob % seq_len == 0, sh0_ref[0], prev)
    xx = prev - h
    mix = lambda n: (h + xx * mu_ref[n:n + 1, :]).astype(BF16)
    r = jnp.dot(mix(0), wrkv_ref[0], preferred_element_type=F32)
    k = jnp.dot(mix(1), wrkv_ref[1], preferred_element_type=F32)
    v_ref[0] = jnp.dot(mix(2), wrkv_ref[2], preferred_element_type=F32)
    r_ref[0] = r
    wl = w0_ref[...] + _bdot(jnp.tanh(jnp.dot(mix(3), w1_ref[...], preferred_element_type=F32)),
                             w2_ref[...])
    lw_ref[0] = -jnp.exp(-_softplus(-wl) - 0.5)
    a = _sigmoid(a0_ref[...] + _bdot(jnp.dot(mix(4), a1_ref[...], preferred_element_type=F32),
                                     a2_ref[...]))
    gate_ref[0] = _bdot(_sigmoid(jnp.dot(mix(5), g1_ref[...], preferred_element_type=F32)),
                        g2_ref[...])
    kx = k * kk_ref[...]
    ones = _group_matrix(MXU_DIM, C_HEAD_DIM, 1.0)
    kn = kx * lax.rsqrt(_group_reduce(kx * kx, ones) + L2_EPS)
    kn_ref[0] = kn
    bb_ref[0] = kn * a
    k_ref[0] = k * (1.0 + (a - 1.0) * ka_ref[...])
    hl_ref[0, 0] = h[tm - 8:tm]


def _rwkv_in(x, g, sc, sh, shift0, seq_len, P, tm):
    bsz, t, d = x.shape
    per_row = shift0.shape[1] != 1
    consts = [P['mu'], P['w_rkv'], P['w1'], P['w2'], P['a1'], P['a2'], P['g1'], P['g2'],
              P['w0'], P['a0'], P['k_k'], P['k_a']]
    shp = jax.ShapeDtypeStruct(x.shape, F32)
    n_t = t // tm
    outs = pl.pallas_call(
        functools.partial(_rwkv_in_kernel, seq_len=seq_len, per_row_shift=per_row),
        grid=(bsz, n_t),
        in_specs=[_tok_spec(tm, d), _prev8_spec(tm, d), _const_spec(g), _mod_spec(sc, tm),
                  _mod_spec(sh, tm), _mod_spec(shift0, tm)] + [_const_spec(c) for c in consts],
        out_specs=[_tok_spec(tm, d)] * 7 + [pl.BlockSpec((1, 1, 8, d), lambda b, i: (b, i, 0, 0))],
        out_shape=[shp] * 7 + [jax.ShapeDtypeStruct((bsz, n_t, 8, d), F32)],
        compiler_params=_params("parallel", "parallel"),
        name="rwkv_in",
    )(x, x, g, sc, sh, shift0, *consts)
    return outs


def _rwkv_scan_kernel(r_ref, lw_ref, k_ref, v_ref, kn_ref, bb_ref, s0_ref, y_ref, s_out_ref, s_ref,
                      *, n_pairs, n_chunks):
    i = pl.program_id(2)
    c = CHUNK
    n = 2 * c

    @pl.when(i == 0)
    def _():
        s_ref[...] = s0_ref[0]

    lane = lax.broadcasted_iota(jnp.int32, (1, LANES), 1)
    row = lax.broadcasted_iota(jnp.int32, (n, n), 0)
    col = lax.broadcasted_iota(jnp.int32, (n, n), 1)
    same = (row // c) == (col // c)
    incl = same & (row >= col)
    strict = same & (row > col)
    incl_b = incl.astype(BF16)

    def stack(x):
        return jnp.concatenate([jnp.where(lane < C_HEAD_DIM, x, 0.0),
                                jnp.where(lane >= C_HEAD_DIM, x, 0.0)], axis=0)

    sls = [(slice(m * c, (m + 1) * c), slice(p * LANES, (p + 1) * LANES))
           for p in range(n_pairs) for m in range(n_chunks)]
    rng = range(len(sls))
    lw = [stack(lw_ref[0, rs, ls]) for rs, ls in sls]
    lc = [_dot_ex(incl_b, x) for x in lw]
    lc_last = [x[c - 1:c, :] + x[n - 1:n, :] for x in lc]
    p_inv = [jnp.exp(-x) for x in lc]
    p_end = [jnp.exp(lc_last[m] - lc[m]) for m in rng]
    kn_t = [stack(kn_ref[0, sls[m][0], sls[m][1]]) * jnp.exp(lc[m] - lw[m]) for m in rng]
    k_s = [stack(k_ref[0, rs, ls]) for rs, ls in sls]
    b_s = [stack(bb_ref[0, rs, ls]) for rs, ls in sls]
    v_s = [stack(v_ref[0, rs, ls]) for rs, ls in sls]
    k_t = [k_s[m] * p_inv[m] for m in rng]
    b_t = [b_s[m] * p_inv[m] for m in rng]
    r_t = [stack(r_ref[0, sls[m][0], sls[m][1]]) * jnp.exp(lc[m]) for m in rng]
    a_kb = [jnp.where(strict, _bdot_nt(kn_t[m], b_t[m]), 0.0) for m in rng]
    a_kk = [jnp.where(strict, _bdot_nt(kn_t[m], k_t[m]), 0.0) for m in rng]
    a_rk = [jnp.where(incl, _bdot_nt(r_t[m], k_t[m]), 0.0) for m in rng]
    a_rb = [jnp.where(incl, _bdot_nt(r_t[m], b_t[m]), 0.0) for m in rng]
    t_inv = _tri_inverses(a_kb, 5)
    kt = [_dot3(t_inv[m], kn_t[m]) for m in rng]
    u0 = [_dot3(t_inv[m], _bdot(a_kk[m], v_s[m])) for m in rng]
    k_e = [k_s[m] * p_end[m] for m in rng]
    b_e = [b_s[m] * p_end[m] for m in rng]
    g_mat = [_bdot_tn(kt[m], b_e[m]) for m in rng]
    n0 = [_bdot_tn(v_s[m], k_e[m]) - _bdot_tn(u0[m], b_e[m]) for m in rng]
    r_eff = [r_t[m] - _bdot(a_rb[m], kt[m]) for m in rng]
    y0 = [_bdot(a_rk[m], v_s[m]) - _bdot(a_rb[m], u0[m]) for m in rng]

    for p in range(n_pairs):
        s = s_ref[p]
        for m in range(p * n_chunks, (p + 1) * n_chunks):
            rs, ls = sls[m]
            y = _bdot_nt(r_eff[m], s) + y0[m]
            y_ref[0, rs, ls] = y[:c] + y[c:]
            s = s * jnp.exp(lc_last[m]) - _bdot(s, g_mat[m]) + n0[m]
        s_ref[p] = s
        s_out_ref[0, p] = s


def _rwkv_scan(r, lw, k, v, kn, bb, s0_pairs, ct, pairs_per_step):
    bsz, t, d = r.shape
    blk = pl.BlockSpec((1, ct, pairs_per_step * LANES), lambda b, p, i: (b, i, p))
    st = pl.BlockSpec((1, pairs_per_step, LANES, LANES), lambda b, p, i: (b, p, 0, 0))
    return pl.pallas_call(
        functools.partial(_rwkv_scan_kernel, n_pairs=pairs_per_step, n_chunks=ct // CHUNK),
        grid=(bsz, d // (pairs_per_step * LANES), t // ct),
        in_specs=[blk] * 6 + [st],
        out_specs=[blk, st],
        out_shape=[jax.ShapeDtypeStruct(r.shape, F32), jax.ShapeDtypeStruct(s0_pairs.shape, F32)],
        scratch_shapes=[pltpu.VMEM((pairs_per_step, LANES, LANES), F32)],
        compiler_params=_params("parallel", "parallel", "arbitrary"),
        name="rwkv_scan",
    )(r, lw, k, v, kn, bb, s0_pairs)


def _pad_time(a, t_pad):
    return jnp.pad(a, ((0, 0), (0, t_pad - a.shape[1]), (0, 0)))


def _pairs_from_heads(s):
    bsz, n_h, n, _ = s.shape
    sp = s.reshape(bsz, n_h // 2, 2, n, n)
    z = jnp.zeros_like(sp[:, :, 0])
    top = jnp.concatenate([sp[:, :, 0], z], axis=-1)
    bot = jnp.concatenate([z, sp[:, :, 1]], axis=-1)
    return jnp.concatenate([top, bot], axis=-2)


def _heads_from_pairs(sp):
    bsz, n_p, n2, _ = sp.shape
    n = n2 // 2
    return jnp.stack([sp[:, :, :n, :n], sp[:, :, n:, n:]], axis=2).reshape(bsz, 2 * n_p, n, n)


def _run_trunk(x, mods, W, tm, seq_len, cache, state):
    bsz, t, d = x.shape
    n_seq = bsz * t // seq_len
    depth = len(mods)
    ks, vs, bconv, bssm, cshift, cwkv = [], [], [], [], [], []
    for l in range(depth):
        kind, j = l % 3, l // 3
        sh1, sc1, gt1, sh2, sc2, gt2 = mods[l]
        g_mix = W['ln_mix'][l].reshape(1, d)
        if kind == 0:
            q, k, v = _qkv(x, g_mix, sc1, sh1, W['a_w_qkv'][j], W['a_q_norm'][j], W['a_k_norm'][j], tm)
            if cache is None:
                o = _sb_prompt(q, k, v, W['a_logit_bias'][j], qb=256)
            else:
                cache_k, cache_v, page_table = cache
                o = _sb_decode(q.reshape(n_seq, seq_len, d), k.reshape(n_seq, seq_len, d),
                               v.reshape(n_seq, seq_len, d), cache_k, cache_v, j, page_table,
                               W['a_logit_bias'][j],
                               pages_per_step=min(8, page_table.shape[1])).reshape(bsz, t, d)
            ks.append(k.reshape(n_seq, seq_len, d // A_HEAD_DIM, A_HEAD_DIM))
            vs.append(v.reshape(n_seq, seq_len, d // A_HEAD_DIM, A_HEAD_DIM))
            x = _proj_res(o, W['a_w_o'][j], x, gt1, tm)
        elif kind == 1:
            n_conv = 3 * d
            xqkv, z, ba = _gdn_in(x, g_mix, sc1, sh1, W['b_w_in'][j], n_conv, tm)
            xq_seq = xqkv.reshape(n_seq, seq_len, n_conv)
            if state is None:
                xs = None
                bconv.append(xq_seq[:, seq_len - (B_CONV - 1):])
                s0 = jnp.zeros((n_seq, d // B_HEAD_DIM, B_HEAD_DIM, B_HEAD_DIM), F32)
            else:
                xc = jnp.concatenate([state['b_conv'][j], xq_seq], axis=1)
                xs = jnp.stack([xc[:, B_CONV - 1 - s:B_CONV - 1 - s + seq_len].reshape(bsz, t, n_conv)
                                for s in range(1, B_CONV)])
                bconv.append(xc[:, seq_len:])
                s0 = state['b_ssm'][j]
            q, k, kb, vb, g = _gdn_conv(xqkv, xs, W['b_conv'][j], ba, W['b_a_log'][j],
                                        W['b_dt_bias'][j], d, min(tm, 256))
            t_pad = -(-seq_len // CHUNK) * CHUNK
            seqs = [_pad_time(a.reshape(n_seq, seq_len, d), t_pad) for a in (q, k, kb, vb, g)]
            ct = min(t_pad, CHAINS_PER_STEP * CHUNK)
            heads = max(1, min(d // B_HEAD_DIM, CHAINS_PER_STEP * CHUNK // ct))
            o, s_new = _gdn_chunk(*seqs, s0, ct=ct, heads_per_step=heads)
            bssm.append(s_new)
            o = o[:, :seq_len].reshape(bsz, t, d)
            x = _gdn_out(o, z, W['b_o_norm'][j], W['b_w_o'][j], x, gt1, tm)
        else:
            if state is None:
                shift0 = jnp.zeros((bsz, 1, d), F32)
                s0 = jnp.zeros((n_seq, d // C_HEAD_DIM, C_HEAD_DIM, C_HEAD_DIM), F32)
            else:
                shift0 = jnp.repeat(state['c_shift'][j], seq_len, axis=0).reshape(bsz, t, d)
                s0 = state['c_wkv'][j]
            tm_c = min(tm, 256)
            r, lw, k, v, kn, bb, gate, h_last = _rwkv_in(x, g_mix, sc1, sh1, shift0, seq_len,
                                                        W['c'][j], tm_c)
            if state is None:
                cshift.append(h_last[:, -1, 7])
            else:
                assert t == tm_c and seq_len == 8
                cshift.append(_last_rows(x, g_mix, sc1, sh1, seq_len))
            t_pad = -(-seq_len // CHUNK) * CHUNK
            seqs = [_pad_time(a.reshape(n_seq, seq_len, d), t_pad) for a in (r, lw, k, v, kn, bb)]
            ct = min(t_pad, CHAINS_PER_STEP * CHUNK)
            pairs = max(1, min(d // LANES, CHAINS_PER_STEP * CHUNK // ct))
            y, s_new = _rwkv_scan(*seqs, _pairs_from_heads(s0), ct=ct, pairs_per_step=pairs)
            cwkv.append(_heads_from_pairs(s_new))
            y = y[:, :seq_len].reshape(bsz, t, d)
            x = _rwkv_out(y, r, k, v, gate, W['c_r_k'][j], W['c_ln_g'][j], W['c_ln_b'][j],
                          W['c_w_o'][j], x, gt1, tm)
        x = _mlp(x, W['ln_mlp'][l].reshape(1, d), sc2, sh2, gt2, W['w_up'][l], W['w_down'][l], tm)
    return x, ks, vs, bconv, bssm, cshift, cwkv


def _hmod_kernel(x_ref, g_ref, sc_ref, sh_ref, o_ref):
    o_ref[0] = _normmod(x_ref[0], g_ref[...], sc_ref[0], sh_ref[0])


def _last_rows(x, g, sc, sh, seq_len):
    bsz, t, d = x.shape
    n_seq = bsz * t // seq_len
    pick = lambda a: a.reshape(n_seq, seq_len, d)[:, seq_len - 1].reshape(1, n_seq, d)
    xs, scs, shs = pick(x), pick(sc), pick(sh)
    return pl.pallas_call(
        _hmod_kernel,
        grid=(1, 1),
        in_specs=[_tok_spec(n_seq, d), _const_spec(g), _tok_spec(n_seq, d), _tok_spec(n_seq, d)],
        out_specs=_tok_spec(n_seq, d),
        out_shape=jax.ShapeDtypeStruct((1, n_seq, d), F32),
        compiler_params=_params("parallel", "parallel"),
        name="last_rows",
    )(xs, g, scs, shs)[0]


def kernel(x_prompt, x_sample, c_prompt, c_sample, cache_k, cache_v, page_table, state_b_conv, state_b_ssm, state_c_shift, state_c_wkv, ln_mix, ln_mlp, w_ada, b_ada, w_up, w_down, a_w_qkv, a_q_norm, a_k_norm, a_logit_bias, a_w_o, b_w_in, b_conv, b_a_log, b_dt_bias, b_o_norm, b_w_o, c_mu, c_w_rkv, c_w0, c_w1, c_w2, c_a0, c_a1, c_a2, c_g1, c_g2, c_k_k, c_k_a, c_r_k, c_ln_g, c_ln_b, c_w_o):
    bp, t_p, d = x_prompt.shape
    bs, t_s, _ = x_sample.shape
    depth = ln_mix.shape[0]
    n_c = c_mu.shape[0]

    def lora_cols(w):
        return jnp.pad(w, ((0, 0), (0, 0), (0, LANES - w.shape[2]))).astype(BF16)

    def lora_rows(w):
        return jnp.pad(w, ((0, 0), (0, LANES - w.shape[1]), (0, 0))).astype(BF16)

    b_in = b_w_in.shape[2]
    b_in_pad = -(-b_in // LANES) * LANES
    W = {
        'ln_mix': ln_mix, 'ln_mlp': ln_mlp,
        'w_up': w_up.astype(BF16), 'w_down': w_down.astype(BF16),
        'a_w_qkv': a_w_qkv.astype(BF16), 'a_q_norm': a_q_norm, 'a_k_norm': a_k_norm,
        'a_logit_bias': a_logit_bias, 'a_w_o': a_w_o.astype(BF16),
        'b_w_in': jnp.pad(b_w_in, ((0, 0), (0, 0), (0, b_in_pad - b_in))).astype(BF16),
        'b_conv': b_conv, 'b_a_log': b_a_log, 'b_dt_bias': b_dt_bias, 'b_o_norm': b_o_norm,
        'b_w_o': b_w_o.astype(BF16),
        'c_r_k': c_r_k.reshape(n_c, d), 'c_ln_g': c_ln_g, 'c_ln_b': c_ln_b, 'c_w_o': c_w_o.astype(BF16),
    }
    w1, w2 = lora_cols(c_w1), lora_rows(c_w2)
    a1, a2 = lora_cols(c_a1), lora_rows(c_a2)
    g1, g2 = c_g1.astype(BF16), c_g2.astype(BF16)
    rkv = c_w_rkv.astype(BF16)
    W['c'] = [{'mu': c_mu[j], 'w_rkv': rkv[j], 'w1': w1[j], 'w2': w2[j], 'a1': a1[j], 'a2': a2[j],
               'g1': g1[j], 'g2': g2[j], 'w0': c_w0[j].reshape(1, d), 'a0': c_a0[j].reshape(1, d),
               'k_k': c_k_k[j].reshape(1, d), 'k_a': c_k_a[j].reshape(1, d)} for j in range(n_c)]

    mod_all = _ada(jnp.concatenate([c_prompt, c_sample], axis=0), w_ada, b_ada)
    mods_p, mods_s = [], []
    for l in range(depth):
        mp = mod_all[l, :bp].reshape(bp, 1, 6, d)
        mods_p.append([mp[:, :, n] for n in range(6)])
        ms = jnp.repeat(mod_all[l, bp:], t_s, axis=0).reshape(1, bs * t_s, 6, d)
        mods_s.append([ms[:, :, n] for n in range(6)])

    y_p, k_p, v_p, bconv_p, bssm_p, cshift_p, cwkv_p = _run_trunk(
        x_prompt, mods_p, W, 512, t_p, None, None)
    state = {'b_conv': state_b_conv, 'b_ssm': state_b_ssm, 'c_shift': state_c_shift,
             'c_wkv': state_c_wkv}
    y_s, k_s, v_s, bconv_s, bssm_s, cshift_s, cwkv_s = _run_trunk(
        x_sample.reshape(1, bs * t_s, d), mods_s, W, bs * t_s, t_s,
        (cache_k, cache_v, page_table), state)
    st = jnp.stack
    return (y_p, y_s.reshape(bs, t_s, d), st(k_p), st(v_p), st(k_s), st(v_s), st(bconv_p), st(bssm_p),
            st(bconv_s), st(bssm_s), st(cshift_p), st(cwkv_p), st(cshift_s), st(cwkv_s))
```

```python
import functools

import jax
import jax.numpy as jnp
from jax import lax
from jax.experimental import pallas as pl
from jax.experimental.pallas import tpu as pltpu

F32 = jnp.float32
BF16 = jnp.bfloat16

LANES = 128
MXU_DIM = 256
VMEM_LIMIT_BYTES = 56 * 1024 * 1024

NORM_EPS = 1e-6
L2_EPS = 1e-6
GN_EPS = 64e-5
LOG2E = 1.4426950408889634
A_HEAD_DIM = 64
B_HEAD_DIM = 128
C_HEAD_DIM = 64
B_CONV = 4
CHUNK = 64
CHAINS_PER_STEP = 8
SB_QUERY_BLOCK = 512
SB_KEY_BLOCK = MXU_DIM
MLP_TOKEN_TILE = 1024


def _params(*sem):
    return pltpu.CompilerParams(dimension_semantics=sem, vmem_limit_bytes=VMEM_LIMIT_BYTES)


def _bdot(a, b):
    return jnp.dot(a.astype(BF16), b.astype(BF16), preferred_element_type=F32)


def _bdot_nt(a, b):
    return lax.dot_general(a.astype(BF16), b.astype(BF16), (((1,), (1,)), ((), ())),
                           preferred_element_type=F32)


def _bdot_tn(a, b):
    return lax.dot_general(a.astype(BF16), b.astype(BF16), (((0,), (0,)), ((), ())),
                           preferred_element_type=F32)


def _split(x):
    hi = x.astype(BF16)
    lo = (x - hi.astype(F32)).astype(BF16)
    return hi, lo


def _sum_dots(lhs, rhs):
    if lhs[0].shape[1] % LANES == 0:
        return jnp.dot(jnp.concatenate(lhs, axis=1), jnp.concatenate(rhs, axis=0),
                       preferred_element_type=F32)
    out = jnp.dot(lhs[0], rhs[0], preferred_element_type=F32)
    for a, b in zip(lhs[1:], rhs[1:]):
        out = out + jnp.dot(a, b, preferred_element_type=F32)
    return out


def _dot_xe(x, e):
    hi, lo = _split(x)
    return _sum_dots([hi, lo], [e, e])


def _dot_ex(e, x):
    hi, lo = _split(x)
    return _sum_dots([e, e], [hi, lo])


def _dot3_parts(a_parts, b_parts):
    ah, al = a_parts
    bh, bl = b_parts
    return _sum_dots([ah, ah, al], [bh, bl, bh])


def _dot3(a, b):
    return _dot3_parts(_split(a), _split(b))


def _sigmoid(x):
    return 1.0 / (1.0 + jnp.exp(-x))


def _silu(x):
    return x * _sigmoid(x)


def _softplus(x):
    return jnp.maximum(x, 0.0) + jnp.log(1.0 + jnp.exp(-jnp.abs(x)))


def _neg_abs(x):
    bits = lax.bitcast_convert_type(x, jnp.uint32) | jnp.uint32(0x80000000)
    return lax.bitcast_convert_type(bits, F32)


def _sb_suffix(z, suffix2, mask):
    sp = jnp.maximum(z, 0.0) + jnp.log(1.0 + jnp.exp2(_neg_abs(z))) * LOG2E
    if mask is not None:
        sp = jnp.where(mask, sp, 0.0)
    hi, lo = _split(sp)
    return jnp.dot(jnp.concatenate([hi, lo], axis=1), suffix2, preferred_element_type=F32)


def _sb_local(z, s_incl, mask):
    w = jnp.exp2(z - s_incl)
    if mask is not None:
        w = jnp.where(mask, w, 0.0)
    return w.astype(BF16)


def _sb_weights(z, suffix2, mask):
    s_incl = _sb_suffix(z, suffix2, mask)
    return _sb_local(z, s_incl, mask), s_incl[:, 0:1]


def _normmod(x, g, sc, sh):
    ms = jnp.mean(x * x, axis=-1, keepdims=True)
    return x * lax.rsqrt(ms + NORM_EPS) * g * (1.0 + sc) + sh


def _group_matrix(n, group, value):
    r = lax.broadcasted_iota(jnp.int32, (n, n), 0) // group
    c = lax.broadcasted_iota(jnp.int32, (n, n), 1) // group
    return jnp.where(r == c, value, 0.0).astype(BF16)


def _group_reduce(x, gmat, split=False):
    n = gmat.shape[0]
    dot = _dot_xe if split else _bdot
    parts = [dot(x[:, s:s + n], gmat) for s in range(0, x.shape[1], n)]
    return parts[0] if len(parts) == 1 else jnp.concatenate(parts, axis=1)


def _tri_inverses(mats, n_steps):
    n = mats[0].shape[0]
    eye = (lax.broadcasted_iota(jnp.int32, (n, n), 0)
           == lax.broadcasted_iota(jnp.int32, (n, n), 1)).astype(F32)
    ps = [-a for a in mats]
    ts = [eye + p for p in ps]
    parts = [_split(p) for p in ps]
    for _ in range(n_steps):
        parts = [_split(_dot3_parts(s, s)) for s in parts]
        ts = [t + _dot3_parts(_split(t), s) for t, s in zip(ts, parts)]
    return ts


def _ada_kernel(c_ref, w_ref, b_ref, o_ref):
    ca = _silu(c_ref[...])
    o_ref[0] = _bdot(ca, w_ref[0]) + b_ref[0]


def _ada(c_all, w_ada, b_ada):
    depth, d, n = w_ada.shape
    rows = c_all.shape[0]
    tn = 1536
    return pl.pallas_call(
        _ada_kernel,
        grid=(depth, n // tn),
        in_specs=[pl.BlockSpec((rows, d), lambda l, j: (0, 0)),
                  pl.BlockSpec((1, d, tn), lambda l, j: (l, 0, j)),
                  pl.BlockSpec((1, 1, tn), lambda l, j: (l, 0, j))],
        out_specs=pl.BlockSpec((1, rows, tn), lambda l, j: (l, 0, j)),
        out_shape=jax.ShapeDtypeStruct((depth, rows, n), F32),
        compiler_params=_params("parallel", "parallel"),
        name="ada",
    )(c_all, w_ada, b_ada.reshape(depth, 1, n))


def _tok_spec(tm, width):
    return pl.BlockSpec((1, tm, width), lambda b, i: (b, i, 0))


def _mod_spec(mod, tm):
    if mod.shape[1] == 1:
        return pl.BlockSpec((1, 1, mod.shape[2]), lambda b, i: (b, 0, 0))
    return pl.BlockSpec((1, tm, mod.shape[2]), lambda b, i: (b, i, 0))


def _const_spec(arr):
    nd = arr.ndim
    return pl.BlockSpec(arr.shape, lambda b, i: (0,) * nd)


def _prev8_spec(tm, width):
    blocks = tm // 8
    return pl.BlockSpec((1, 8, width), lambda b, i: (b, jnp.maximum(i * blocks - 1, 0), 0))


def _mlp_kernel(x_ref, g_ref, sc_ref, sh_ref, gt_ref, wu_ref, wd_ref, o_ref, h_ref, acc_ref):
    f = pl.program_id(2)

    @pl.when(f == 0)
    def _():
        h_ref[...] = _normmod(x_ref[0], g_ref[...], sc_ref[0], sh_ref[0]).astype(BF16)
        acc_ref[...] = jnp.zeros_like(acc_ref)

    u = jnp.maximum(jnp.dot(h_ref[...], wu_ref[...], preferred_element_type=F32), 0.0)
    acc_ref[...] += jnp.dot((u * u).astype(BF16), wd_ref[...], preferred_element_type=F32)

    @pl.when(f == pl.num_programs(2) - 1)
    def _():
        o_ref[0] = x_ref[0] + gt_ref[0] * acc_ref[...]


def _mlp(x, g, sc, sh, gt, w_up, w_down, tm):
    bsz, t, d = x.shape
    ff = w_up.shape[1]
    tf = 1024

    def tok(b, i, f):
        return (b, i, 0)

    def mod_spec(m):
        if m.shape[1] == 1:
            return pl.BlockSpec((1, 1, d), lambda b, i, f: (b, 0, 0))
        return pl.BlockSpec((1, tm, d), tok)

    return pl.pallas_call(
        _mlp_kernel,
        grid=(bsz, t // tm, ff // tf),
        in_specs=[pl.BlockSpec((1, tm, d), tok),
                  pl.BlockSpec((1, d), lambda b, i, f: (0, 0)),
                  mod_spec(sc), mod_spec(sh), mod_spec(gt),
                  pl.BlockSpec((d, tf), lambda b, i, f: (0, f)),
                  pl.BlockSpec((tf, d), lambda b, i, f: (f, 0))],
        out_specs=pl.BlockSpec((1, tm, d), tok),
        out_shape=jax.ShapeDtypeStruct(x.shape, F32),
        scratch_shapes=[pltpu.VMEM((tm, d), BF16), pltpu.VMEM((tm, d), F32)],
        compiler_params=_params("parallel", "parallel", "arbitrary"),
        name="mlp",
    )(x, g, sc, sh, gt, w_up, w_down)


def _qkv_kernel(x_ref, g_ref, sc_ref, sh_ref, w_ref, qg_ref, kg_ref, q_ref, k_ref, v_ref):
    d = x_ref.shape[2]
    h = _normmod(x_ref[0], g_ref[...], sc_ref[0], sh_ref[0]).astype(BF16)
    gmat = _group_matrix(MXU_DIM, A_HEAD_DIM, 1.0 / A_HEAD_DIM)
    q_scale = A_HEAD_DIM ** -0.5 * LOG2E
    for s in range(0, d, MXU_DIM):
        sl = slice(s, s + MXU_DIM)
        q = jnp.dot(h, w_ref[:, s:s + MXU_DIM], preferred_element_type=F32)
        q_ref[0, :, sl] = q * lax.rsqrt(_bdot(q * q, gmat) + NORM_EPS) * qg_ref[...] * q_scale
        k = jnp.dot(h, w_ref[:, d + s:d + s + MXU_DIM], preferred_element_type=F32)
        k_ref[0, :, sl] = k * lax.rsqrt(_bdot(k * k, gmat) + NORM_EPS) * kg_ref[...]
        v_ref[0, :, sl] = jnp.dot(h, w_ref[:, 2 * d + s:2 * d + s + MXU_DIM],
                                  preferred_element_type=F32)


def _qkv(x, g, sc, sh, w, q_gain, k_gain, tm):
    bsz, t, d = x.shape
    qg = jnp.tile(q_gain, MXU_DIM // A_HEAD_DIM).reshape(1, MXU_DIM)
    kg = jnp.tile(k_gain, MXU_DIM // A_HEAD_DIM).reshape(1, MXU_DIM)
    shp = jax.ShapeDtypeStruct(x.shape, F32)
    return pl.pallas_call(
        _qkv_kernel,
        grid=(bsz, t // tm),
        in_specs=[_tok_spec(tm, d), _const_spec(g), _mod_spec(sc, tm), _mod_spec(sh, tm),
                  _const_spec(w), _const_spec(qg), _const_spec(kg)],
        out_specs=[_tok_spec(tm, d)] * 3,
        out_shape=[shp] * 3,
        compiler_params=_params("parallel", "parallel"),
        name="qkv",
    )(x, g, sc, sh, w, qg, kg)


def _sb_prompt_kernel(bias_ref, q_ref, k_ref, v_ref, o_ref, z_ref, s_ref, acc_ref, c_ref, *, qb, kb):
    hp = pl.program_id(1)
    i = pl.program_id(2)
    q = q_ref[0]
    lane = lax.broadcasted_iota(jnp.int32, (1, LANES), 1)
    row = lax.broadcasted_iota(jnp.int32, (kb, kb), 0)
    col = lax.broadcasted_iota(jnp.int32, (kb, kb), 1)
    suffix = (row >= col).astype(BF16)
    suffix2 = jnp.concatenate([suffix, suffix], axis=0)
    qm = [jnp.where(lane // A_HEAD_DIM == e, q, 0.0).astype(BF16) for e in range(2)]
    bias = [bias_ref[2 * hp + e] * LOG2E for e in range(2)]
    per_q = qb // kb
    n_blocks = per_q * (i + 1)

    def rows(t):
        return pl.ds(pl.multiple_of(jnp.maximum(n_blocks - 1 - t, 0) * kb, kb), kb)

    def mask(t):
        if not isinstance(t, int) or t >= per_q:
            return None
        q_pos = lax.broadcasted_iota(jnp.int32, (qb, kb), 0)
        k_pos = lax.broadcasted_iota(jnp.int32, (qb, kb), 1) + (per_q - 1 - t) * kb
        return k_pos < q_pos

    def logits(t, slot, e):
        z_ref[slot, e] = _bdot_nt(qm[e], k_ref[0, rows(t), :]) + bias[e]

    def suffix_sums(t, slot, e):
        s_ref[slot, e] = _sb_suffix(z_ref[slot, e], suffix2, mask(t))

    def consume(t, slot, e):
        s_incl = s_ref[slot, e]
        p = _bdot(_sb_local(z_ref[slot, e], s_incl, mask(t)), v_ref[0, rows(t), :])
        acc_ref[e] += jnp.exp2(-c_ref[e]) * p
        c_ref[e] += s_incl[:, 0:1]

    def trip(t, slot):
        for e in range(2):
            consume(t, slot, e)
            logits(t + 2, slot, e)
            suffix_sums(t + 1, 1 - slot, e)

    acc_ref[...] = jnp.zeros_like(acc_ref)
    c_ref[...] = jnp.zeros_like(c_ref)
    for e in range(2):
        logits(0, 0, e)
        logits(1, 1, e)
        suffix_sums(0, 0, e)
    for t in range(per_q):
        trip(t, t % 2)

    def pair(u, _):
        trip(2 * u + per_q, 0)
        trip(2 * u + per_q + 1, 1)
        return 0

    lax.fori_loop(0, (per_q // 2) * i, pair, 0)
    o_ref[0] = jnp.where(lane < A_HEAD_DIM, acc_ref[0], acc_ref[1])


def _sb_prompt(q, k, v, logit_bias, qb, kb):
    bsz, t, d = q.shape
    n_pairs = d // LANES
    assert qb % (2 * kb) == 0 and t % qb == 0
    grid_spec = pltpu.PrefetchScalarGridSpec(
        num_scalar_prefetch=1,
        grid=(bsz, n_pairs, t // qb),
        in_specs=[pl.BlockSpec((1, qb, LANES), lambda b, p, i, bias: (b, i, p)),
                  pl.BlockSpec((1, t, LANES), lambda b, p, i, bias: (b, 0, p)),
                  pl.BlockSpec((1, t, LANES), lambda b, p, i, bias: (b, 0, p))],
        out_specs=pl.BlockSpec((1, qb, LANES), lambda b, p, i, bias: (b, i, p)),
        scratch_shapes=[pltpu.VMEM((2, 2, qb, kb), F32), pltpu.VMEM((2, 2, qb, kb), F32),
                        pltpu.VMEM((2, qb, LANES), F32), pltpu.VMEM((2, qb, 1), F32)],
    )
    return pl.pallas_call(
        functools.partial(_sb_prompt_kernel, qb=qb, kb=kb),
        grid_spec=grid_spec,
        out_shape=jax.ShapeDtypeStruct(q.shape, F32),
        compiler_params=_params("parallel", "parallel", "arbitrary"),
        name="sb_prompt",
    )(logit_bias, q, k, v)


def _sb_decode_kernel(pt_ref, qbd_ref, bias_ref, kn_ref, vn_ref, *rest, pages_per_step, page, t_new):
    k_refs = rest[:pages_per_step]
    v_refs = rest[pages_per_step:2 * pages_per_step]
    o_ref = rest[2 * pages_per_step]
    acc_ref, c_ref = rest[2 * pages_per_step + 1:]
    s = pl.program_id(1)
    d = kn_ref.shape[2]
    n_rows = qbd_ref.shape[1]
    qbd = qbd_ref[0]
    bias = bias_ref[...]
    row = lax.broadcasted_iota(jnp.int32, (page, page), 0)
    col = lax.broadcasted_iota(jnp.int32, (page, page), 1)
    suffix = (row >= col).astype(BF16)
    suffix2 = jnp.concatenate([suffix, suffix], axis=0)

    def accumulate(p, tot):
        acc_ref[...] += jnp.exp2(-c_ref[...]) * p
        c_ref[...] += tot

    @pl.when(s == 0)
    def _():
        acc_ref[...] = jnp.zeros_like(acc_ref)
        c_ref[...] = jnp.zeros_like(c_ref)
        pad = jnp.zeros((page - t_new, d), F32)
        k_pad = jnp.concatenate([kn_ref[0], pad], axis=0).astype(BF16)
        v_pad = jnp.concatenate([vn_ref[0], pad], axis=0).astype(BF16)
        q_t = lax.broadcasted_iota(jnp.int32, (n_rows, page), 0) % t_new
        key_i = lax.broadcasted_iota(jnp.int32, (n_rows, page), 1)
        w, tot = _sb_weights(_bdot_nt(qbd, k_pad) + bias, suffix2, key_i < q_t)
        accumulate(jnp.dot(w, v_pad, preferred_element_type=F32), tot)

    zs = [jnp.dot(qbd, k_refs[r][0, 0].astype(BF16), preferred_element_type=F32) + bias
          for r in range(pages_per_step)]
    ws = [_sb_weights(z, suffix2, None) for z in zs]
    ps = [_bdot_nt(w, v_refs[r][0, 0]) for r, (w, _) in enumerate(ws)]
    for p, (_, tot) in zip(ps, ws):
        accumulate(p, tot)

    @pl.when(s == pl.num_programs(1) - 1)
    def _():
        n_heads = d // A_HEAD_DIM
        r_head = lax.broadcasted_iota(jnp.int32, (n_rows, d), 0) // t_new
        c_head = lax.broadcasted_iota(jnp.int32, (n_rows, d), 1) // A_HEAD_DIM
        diag = jnp.where(r_head == c_head, acc_ref[...], 0.0)
        out = diag[0:t_new]
        for h in range(1, n_heads):
            out = out + diag[h * t_new:(h + 1) * t_new]
        o_ref[0] = out


def _sb_decode(q, k_new, v_new, cache_k, cache_v, layer, page_table, logit_bias, pages_per_step):
    bsz, t_new, d = q.shape
    n_heads = d // A_HEAD_DIM
    n_layers, n_pool, page = cache_k.shape[:3]
    n_pages = page_table.shape[1]
    n_rows = n_heads * t_new
    ck = cache_k.transpose(0, 1, 3, 4, 2).reshape(n_layers, n_pool, d, page)
    cv = cache_v.transpose(0, 1, 3, 4, 2).reshape(n_layers, n_pool, d, page)
    qh = q.reshape(bsz, t_new, n_heads, A_HEAD_DIM)
    eye = jnp.eye(n_heads, dtype=F32)
    qbd = (qh.transpose(0, 2, 1, 3)[:, :, :, None, :] * eye[None, :, None, :, None])
    qbd = qbd.reshape(bsz, n_rows, d).astype(BF16)
    bias_rows = jnp.broadcast_to(jnp.repeat(logit_bias * LOG2E, t_new)[:, None], (n_rows, page))
    n_steps = n_pages // pages_per_step

    def page_map(r):
        def index(b, s, pt):
            return (layer, pt[b, n_pages - 1 - (s * pages_per_step + r)], 0, 0)
        return index

    kv_specs = [pl.BlockSpec((1, 1, d, page), page_map(r)) for r in range(pages_per_step)]
    grid_spec = pltpu.PrefetchScalarGridSpec(
        num_scalar_prefetch=1,
        grid=(bsz, n_steps),
        in_specs=[pl.BlockSpec((1, n_rows, d), lambda b, s, pt: (b, 0, 0)),
                  pl.BlockSpec((n_rows, page), lambda b, s, pt: (0, 0)),
                  pl.BlockSpec((1, t_new, d), lambda b, s, pt: (b, 0, 0)),
                  pl.BlockSpec((1, t_new, d), lambda b, s, pt: (b, 0, 0))] + kv_specs + kv_specs,
        out_specs=pl.BlockSpec((1, t_new, d), lambda b, s, pt: (b, 0, 0)),
        scratch_shapes=[pltpu.VMEM((n_rows, d), F32), pltpu.VMEM((n_rows, 1), F32)],
    )
    return pl.pallas_call(
        functools.partial(_sb_decode_kernel, pages_per_step=pages_per_step, page=page, t_new=t_new),
        grid_spec=grid_spec,
        out_shape=jax.ShapeDtypeStruct(q.shape, F32),
        compiler_params=_params("parallel", "arbitrary"),
        name="sb_decode",
    )(page_table, qbd, bias_rows, k_new, v_new, *([ck] * pages_per_step), *([cv] * pages_per_step))


def _proj_res_kernel(a_ref, w_ref, x_ref, gt_ref, o_ref):
    o_ref[0] = x_ref[0] + gt_ref[0] * _bdot(a_ref[0], w_ref[...])


def _proj_res(a, w, x, gt, tm):
    bsz, t, d = x.shape
    return pl.pallas_call(
        _proj_res_kernel,
        grid=(bsz, t // tm),
        in_specs=[_tok_spec(tm, d), _const_spec(w), _tok_spec(tm, d), _mod_spec(gt, tm)],
        out_specs=_tok_spec(tm, d),
        out_shape=jax.ShapeDtypeStruct(x.shape, F32),
        compiler_params=_params("parallel", "parallel"),
        name="proj_res",
    )(a, w, x, gt)


def _gdn_out_kernel(o_ref_in, z_ref, on_ref, w_ref, x_ref, gt_ref, o_ref):
    o = o_ref_in[0]
    gmat = _group_matrix(MXU_DIM, B_HEAD_DIM, 1.0 / B_HEAD_DIM)
    ms = _group_reduce(o * o, gmat)
    a = o * lax.rsqrt(ms + NORM_EPS) * on_ref[...] * _silu(z_ref[0])
    o_ref[0] = x_ref[0] + gt_ref[0] * _bdot(a, w_ref[...])


def _gdn_out(o, z, o_norm, w, x, gt, tm):
    bsz, t, d = x.shape
    on = jnp.tile(o_norm, d // B_HEAD_DIM).reshape(1, d)
    return pl.pallas_call(
        _gdn_out_kernel,
        grid=(bsz, t // tm),
        in_specs=[_tok_spec(tm, d), _tok_spec(tm, d), _const_spec(on), _const_spec(w),
                  _tok_spec(tm, d), _mod_spec(gt, tm)],
        out_specs=_tok_spec(tm, d),
        out_shape=jax.ShapeDtypeStruct(x.shape, F32),
        compiler_params=_params("parallel", "parallel"),
        name="gdn_out",
    )(o, z, on, w, x, gt)


def _rwkv_out_kernel(y_ref, r_ref, k_ref, v_ref, gate_ref, rk_ref, lg_ref, lb_ref, w_ref,
                     x_ref, gt_ref, o_ref):
    y = y_ref[0]
    mean_mat = _group_matrix(MXU_DIM, C_HEAD_DIM, 1.0 / C_HEAD_DIM)
    sum_mat = _group_matrix(MXU_DIM, C_HEAD_DIM, 1.0)
    yc = y - _group_reduce(y, mean_mat, split=True)
    var = _group_reduce(yc * yc, mean_mat)
    yn = yc * lax.rsqrt(var + GN_EPS) * lg_ref[...] + lb_ref[...]
    bonus = _group_reduce(r_ref[0] * k_ref[0] * rk_ref[...], sum_mat) * v_ref[0]
    a = (yn + bonus) * gate_ref[0]
    o_ref[0] = x_ref[0] + gt_ref[0] * _bdot(a, w_ref[...])


def _rwkv_out(y, r, k, v, gate, r_k, ln_g, ln_b, w, x, gt, tm):
    bsz, t, d = x.shape
    rk = r_k.reshape(1, d)
    lg = ln_g.reshape(1, d)
    lb = ln_b.reshape(1, d)
    tok = _tok_spec(tm, d)
    return pl.pallas_call(
        _rwkv_out_kernel,
        grid=(bsz, t // tm),
        in_specs=[tok, tok, tok, tok, tok, _const_spec(rk), _const_spec(lg), _const_spec(lb),
                  _const_spec(w), tok, _mod_spec(gt, tm)],
        out_specs=tok,
        out_shape=jax.ShapeDtypeStruct(x.shape, F32),
        compiler_params=_params("parallel", "parallel"),
        name="rwkv_out",
    )(y, r, k, v, gate, rk, lg, lb, w, x, gt)


def _gdn_in_kernel(x_ref, g_ref, sc_ref, sh_ref, w_ref, xqkv_ref, z_ref, ba_ref):
    n_conv = xqkv_ref.shape[2]
    n_z = z_ref.shape[2]
    h = _normmod(x_ref[0], g_ref[...], sc_ref[0], sh_ref[0]).astype(BF16)
    for s in range(0, n_conv, MXU_DIM):
        xqkv_ref[0, :, s:s + MXU_DIM] = jnp.dot(h, w_ref[:, s:s + MXU_DIM],
                                                preferred_element_type=F32)
    for s in range(0, n_z, MXU_DIM):
        z_ref[0, :, s:s + MXU_DIM] = jnp.dot(h, w_ref[:, n_conv + s:n_conv + s + MXU_DIM],
                                             preferred_element_type=F32)
    ba_ref[0] = jnp.dot(h, w_ref[:, n_conv + n_z:], preferred_element_type=F32)


def _gdn_in(x, g, sc, sh, w_pad, n_conv, tm):
    bsz, t, d = x.shape
    return pl.pallas_call(
        _gdn_in_kernel,
        grid=(bsz, t // tm),
        in_specs=[_tok_spec(tm, d), _const_spec(g), _mod_spec(sc, tm), _mod_spec(sh, tm),
                  _const_spec(w_pad)],
        out_specs=[_tok_spec(tm, n_conv), _tok_spec(tm, d), _tok_spec(tm, LANES)],
        out_shape=[jax.ShapeDtypeStruct((bsz, t, n_conv), F32),
                   jax.ShapeDtypeStruct((bsz, t, d), F32),
                   jax.ShapeDtypeStruct((bsz, t, LANES), F32)],
        compiler_params=_params("parallel", "parallel"),
        name="gdn_in",
    )(x, g, sc, sh, w_pad)


def _gdn_conv_kernel(*refs, pre_shifted):
    if pre_shifted:
        x_ref, xs_ref, cw_ref, ba_ref, alog_ref, dtb_ref = refs[:6]
    else:
        x_ref, p8_ref, cw_ref, ba_ref, alog_ref, dtb_ref = refs[:6]
    q_ref, k_ref, kb_ref, vb_ref, g_ref = refs[6:]
    i = pl.program_id(1)
    tm = x_ref.shape[1]
    hd = q_ref.shape[2]
    n_heads = hd // B_HEAD_DIM
    x = x_ref[0]
    conv = x * cw_ref[B_CONV - 1:B_CONV, :]
    if pre_shifted:
        for s in range(1, B_CONV):
            conv = conv + xs_ref[s - 1, 0] * cw_ref[B_CONV - 1 - s:B_CONV - s, :]
    else:
        full = jnp.concatenate([p8_ref[0], x], axis=0)
        t_glob = i * tm + lax.broadcasted_iota(jnp.int32, (tm, 1), 0)
        for s in range(1, B_CONV):
            xs = pltpu.roll(full, s, 0)[8:]
            xs = jnp.where(t_glob >= s, xs, 0.0)
            conv = conv + xs * cw_ref[B_CONV - 1 - s:B_CONV - s, :]
    act = _silu(conv)
    r = lax.broadcasted_iota(jnp.int32, (LANES, hd), 0)
    c_head = lax.broadcasted_iota(jnp.int32, (LANES, hd), 1) // B_HEAD_DIM
    e_b = (r == c_head).astype(BF16)
    e_a = (r == c_head + n_heads).astype(BF16)
    ba = ba_ref[0]
    beta = _sigmoid(_dot_xe(ba, e_b))
    g = -jnp.exp(alog_ref[...]) * _softplus(_dot_xe(ba, e_a) + dtb_ref[...])
    g_ref[0] = g
    ones = _group_matrix(MXU_DIM, B_HEAD_DIM, 1.0)
    aq = act[:, :hd]
    ak = act[:, hd:2 * hd]
    q_ref[0] = aq * lax.rsqrt(_group_reduce(aq * aq, ones) + L2_EPS) * (B_HEAD_DIM ** -0.5)
    k = ak * lax.rsqrt(_group_reduce(ak * ak, ones) + L2_EPS)
    k_ref[0] = k
    kb_ref[0] = k * beta
    vb_ref[0] = act[:, 2 * hd:] * beta


def _gdn_conv(xqkv, xs, conv_w, ba, a_log, dt_bias, hd, tm):
    bsz, t, n_conv = xqkv.shape
    alog = jnp.repeat(a_log, B_HEAD_DIM).reshape(1, hd)
    dtb = jnp.repeat(dt_bias, B_HEAD_DIM).reshape(1, hd)
    pre_shifted = xs is not None
    if pre_shifted:
        second = xs
        second_spec = pl.BlockSpec((B_CONV - 1, 1, tm, n_conv), lambda b, i: (0, b, i, 0))
    else:
        second = xqkv
        second_spec = _prev8_spec(tm, n_conv)
    shp = jax.ShapeDtypeStruct((bsz, t, hd), F32)
    return pl.pallas_call(
        functools.partial(_gdn_conv_kernel, pre_shifted=pre_shifted),
        grid=(bsz, t // tm),
        in_specs=[_tok_spec(tm, n_conv), second_spec, _const_spec(conv_w), _tok_spec(tm, LANES),
                  _const_spec(alog), _const_spec(dtb)],
        out_specs=[_tok_spec(tm, hd)] * 5,
        out_shape=[shp] * 5,
        compiler_params=_params("parallel", "parallel"),
        name="gdn_conv",
    )(xqkv, second, conv_w, ba, alog, dtb)


def _gdn_chunk_kernel(q_ref, k_ref, kb_ref, vb_ref, g_ref, s0_ref, o_ref, s_out_ref, s_ref, *,
                      n_heads, n_chunks):
    i = pl.program_id(2)
    c = CHUNK
    hw = B_HEAD_DIM

    @pl.when(i == 0)
    def _():
        s_ref[...] = s0_ref[0]

    row = lax.broadcasted_iota(jnp.int32, (c, c), 0)
    col = lax.broadcasted_iota(jnp.int32, (c, c), 1)
    lower = row >= col
    strict = row > col
    lower_b = lower.astype(BF16)
    ones_avg = jnp.full((c, LANES), 1.0 / LANES, BF16)

    def row_layout(g_cb):
        gh, gl = _split(g_cb)
        nt = lambda x: lax.dot_general(ones_avg, x, (((1,), (1,)), ((), ())),
                                       preferred_element_type=F32)
        return nt(gh) + nt(gl)

    sls = [(slice(n * c, (n + 1) * c), slice(h * hw, (h + 1) * hw))
           for h in range(n_heads) for n in range(n_chunks)]
    rng = range(len(sls))
    q = [q_ref[0, rs, ls] for rs, ls in sls]
    k = [k_ref[0, rs, ls] for rs, ls in sls]
    kb = [kb_ref[0, rs, ls] for rs, ls in sls]
    vb = [vb_ref[0, rs, ls] for rs, ls in sls]
    g_cb = [_dot_ex(lower_b, g_ref[0, rs, ls]) for rs, ls in sls]
    g_row = [row_layout(x) for x in g_cb]
    decay = [jnp.where(lower, jnp.exp(jnp.where(lower, g_cb[n][:, :c] - g_row[n], 0.0)), 0.0)
             for n in rng]
    a = [jnp.where(strict, decay[n] * _bdot_nt(kb[n], k[n]), 0.0) for n in rng]
    t_inv = _tri_inverses(a, 5)
    e_g = [jnp.exp(x) for x in g_cb]
    t_parts = [_split(x) for x in t_inv]
    w_mat = [_dot3_parts(t_parts[n], _split(kb[n] * e_g[n])) for n in rng]
    u0 = [_dot3_parts(t_parts[n], _split(vb[n])) for n in rng]
    qk = [decay[n] * _bdot_nt(q[n], k[n]) for n in rng]
    g_last = [x[c - 1:c, :] for x in g_cb]
    k_end = [jnp.exp(g_last[n] - g_cb[n]) * k[n] for n in rng]
    q_eff = [q[n] * e_g[n] - _bdot(qk[n], w_mat[n]) for n in rng]
    o0 = [_bdot(qk[n], u0[n]) for n in rng]
    m0 = [_bdot_tn(k_end[n], w_mat[n]) for n in rng]
    n0 = [_bdot_tn(k_end[n], u0[n]) for n in rng]

    for h in range(n_heads):
        s = s_ref[h]
        for n in range(h * n_chunks, (h + 1) * n_chunks):
            rs, ls = sls[n]
            o_ref[0, rs, ls] = _bdot(q_eff[n], s) + o0[n]
            s = jnp.exp(g_last[n]) * s - _bdot(m0[n], s) + n0[n]
        s_ref[h] = s
        s_out_ref[0, h] = s


def _gdn_chunk(q, k, kb, vb, g, s0, ct, heads_per_step):
    bsz, t, hd = q.shape
    hw = B_HEAD_DIM
    blk = pl.BlockSpec((1, ct, heads_per_step * hw), lambda b, h, i: (b, i, h))
    st = pl.BlockSpec((1, heads_per_step, hw, hw), lambda b, h, i: (b, h, 0, 0))
    return pl.pallas_call(
        functools.partial(_gdn_chunk_kernel, n_heads=heads_per_step, n_chunks=ct // CHUNK),
        grid=(bsz, hd // (heads_per_step * hw), t // ct),
        in_specs=[blk] * 5 + [st],
        out_specs=[blk, st],
        out_shape=[jax.ShapeDtypeStruct(q.shape, F32), jax.ShapeDtypeStruct(s0.shape, F32)],
        scratch_shapes=[pltpu.VMEM((heads_per_step, hw, hw), F32)],
        compiler_params=_params("parallel", "parallel", "arbitrary"),
        name="gdn_chunk",
    )(q, k, kb, vb, g, s0)


def _rwkv_in_kernel(*refs, seq_len, per_row_shift):
    (x_ref, p8_ref, g_ref, sc_ref, sh_ref, sh0_ref, mu_ref, wrkv_ref, w1_ref, w2_ref, a1_ref, a2_ref,
     g1_ref, g2_ref, w0_ref, a0_ref, kk_ref, ka_ref) = refs[:18]
    r_ref, lw_ref, k_ref, v_ref, kn_ref, bb_ref, gate_ref, hl_ref = refs[18:]
    i = pl.program_id(1)
    tm = x_ref.shape[1]
    g = g_ref[...]
    h = _normmod(x_ref[0], g, sc_ref[0], sh_ref[0])
    if per_row_shift:
        sc8, sh8 = sc_ref[0, 0:8], sh_ref[0, 0:8]
    else:
        sc8, sh8 = sc_ref[0], sh_ref[0]
    h8 = _normmod(p8_ref[0], g, sc8, sh8)
    prev = pltpu.roll(jnp.concatenate([h8, h], axis=0), 1, 0)[8:]
    t_glob = i * tm + lax.broadcasted_iota(jnp.int32, (tm, 1), 0)
    prev = jnp.where(t_glob % seq_len == 0, sh0_ref[0], prev)
    xx = prev - h
    mix = lambda n: (h + xx * mu_ref[n:n + 1, :]).astype(BF16)
    r = jnp.dot(mix(0), wrkv_ref[0], preferred_element_type=F32)
    k = jnp.dot(mix(1), wrkv_ref[1], preferred_element_type=F32)
    v_ref[0] = jnp.dot(mix(2), wrkv_ref[2], preferred_element_type=F32)
    r_ref[0] = r
    wl = w0_ref[...] + _bdot(jnp.tanh(jnp.dot(mix(3), w1_ref[...], preferred_element_type=F32)),
                             w2_ref[...])
    lw_ref[0] = -jnp.exp(-_softplus(-wl) - 0.5)
    a = _sigmoid(a0_ref[...] + _bdot(jnp.dot(mix(4), a1_ref[...], preferred_element_type=F32),
                                     a2_ref[...]))
    gate_ref[0] = _bdot(_sigmoid(jnp.dot(mix(5), g1_ref[...], preferred_element_type=F32)),
                        g2_ref[...])
    kx = k * kk_ref[...]
    ones = _group_matrix(MXU_DIM, C_HEAD_DIM, 1.0)
    kn = kx * lax.rsqrt(_group_reduce(kx * kx, ones) + L2_EPS)
    kn_ref[0] = kn
    bb_ref[0] = kn * a
    k_ref[0] = k * (1.0 + (a - 1.0) * ka_ref[...])
    hl_ref[0, 0] = h[tm - 8:tm]


def _rwkv_in(x, g, sc, sh, shift0, seq_len, P, tm):
    bsz, t, d = x.shape
    per_row = shift0.shape[1] != 1
    consts = [P['mu'], P['w_rkv'], P['w1'], P['w2'], P['a1'], P['a2'], P['g1'], P['g2'],
              P['w0'], P['a0'], P['k_k'], P['k_a']]
    shp = jax.ShapeDtypeStruct(x.shape, F32)
    n_t = t // tm
    outs = pl.pallas_call(
        functools.partial(_rwkv_in_kernel, seq_len=seq_len, per_row_shift=per_row),
        grid=(bsz, n_t),
        in_specs=[_tok_spec(tm, d), _prev8_spec(tm, d), _const_spec(g), _mod_spec(sc, tm),
                  _mod_spec(sh, tm), _mod_spec(shift0, tm)] + [_const_spec(c) for c in consts],
        out_specs=[_tok_spec(tm, d)] * 7 + [pl.BlockSpec((1, 1, 8, d), lambda b, i: (b, i, 0, 0))],
        out_shape=[shp] * 7 + [jax.ShapeDtypeStruct((bsz, n_t, 8, d), F32)],
        compiler_params=_params("parallel", "parallel"),
        name="rwkv_in",
    )(x, x, g, sc, sh, shift0, *consts)
    return outs


def _rwkv_scan_kernel(r_ref, lw_ref, k_ref, v_ref, kn_ref, bb_ref, s0_ref, y_ref, s_out_ref, s_ref,
                      *, n_pairs, n_chunks):
    i = pl.program_id(2)
    c = CHUNK
    n = 2 * c

    @pl.when(i == 0)
    def _():
        s_ref[...] = s0_ref[0]

    lane = lax.broadcasted_iota(jnp.int32, (1, LANES), 1)
    row = lax.broadcasted_iota(jnp.int32, (n, n), 0)
    col = lax.broadcasted_iota(jnp.int32, (n, n), 1)
    same = (row // c) == (col // c)
    incl = same & (row >= col)
    strict = same & (row > col)
    incl_b = incl.astype(BF16)

    def stack(x):
        return jnp.concatenate([jnp.where(lane < C_HEAD_DIM, x, 0.0),
                                jnp.where(lane >= C_HEAD_DIM, x, 0.0)], axis=0)

    sls = [(slice(m * c, (m + 1) * c), slice(p * LANES, (p + 1) * LANES))
           for p in range(n_pairs) for m in range(n_chunks)]
    rng = range(len(sls))
    lw = [stack(lw_ref[0, rs, ls]) for rs, ls in sls]
    lc = [_dot_ex(incl_b, x) for x in lw]
    lc_last = [x[c - 1:c, :] + x[n - 1:n, :] for x in lc]
    p_inv = [jnp.exp(-x) for x in lc]
    p_end = [jnp.exp(lc_last[m] - lc[m]) for m in rng]
    kn_t = [stack(kn_ref[0, sls[m][0], sls[m][1]]) * jnp.exp(lc[m] - lw[m]) for m in rng]
    k_s = [stack(k_ref[0, rs, ls]) for rs, ls in sls]
    b_s = [stack(bb_ref[0, rs, ls]) for rs, ls in sls]
    v_s = [stack(v_ref[0, rs, ls]) for rs, ls in sls]
    k_t = [k_s[m] * p_inv[m] for m in rng]
    b_t = [b_s[m] * p_inv[m] for m in rng]
    r_t = [stack(r_ref[0, sls[m][0], sls[m][1]]) * jnp.exp(lc[m]) for m in rng]
    a_kb = [jnp.where(strict, _bdot_nt(kn_t[m], b_t[m]), 0.0) for m in rng]
    a_kk = [jnp.where(strict, _bdot_nt(kn_t[m], k_t[m]), 0.0) for m in rng]
    a_rk = [jnp.where(incl, _bdot_nt(r_t[m], k_t[m]), 0.0) for m in rng]
    a_rb = [jnp.where(incl, _bdot_nt(r_t[m], b_t[m]), 0.0) for m in rng]
    t_inv = _tri_inverses(a_kb, 5)
    t_parts = [_split(x) for x in t_inv]
    kt = [_dot3_parts(t_parts[m], _split(kn_t[m])) for m in rng]
    u0 = [_dot3_parts(t_parts[m], _split(_bdot(a_kk[m], v_s[m]))) for m in rng]
    k_e = [k_s[m] * p_end[m] for m in rng]
    b_e = [b_s[m] * p_end[m] for m in rng]
    g_mat = [_bdot_tn(kt[m], b_e[m]) for m in rng]
    n0 = [_bdot_tn(v_s[m], k_e[m]) - _bdot_tn(u0[m], b_e[m]) for m in rng]
    r_eff = [r_t[m] - _bdot(a_rb[m], kt[m]) for m in rng]
    y0 = [_bdot(a_rk[m], v_s[m]) - _bdot(a_rb[m], u0[m]) for m in rng]

    for p in range(n_pairs):
        s = s_ref[p]
        for m in range(p * n_chunks, (p + 1) * n_chunks):
            rs, ls = sls[m]
            y = _bdot_nt(r_eff[m], s) + y0[m]
            y_ref[0, rs, ls] = y[:c] + y[c:]
            s = s * jnp.exp(lc_last[m]) - _bdot(s, g_mat[m]) + n0[m]
        s_ref[p] = s
        s_out_ref[0, p] = s


def _rwkv_scan(r, lw, k, v, kn, bb, s0_pairs, ct, pairs_per_step):
    bsz, t, d = r.shape
    blk = pl.BlockSpec((1, ct, pairs_per_step * LANES), lambda b, p, i: (b, i, p))
    st = pl.BlockSpec((1, pairs_per_step, LANES, LANES), lambda b, p, i: (b, p, 0, 0))
    return pl.pallas_call(
        functools.partial(_rwkv_scan_kernel, n_pairs=pairs_per_step, n_chunks=ct // CHUNK),
        grid=(bsz, d // (pairs_per_step * LANES), t // ct),
        in_specs=[blk] * 6 + [st],
        out_specs=[blk, st],
        out_shape=[jax.ShapeDtypeStruct(r.shape, F32), jax.ShapeDtypeStruct(s0_pairs.shape, F32)],
        scratch_shapes=[pltpu.VMEM((pairs_per_step, LANES, LANES), F32)],
        compiler_params=_params("parallel", "parallel", "arbitrary"),
        name="rwkv_scan",
    )(r, lw, k, v, kn, bb, s0_pairs)


def _pad_time(a, t_pad):
    return jnp.pad(a, ((0, 0), (0, t_pad - a.shape[1]), (0, 0)))


def _pairs_from_heads(s):
    bsz, n_h, n, _ = s.shape
    sp = s.reshape(bsz, n_h // 2, 2, n, n)
    z = jnp.zeros_like(sp[:, :, 0])
    top = jnp.concatenate([sp[:, :, 0], z], axis=-1)
    bot = jnp.concatenate([z, sp[:, :, 1]], axis=-1)
    return jnp.concatenate([top, bot], axis=-2)


def _heads_from_pairs(sp):
    bsz, n_p, n2, _ = sp.shape
    n = n2 // 2
    return jnp.stack([sp[:, :, :n, :n], sp[:, :, n:, n:]], axis=2).reshape(bsz, 2 * n_p, n, n)


def _run_trunk(x, mods, W, tm, seq_len, cache, state):
    bsz, t, d = x.shape
    n_seq = bsz * t // seq_len
    depth = len(mods)
    ks, vs, bconv, bssm, cshift, cwkv = [], [], [], [], [], []
    for l in range(depth):
        kind, j = l % 3, l // 3
        sh1, sc1, gt1, sh2, sc2, gt2 = mods[l]
        g_mix = W['ln_mix'][l].reshape(1, d)
        if kind == 0:
            q, k, v = _qkv(x, g_mix, sc1, sh1, W['a_w_qkv'][j], W['a_q_norm'][j], W['a_k_norm'][j], tm)
            if cache is None:
                o = _sb_prompt(q, k, v, W['a_logit_bias'][j], qb=SB_QUERY_BLOCK, kb=SB_KEY_BLOCK)
            else:
                cache_k, cache_v, page_table = cache
                o = _sb_decode(q.reshape(n_seq, seq_len, d), k.reshape(n_seq, seq_len, d),
                               v.reshape(n_seq, seq_len, d), cache_k, cache_v, j, page_table,
                               W['a_logit_bias'][j],
                               pages_per_step=min(8, page_table.shape[1])).reshape(bsz, t, d)
            ks.append(k.reshape(n_seq, seq_len, d // A_HEAD_DIM, A_HEAD_DIM))
            vs.append(v.reshape(n_seq, seq_len, d // A_HEAD_DIM, A_HEAD_DIM))
            x = _proj_res(o, W['a_w_o'][j], x, gt1, tm)
        elif kind == 1:
            n_conv = 3 * d
            xqkv, z, ba = _gdn_in(x, g_mix, sc1, sh1, W['b_w_in'][j], n_conv, tm)
            xq_seq = xqkv.reshape(n_seq, seq_len, n_conv)
            if state is None:
                xs = None
                bconv.append(xq_seq[:, seq_len - (B_CONV - 1):])
                s0 = jnp.zeros((n_seq, d // B_HEAD_DIM, B_HEAD_DIM, B_HEAD_DIM), F32)
            else:
                xc = jnp.concatenate([state['b_conv'][j], xq_seq], axis=1)
                xs = jnp.stack([xc[:, B_CONV - 1 - s:B_CONV - 1 - s + seq_len].reshape(bsz, t, n_conv)
                                for s in range(1, B_CONV)])
                bconv.append(xc[:, seq_len:])
                s0 = state['b_ssm'][j]
            q, k, kb, vb, g = _gdn_conv(xqkv, xs, W['b_conv'][j], ba, W['b_a_log'][j],
                                        W['b_dt_bias'][j], d, min(tm, 256))
            t_pad = -(-seq_len // CHUNK) * CHUNK
            seqs = [_pad_time(a.reshape(n_seq, seq_len, d), t_pad) for a in (q, k, kb, vb, g)]
            ct = min(t_pad, CHAINS_PER_STEP * CHUNK)
            heads = max(1, min(d // B_HEAD_DIM, CHAINS_PER_STEP * CHUNK // ct))
            o, s_new = _gdn_chunk(*seqs, s0, ct=ct, heads_per_step=heads)
            bssm.append(s_new)
            o = o[:, :seq_len].reshape(bsz, t, d)
            x = _gdn_out(o, z, W['b_o_norm'][j], W['b_w_o'][j], x, gt1, tm)
        else:
            if state is None:
                shift0 = jnp.zeros((bsz, 1, d), F32)
                s0 = jnp.zeros((n_seq, d // C_HEAD_DIM, C_HEAD_DIM, C_HEAD_DIM), F32)
            else:
                shift0 = jnp.repeat(state['c_shift'][j], seq_len, axis=0).reshape(bsz, t, d)
                s0 = state['c_wkv'][j]
            tm_c = min(tm, 256)
            r, lw, k, v, kn, bb, gate, h_last = _rwkv_in(x, g_mix, sc1, sh1, shift0, seq_len,
                                                        W['c'][j], tm_c)
            if state is None:
                cshift.append(h_last[:, -1, 7])
            else:
                assert t == tm_c and seq_len == 8
                cshift.append(_last_rows(x, g_mix, sc1, sh1, seq_len))
            t_pad = -(-seq_len // CHUNK) * CHUNK
            seqs = [_pad_time(a.reshape(n_seq, seq_len, d), t_pad) for a in (r, lw, k, v, kn, bb)]
            ct = min(t_pad, CHAINS_PER_STEP * CHUNK)
            pairs = max(1, min(d // LANES, CHAINS_PER_STEP * CHUNK // ct))
            y, s_new = _rwkv_scan(*seqs, _pairs_from_heads(s0), ct=ct, pairs_per_step=pairs)
            cwkv.append(_heads_from_pairs(s_new))
            y = y[:, :seq_len].reshape(bsz, t, d)
            x = _rwkv_out(y, r, k, v, gate, W['c_r_k'][j], W['c_ln_g'][j], W['c_ln_b'][j],
                          W['c_w_o'][j], x, gt1, tm)
        x = _mlp(x, W['ln_mlp'][l].reshape(1, d), sc2, sh2, gt2, W['w_up'][l], W['w_down'][l],
                 MLP_TOKEN_TILE if t % MLP_TOKEN_TILE == 0 else tm)
    return x, ks, vs, bconv, bssm, cshift, cwkv


def _hmod_kernel(x_ref, g_ref, sc_ref, sh_ref, o_ref):
    o_ref[0] = _normmod(x_ref[0], g_ref[...], sc_ref[0], sh_ref[0])


def _last_rows(x, g, sc, sh, seq_len):
    bsz, t, d = x.shape
    n_seq = bsz * t // seq_len
    pick = lambda a: a.reshape(n_seq, seq_len, d)[:, seq_len - 1].reshape(1, n_seq, d)
    xs, scs, shs = pick(x), pick(sc), pick(sh)
    return pl.pallas_call(
        _hmod_kernel,
        grid=(1, 1),
        in_specs=[_tok_spec(n_seq, d), _const_spec(g), _tok_spec(n_seq, d), _tok_spec(n_seq, d)],
        out_specs=_tok_spec(n_seq, d),
        out_shape=jax.ShapeDtypeStruct((1, n_seq, d), F32),
        compiler_params=_params("parallel", "parallel"),
        name="last_rows",
    )(xs, g, scs, shs)[0]


def kernel(x_prompt, x_sample, c_prompt, c_sample, cache_k, cache_v, page_table, state_b_conv, state_b_ssm, state_c_shift, state_c_wkv, ln_mix, ln_mlp, w_ada, b_ada, w_up, w_down, a_w_qkv, a_q_norm, a_k_norm, a_logit_bias, a_w_o, b_w_in, b_conv, b_a_log, b_dt_bias, b_o_norm, b_w_o, c_mu, c_w_rkv, c_w0, c_w1, c_w2, c_a0, c_a1, c_a2, c_g1, c_g2, c_k_k, c_k_a, c_r_k, c_ln_g, c_ln_b, c_w_o):
    bp, t_p, d = x_prompt.shape
    bs, t_s, _ = x_sample.shape
    depth = ln_mix.shape[0]
    n_c = c_mu.shape[0]

    def lora_cols(w):
        return jnp.pad(w, ((0, 0), (0, 0), (0, LANES - w.shape[2]))).astype(BF16)

    def lora_rows(w):
        return jnp.pad(w, ((0, 0), (0, LANES - w.shape[1]), (0, 0))).astype(BF16)

    b_in = b_w_in.shape[2]
    b_in_pad = -(-b_in // LANES) * LANES
    W = {
        'ln_mix': ln_mix, 'ln_mlp': ln_mlp,
        'w_up': w_up.astype(BF16), 'w_down': w_down.astype(BF16),
        'a_w_qkv': a_w_qkv.astype(BF16), 'a_q_norm': a_q_norm, 'a_k_norm': a_k_norm,
        'a_logit_bias': a_logit_bias, 'a_w_o': a_w_o.astype(BF16),
        'b_w_in': jnp.pad(b_w_in, ((0, 0), (0, 0), (0, b_in_pad - b_in))).astype(BF16),
        'b_conv': b_conv, 'b_a_log': b_a_log, 'b_dt_bias': b_dt_bias, 'b_o_norm': b_o_norm,
        'b_w_o': b_w_o.astype(BF16),
        'c_r_k': c_r_k.reshape(n_c, d), 'c_ln_g': c_ln_g, 'c_ln_b': c_ln_b, 'c_w_o': c_w_o.astype(BF16),
    }
    w1, w2 = lora_cols(c_w1), lora_rows(c_w2)
    a1, a2 = lora_cols(c_a1), lora_rows(c_a2)
    g1, g2 = c_g1.astype(BF16), c_g2.astype(BF16)
    rkv = c_w_rkv.astype(BF16)
    W['c'] = [{'mu': c_mu[j], 'w_rkv': rkv[j], 'w1': w1[j], 'w2': w2[j], 'a1': a1[j], 'a2': a2[j],
               'g1': g1[j], 'g2': g2[j], 'w0': c_w0[j].reshape(1, d), 'a0': c_a0[j].reshape(1, d),
               'k_k': c_k_k[j].reshape(1, d), 'k_a': c_k_a[j].reshape(1, d)} for j in range(n_c)]

    mod_all = _ada(jnp.concatenate([c_prompt, c_sample], axis=0), w_ada, b_ada)
    mods_p, mods_s = [], []
    for l in range(depth):
        mp = mod_all[l, :bp].reshape(bp, 1, 6, d)
        mods_p.append([mp[:, :, n] for n in range(6)])
        ms = jnp.repeat(mod_all[l, bp:], t_s, axis=0).reshape(1, bs * t_s, 6, d)
        mods_s.append([ms[:, :, n] for n in range(6)])

    y_p, k_p, v_p, bconv_p, bssm_p, cshift_p, cwkv_p = _run_trunk(
        x_prompt, mods_p, W, 512, t_p, None, None)
    state = {'b_conv': state_b_conv, 'b_ssm': state_b_ssm, 'c_shift': state_c_shift,
             'c_wkv': state_c_wkv}
    y_s, k_s, v_s, bconv_s, bssm_s, cshift_s, cwkv_s = _run_trunk(
        x_sample.reshape(1, bs * t_s, d), mods_s, W, bs * t_s, t_s,
        (cache_k, cache_v, page_table), state)
    st = jnp.stack
    return (y_p, y_s.reshape(bs, t_s, d), st(k_p), st(v_p), st(k_s), st(v_s), st(bconv_p), st(bssm_p),
            st(bconv_s), st(bssm_s), st(cshift_p), st(cwkv_p), st(cshift_s), st(cwkv_s))
```

```python
import functools

import jax
import jax.numpy as jnp
from jax import lax
from jax.experimental import pallas as pl
from jax.experimental.pallas import tpu as pltpu

F32 = jnp.float32
BF16 = jnp.bfloat16

LANES = 128
MXU_DIM = 256
VMEM_LIMIT_BYTES = 56 * 1024 * 1024

NORM_EPS = 1e-6
L2_EPS = 1e-6
GN_EPS = 64e-5
LOG2E = 1.4426950408889634
A_HEAD_DIM = 64
B_HEAD_DIM = 128
C_HEAD_DIM = 64
B_CONV = 4
CHUNK = 64
CHAINS_PER_STEP = 8
SB_QUERY_BLOCK = 512
SB_KEY_BLOCK = MXU_DIM
MLP_TOKEN_TILE = 1024


def _params(*sem):
    return pltpu.CompilerParams(dimension_semantics=sem, vmem_limit_bytes=VMEM_LIMIT_BYTES)


def _bdot(a, b):
    return jnp.dot(a.astype(BF16), b.astype(BF16), preferred_element_type=F32)


def _bdot_nt(a, b):
    return lax.dot_general(a.astype(BF16), b.astype(BF16), (((1,), (1,)), ((), ())),
                           preferred_element_type=F32)


def _bdot_tn(a, b):
    return lax.dot_general(a.astype(BF16), b.astype(BF16), (((0,), (0,)), ((), ())),
                           preferred_element_type=F32)


def _split(x):
    hi = x.astype(BF16)
    lo = (x - hi.astype(F32)).astype(BF16)
    return hi, lo


def _sum_dots(lhs, rhs):
    if lhs[0].shape[1] % LANES == 0:
        return jnp.dot(jnp.concatenate(lhs, axis=1), jnp.concatenate(rhs, axis=0),
                       preferred_element_type=F32)
    out = jnp.dot(lhs[0], rhs[0], preferred_element_type=F32)
    for a, b in zip(lhs[1:], rhs[1:]):
        out = out + jnp.dot(a, b, preferred_element_type=F32)
    return out


def _dot_xe(x, e):
    hi, lo = _split(x)
    return _sum_dots([hi, lo], [e, e])


def _dot_ex(e, x):
    hi, lo = _split(x)
    return _sum_dots([e, e], [hi, lo])


def _dot3_parts(a_parts, b_parts):
    ah, al = a_parts
    bh, bl = b_parts
    return _sum_dots([ah, ah, al], [bh, bl, bh])


def _dot3(a, b):
    return _dot3_parts(_split(a), _split(b))


def _sigmoid(x):
    return 1.0 / (1.0 + jnp.exp(-x))


def _silu(x):
    return x * _sigmoid(x)


def _softplus(x):
    return jnp.maximum(x, 0.0) + jnp.log(1.0 + jnp.exp(-jnp.abs(x)))


def _neg_abs(x):
    bits = lax.bitcast_convert_type(x, jnp.uint32) | jnp.uint32(0x80000000)
    return lax.bitcast_convert_type(bits, F32)


def _sb_suffix(z, suffix2, mask):
    sp = jnp.maximum(z, 0.0) + jnp.log(1.0 + jnp.exp2(_neg_abs(z))) * LOG2E
    if mask is not None:
        sp = jnp.where(mask, sp, 0.0)
    hi, lo = _split(sp)
    return jnp.dot(jnp.concatenate([hi, lo], axis=1), suffix2, preferred_element_type=F32)


def _sb_local(z, s_incl, mask):
    w = jnp.exp2(z - s_incl)
    if mask is not None:
        w = jnp.where(mask, w, 0.0)
    return w.astype(BF16)


def _sb_weights(z, suffix2, mask):
    s_incl = _sb_suffix(z, suffix2, mask)
    return _sb_local(z, s_incl, mask), s_incl[:, 0:1]


def _normmod(x, g, sc, sh):
    ms = jnp.mean(x * x, axis=-1, keepdims=True)
    return x * lax.rsqrt(ms + NORM_EPS) * g * (1.0 + sc) + sh


def _group_matrix(n, group, value):
    r = lax.broadcasted_iota(jnp.int32, (n, n), 0) // group
    c = lax.broadcasted_iota(jnp.int32, (n, n), 1) // group
    return jnp.where(r == c, value, 0.0).astype(BF16)


def _group_reduce(x, gmat, split=False):
    n = gmat.shape[0]
    dot = _dot_xe if split else _bdot
    parts = [dot(x[:, s:s + n], gmat) for s in range(0, x.shape[1], n)]
    return parts[0] if len(parts) == 1 else jnp.concatenate(parts, axis=1)


def _tri_inverses(mats, n_steps):
    n = mats[0].shape[0]
    eye = (lax.broadcasted_iota(jnp.int32, (n, n), 0)
           == lax.broadcasted_iota(jnp.int32, (n, n), 1)).astype(F32)
    ps = [-a for a in mats]
    ts = [eye + p for p in ps]
    parts = [_split(p) for p in ps]
    for _ in range(n_steps):
        parts = [_split(_dot3_parts(s, s)) for s in parts]
        ts = [t + _dot3_parts(_split(t), s) for t, s in zip(ts, parts)]
    return ts


def _ada_kernel(c_ref, w_ref, b_ref, o_ref):
    ca = _silu(c_ref[...])
    o_ref[0, 0] = _bdot(ca, w_ref[0]) + b_ref[0]


def _ada(c_all, w_ada, b_ada):
    depth, d, n = w_ada.shape
    rows = c_all.shape[0]
    return pl.pallas_call(
        _ada_kernel,
        grid=(depth, n // d),
        in_specs=[pl.BlockSpec((rows, d), lambda l, j: (0, 0)),
                  pl.BlockSpec((1, d, d), lambda l, j: (l, 0, j)),
                  pl.BlockSpec((1, 1, d), lambda l, j: (l, 0, j))],
        out_specs=pl.BlockSpec((1, 1, rows, d), lambda l, j: (l, j, 0, 0)),
        out_shape=jax.ShapeDtypeStruct((depth, n // d, rows, d), F32),
        compiler_params=_params("parallel", "parallel"),
        name="ada",
    )(c_all, w_ada, b_ada.reshape(depth, 1, n))


def _tok_spec(tm, width):
    return pl.BlockSpec((1, tm, width), lambda b, i: (b, i, 0))


def _mod_spec(mod, tm):
    if mod.shape[1] == 1:
        return pl.BlockSpec((1, 1, mod.shape[2]), lambda b, i: (b, 0, 0))
    return pl.BlockSpec((1, tm, mod.shape[2]), lambda b, i: (b, i, 0))


def _const_spec(arr):
    nd = arr.ndim
    return pl.BlockSpec(arr.shape, lambda b, i: (0,) * nd)


def _prev8_spec(tm, width):
    blocks = tm // 8
    return pl.BlockSpec((1, 8, width), lambda b, i: (b, jnp.maximum(i * blocks - 1, 0), 0))


def _mlp_kernel(x_ref, g_ref, sc_ref, sh_ref, gt_ref, wu_ref, wd_ref, o_ref, h_ref, acc_ref):
    f = pl.program_id(2)

    @pl.when(f == 0)
    def _():
        h_ref[...] = _normmod(x_ref[0], g_ref[...], sc_ref[0], sh_ref[0]).astype(BF16)
        acc_ref[...] = jnp.zeros_like(acc_ref)

    u = jnp.maximum(jnp.dot(h_ref[...], wu_ref[0], preferred_element_type=F32), 0.0)
    acc_ref[...] += jnp.dot((u * u).astype(BF16), wd_ref[0], preferred_element_type=F32)

    @pl.when(f == pl.num_programs(2) - 1)
    def _():
        o_ref[0] = x_ref[0] + gt_ref[0] * acc_ref[...]


def _mlp(x, g, sc, sh, gt, w_up, w_down, layer, tm):
    bsz, t, d = x.shape
    ff = w_up.shape[2]
    tf = 1024

    def tok(b, i, f):
        return (b, i, 0)

    def mod_spec(m):
        if m.shape[1] == 1:
            return pl.BlockSpec((1, 1, d), lambda b, i, f: (b, 0, 0))
        return pl.BlockSpec((1, tm, d), tok)

    return pl.pallas_call(
        _mlp_kernel,
        grid=(bsz, t // tm, ff // tf),
        in_specs=[pl.BlockSpec((1, tm, d), tok),
                  pl.BlockSpec((1, d), lambda b, i, f: (0, 0)),
                  mod_spec(sc), mod_spec(sh), mod_spec(gt),
                  pl.BlockSpec((1, d, tf), lambda b, i, f: (layer, 0, f)),
                  pl.BlockSpec((1, tf, d), lambda b, i, f: (layer, f, 0))],
        out_specs=pl.BlockSpec((1, tm, d), tok),
        out_shape=jax.ShapeDtypeStruct(x.shape, F32),
        scratch_shapes=[pltpu.VMEM((tm, d), BF16), pltpu.VMEM((tm, d), F32)],
        compiler_params=_params("parallel", "parallel", "arbitrary"),
        name="mlp",
    )(x, g, sc, sh, gt, w_up, w_down)


def _qkv_kernel(x_ref, g_ref, sc_ref, sh_ref, w_ref, qg_ref, kg_ref, *rest):
    q_ref, k_ref, v_ref = rest[-3:]
    d = x_ref.shape[2]
    h = _normmod(x_ref[0], g_ref[...], sc_ref[0], sh_ref[0]).astype(BF16)
    gmat = _group_matrix(MXU_DIM, A_HEAD_DIM, 1.0 / A_HEAD_DIM)
    q_scale = A_HEAD_DIM ** -0.5 * LOG2E
    for s in range(0, d, MXU_DIM):
        sl = slice(s, s + MXU_DIM)
        q = jnp.dot(h, w_ref[0, :, s:s + MXU_DIM], preferred_element_type=F32)
        q_ref[0, :, sl] = q * lax.rsqrt(_bdot(q * q, gmat) + NORM_EPS) * qg_ref[...] * q_scale
        k = jnp.dot(h, w_ref[0, :, d + s:d + s + MXU_DIM], preferred_element_type=F32)
        k_ref[0, 0, :, sl] = k * lax.rsqrt(_bdot(k * k, gmat) + NORM_EPS) * kg_ref[...]
        v_ref[0, 0, :, sl] = jnp.dot(h, w_ref[0, :, 2 * d + s:2 * d + s + MXU_DIM],
                                     preferred_element_type=F32)


def _qkv(x, g, sc, sh, w_all, layer, q_gain, k_gain, kv_all, tm):
    bsz, t, d = x.shape
    n_layers = w_all.shape[0]
    qg = jnp.tile(q_gain, MXU_DIM // A_HEAD_DIM).reshape(1, MXU_DIM)
    kg = jnp.tile(k_gain, MXU_DIM // A_HEAD_DIM).reshape(1, MXU_DIM)
    kv_shape = jax.ShapeDtypeStruct((n_layers, bsz, t, d), F32)
    kv_spec = pl.BlockSpec((1, 1, tm, d), lambda b, i: (layer, b, i, 0))
    w_spec = pl.BlockSpec((1,) + w_all.shape[1:], lambda b, i: (layer, 0, 0))
    aliased = [] if kv_all is None else list(kv_all)
    n_in = 7
    return pl.pallas_call(
        _qkv_kernel,
        grid=(bsz, t // tm),
        in_specs=[_tok_spec(tm, d), _const_spec(g), _mod_spec(sc, tm), _mod_spec(sh, tm),
                  w_spec, _const_spec(qg), _const_spec(kg)]
                 + [pl.BlockSpec(memory_space=pl.ANY)] * len(aliased),
        out_specs=[_tok_spec(tm, d), kv_spec, kv_spec],
        out_shape=[jax.ShapeDtypeStruct(x.shape, F32), kv_shape, kv_shape],
        input_output_aliases={n_in + n: 1 + n for n in range(len(aliased))},
        compiler_params=_params("parallel", "parallel"),
        name="qkv",
    )(x, g, sc, sh, w_all, qg, kg, *aliased)


def _sb_prompt_kernel(bias_ref, q_ref, k_ref, v_ref, o_ref, z_ref, s_ref, acc_ref, c_ref, *, qb, kb):
    hp = pl.program_id(1)
    i = pl.program_id(2)
    q = q_ref[0]
    lane = lax.broadcasted_iota(jnp.int32, (1, LANES), 1)
    row = lax.broadcasted_iota(jnp.int32, (kb, kb), 0)
    col = lax.broadcasted_iota(jnp.int32, (kb, kb), 1)
    suffix = (row >= col).astype(BF16)
    suffix2 = jnp.concatenate([suffix, suffix], axis=0)
    qm = [jnp.where(lane // A_HEAD_DIM == e, q, 0.0).astype(BF16) for e in range(2)]
    bias = [bias_ref[2 * hp + e] * LOG2E for e in range(2)]
    per_q = qb // kb
    n_blocks = per_q * (i + 1)

    def rows(t):
        return pl.ds(pl.multiple_of(jnp.maximum(n_blocks - 1 - t, 0) * kb, kb), kb)

    def mask(t):
        if not isinstance(t, int) or t >= per_q:
            return None
        q_pos = lax.broadcasted_iota(jnp.int32, (qb, kb), 0)
        k_pos = lax.broadcasted_iota(jnp.int32, (qb, kb), 1) + (per_q - 1 - t) * kb
        return k_pos < q_pos

    def logits(t, slot, e):
        z_ref[slot, e] = _bdot_nt(qm[e], k_ref[0, 0, rows(t), :]) + bias[e]

    def suffix_sums(t, slot, e):
        s_ref[slot, e] = _sb_suffix(z_ref[slot, e], suffix2, mask(t))

    def consume(t, slot, e):
        s_incl = s_ref[slot, e]
        p = _bdot(_sb_local(z_ref[slot, e], s_incl, mask(t)), v_ref[0, 0, rows(t), :])
        acc_ref[e] += jnp.exp2(-c_ref[e]) * p
        c_ref[e] += s_incl[:, 0:1]

    def trip(t, slot):
        for e in range(2):
            consume(t, slot, e)
            logits(t + 2, slot, e)
            suffix_sums(t + 1, 1 - slot, e)

    acc_ref[...] = jnp.zeros_like(acc_ref)
    c_ref[...] = jnp.zeros_like(c_ref)
    for e in range(2):
        logits(0, 0, e)
        logits(1, 1, e)
        suffix_sums(0, 0, e)
    for t in range(per_q):
        trip(t, t % 2)

    def pair(u, _):
        trip(2 * u + per_q, 0)
        trip(2 * u + per_q + 1, 1)
        return 0

    lax.fori_loop(0, (per_q // 2) * i, pair, 0)
    o_ref[0] = jnp.where(lane < A_HEAD_DIM, acc_ref[0], acc_ref[1])


def _sb_prompt(q, k_all, v_all, layer, logit_bias, qb, kb):
    bsz, t, d = q.shape
    n_pairs = d // LANES
    assert qb % (2 * kb) == 0 and t % qb == 0
    kv_spec = pl.BlockSpec((1, 1, t, LANES), lambda b, p, i, bias: (layer, b, 0, p))
    grid_spec = pltpu.PrefetchScalarGridSpec(
        num_scalar_prefetch=1,
        grid=(bsz, n_pairs, t // qb),
        in_specs=[pl.BlockSpec((1, qb, LANES), lambda b, p, i, bias: (b, i, p)), kv_spec, kv_spec],
        out_specs=pl.BlockSpec((1, qb, LANES), lambda b, p, i, bias: (b, i, p)),
        scratch_shapes=[pltpu.VMEM((2, 2, qb, kb), F32), pltpu.VMEM((2, 2, qb, kb), F32),
                        pltpu.VMEM((2, qb, LANES), F32), pltpu.VMEM((2, qb, 1), F32)],
    )
    return pl.pallas_call(
        functools.partial(_sb_prompt_kernel, qb=qb, kb=kb),
        grid_spec=grid_spec,
        out_shape=jax.ShapeDtypeStruct(q.shape, F32),
        compiler_params=_params("parallel", "parallel", "arbitrary"),
        name="sb_prompt",
    )(logit_bias, q, k_all, v_all)


def _sb_decode_kernel(pt_ref, q_ref, bias_ref, kn_ref, vn_ref, *rest, pages_per_step, page, t_new):
    k_refs = rest[:pages_per_step]
    v_refs = rest[pages_per_step:2 * pages_per_step]
    o_ref = rest[2 * pages_per_step]
    acc_ref, c_ref, qbd_ref = rest[2 * pages_per_step + 1:]
    s = pl.program_id(1)
    d = q_ref.shape[2]
    n_heads = d // A_HEAD_DIM
    n_rows = n_heads * t_new
    r_head = lax.broadcasted_iota(jnp.int32, (n_rows, d), 0) // t_new
    c_head = lax.broadcasted_iota(jnp.int32, (n_rows, d), 1) // A_HEAD_DIM

    @pl.when(s == 0)
    def _():
        tiled = jnp.concatenate([q_ref[0]] * n_heads, axis=0)
        qbd_ref[...] = jnp.where(r_head == c_head, tiled, 0.0).astype(BF16)

    qbd = qbd_ref[...]
    bias = bias_ref[...]
    row = lax.broadcasted_iota(jnp.int32, (page, page), 0)
    col = lax.broadcasted_iota(jnp.int32, (page, page), 1)
    suffix = (row >= col).astype(BF16)
    suffix2 = jnp.concatenate([suffix, suffix], axis=0)

    def accumulate(p, tot):
        acc_ref[...] += jnp.exp2(-c_ref[...]) * p
        c_ref[...] += tot

    @pl.when(s == 0)
    def _():
        acc_ref[...] = jnp.zeros_like(acc_ref)
        c_ref[...] = jnp.zeros_like(c_ref)
        pad = jnp.zeros((page - t_new, d), F32)
        k_pad = jnp.concatenate([kn_ref[0, 0], pad], axis=0).astype(BF16)
        v_pad = jnp.concatenate([vn_ref[0, 0], pad], axis=0).astype(BF16)
        q_t = lax.broadcasted_iota(jnp.int32, (n_rows, page), 0) % t_new
        key_i = lax.broadcasted_iota(jnp.int32, (n_rows, page), 1)
        w, tot = _sb_weights(_bdot_nt(qbd, k_pad) + bias, suffix2, key_i < q_t)
        accumulate(jnp.dot(w, v_pad, preferred_element_type=F32), tot)

    zs = [jnp.dot(qbd, k_refs[r][0, 0].astype(BF16), preferred_element_type=F32) + bias
          for r in range(pages_per_step)]
    ws = [_sb_weights(z, suffix2, None) for z in zs]
    ps = [_bdot_nt(w, v_refs[r][0, 0]) for r, (w, _) in enumerate(ws)]
    for p, (_, tot) in zip(ps, ws):
        accumulate(p, tot)

    @pl.when(s == pl.num_programs(1) - 1)
    def _():
        diag = jnp.where(r_head == c_head, acc_ref[...], 0.0)
        out = diag[0:t_new]
        for h in range(1, n_heads):
            out = out + diag[h * t_new:(h + 1) * t_new]
        o_ref[0] = out


def _sb_decode(q, k_all, v_all, new_layer, cache_k, cache_v, layer, page_table, logit_bias,
               pages_per_step):
    bsz, t_new, d = q.shape
    n_heads = d // A_HEAD_DIM
    n_layers, n_pool, page = cache_k.shape[:3]
    n_pages = page_table.shape[1]
    n_rows = n_heads * t_new
    ck = cache_k.transpose(0, 1, 3, 4, 2).reshape(n_layers, n_pool, d, page)
    cv = cache_v.transpose(0, 1, 3, 4, 2).reshape(n_layers, n_pool, d, page)
    bias_rows = jnp.broadcast_to(jnp.repeat(logit_bias * LOG2E, t_new)[:, None], (n_rows, page))
    n_steps = n_pages // pages_per_step

    def page_map(r):
        def index(b, s, pt):
            return (layer, pt[b, n_pages - 1 - (s * pages_per_step + r)], 0, 0)
        return index

    kv_specs = [pl.BlockSpec((1, 1, d, page), page_map(r)) for r in range(pages_per_step)]
    grid_spec = pltpu.PrefetchScalarGridSpec(
        num_scalar_prefetch=1,
        grid=(bsz, n_steps),
        in_specs=[pl.BlockSpec((1, t_new, d), lambda b, s, pt: (b, 0, 0)),
                  pl.BlockSpec((n_rows, page), lambda b, s, pt: (0, 0)),
                  pl.BlockSpec((1, 1, t_new, d), lambda b, s, pt: (new_layer, b, 0, 0)),
                  pl.BlockSpec((1, 1, t_new, d), lambda b, s, pt: (new_layer, b, 0, 0))]
                 + kv_specs + kv_specs,
        out_specs=pl.BlockSpec((1, t_new, d), lambda b, s, pt: (b, 0, 0)),
        scratch_shapes=[pltpu.VMEM((n_rows, d), F32), pltpu.VMEM((n_rows, 1), F32),
                        pltpu.VMEM((n_rows, d), BF16)],
    )
    return pl.pallas_call(
        functools.partial(_sb_decode_kernel, pages_per_step=pages_per_step, page=page, t_new=t_new),
        grid_spec=grid_spec,
        out_shape=jax.ShapeDtypeStruct(q.shape, F32),
        compiler_params=_params("parallel", "arbitrary"),
        name="sb_decode",
    )(page_table, q, bias_rows, k_all, v_all, *([ck] * pages_per_step), *([cv] * pages_per_step))


def _proj_res_kernel(a_ref, w_ref, x_ref, gt_ref, o_ref):
    o_ref[0] = x_ref[0] + gt_ref[0] * _bdot(a_ref[0], w_ref[...])


def _proj_res(a, w, x, gt, tm):
    bsz, t, d = x.shape
    return pl.pallas_call(
        _proj_res_kernel,
        grid=(bsz, t // tm),
        in_specs=[_tok_spec(tm, d), _const_spec(w), _tok_spec(tm, d), _mod_spec(gt, tm)],
        out_specs=_tok_spec(tm, d),
        out_shape=jax.ShapeDtypeStruct(x.shape, F32),
        compiler_params=_params("parallel", "parallel"),
        name="proj_res",
    )(a, w, x, gt)


def _gdn_out_kernel(o_ref_in, z_ref, on_ref, w_ref, x_ref, gt_ref, o_ref):
    o = o_ref_in[0]
    gmat = _group_matrix(MXU_DIM, B_HEAD_DIM, 1.0 / B_HEAD_DIM)
    ms = _group_reduce(o * o, gmat)
    a = o * lax.rsqrt(ms + NORM_EPS) * on_ref[...] * _silu(z_ref[0])
    o_ref[0] = x_ref[0] + gt_ref[0] * _bdot(a, w_ref[...])


def _gdn_out(o, z, o_norm, w, x, gt, tm):
    bsz, t, d = x.shape
    on = jnp.tile(o_norm, d // B_HEAD_DIM).reshape(1, d)
    return pl.pallas_call(
        _gdn_out_kernel,
        grid=(bsz, t // tm),
        in_specs=[_tok_spec(tm, d), _tok_spec(tm, d), _const_spec(on), _const_spec(w),
                  _tok_spec(tm, d), _mod_spec(gt, tm)],
        out_specs=_tok_spec(tm, d),
        out_shape=jax.ShapeDtypeStruct(x.shape, F32),
        compiler_params=_params("parallel", "parallel"),
        name="gdn_out",
    )(o, z, on, w, x, gt)


def _rwkv_out_kernel(y_ref, r_ref, k_ref, v_ref, gate_ref, rk_ref, lg_ref, lb_ref, w_ref,
                     x_ref, gt_ref, o_ref):
    y = y_ref[0]
    mean_mat = _group_matrix(MXU_DIM, C_HEAD_DIM, 1.0 / C_HEAD_DIM)
    sum_mat = _group_matrix(MXU_DIM, C_HEAD_DIM, 1.0)
    yc = y - _group_reduce(y, mean_mat, split=True)
    var = _group_reduce(yc * yc, mean_mat)
    yn = yc * lax.rsqrt(var + GN_EPS) * lg_ref[...] + lb_ref[...]
    bonus = _group_reduce(r_ref[0] * k_ref[0] * rk_ref[...], sum_mat) * v_ref[0]
    a = (yn + bonus) * gate_ref[0]
    o_ref[0] = x_ref[0] + gt_ref[0] * _bdot(a, w_ref[...])


def _rwkv_out(y, r, k, v, gate, r_k, ln_g, ln_b, w, x, gt, tm):
    bsz, t, d = x.shape
    rk = r_k.reshape(1, d)
    lg = ln_g.reshape(1, d)
    lb = ln_b.reshape(1, d)
    tok = _tok_spec(tm, d)
    return pl.pallas_call(
        _rwkv_out_kernel,
        grid=(bsz, t // tm),
        in_specs=[tok, tok, tok, tok, tok, _const_spec(rk), _const_spec(lg), _const_spec(lb),
                  _const_spec(w), tok, _mod_spec(gt, tm)],
        out_specs=tok,
        out_shape=jax.ShapeDtypeStruct(x.shape, F32),
        compiler_params=_params("parallel", "parallel"),
        name="rwkv_out",
    )(y, r, k, v, gate, rk, lg, lb, w, x, gt)


def _gdn_in_kernel(x_ref, g_ref, sc_ref, sh_ref, w_ref, xqkv_ref, z_ref, ba_ref):
    n_conv = xqkv_ref.shape[2]
    n_z = z_ref.shape[2]
    h = _normmod(x_ref[0], g_ref[...], sc_ref[0], sh_ref[0]).astype(BF16)
    for s in range(0, n_conv, MXU_DIM):
        xqkv_ref[0, :, s:s + MXU_DIM] = jnp.dot(h, w_ref[:, s:s + MXU_DIM],
                                                preferred_element_type=F32)
    for s in range(0, n_z, MXU_DIM):
        z_ref[0, :, s:s + MXU_DIM] = jnp.dot(h, w_ref[:, n_conv + s:n_conv + s + MXU_DIM],
                                             preferred_element_type=F32)
    ba_ref[0] = jnp.dot(h, w_ref[:, n_conv + n_z:], preferred_element_type=F32)


def _gdn_in(x, g, sc, sh, w_pad, n_conv, tm):
    bsz, t, d = x.shape
    return pl.pallas_call(
        _gdn_in_kernel,
        grid=(bsz, t // tm),
        in_specs=[_tok_spec(tm, d), _const_spec(g), _mod_spec(sc, tm), _mod_spec(sh, tm),
                  _const_spec(w_pad)],
        out_specs=[_tok_spec(tm, n_conv), _tok_spec(tm, d), _tok_spec(tm, LANES)],
        out_shape=[jax.ShapeDtypeStruct((bsz, t, n_conv), F32),
                   jax.ShapeDtypeStruct((bsz, t, d), F32),
                   jax.ShapeDtypeStruct((bsz, t, LANES), F32)],
        compiler_params=_params("parallel", "parallel"),
        name="gdn_in",
    )(x, g, sc, sh, w_pad)


def _gdn_conv_kernel(*refs, pre_shifted):
    if pre_shifted:
        x_ref, xs_ref, cw_ref, ba_ref, alog_ref, dtb_ref = refs[:6]
    else:
        x_ref, p8_ref, cw_ref, ba_ref, alog_ref, dtb_ref = refs[:6]
    q_ref, k_ref, kb_ref, vb_ref, g_ref = refs[6:]
    i = pl.program_id(1)
    tm = x_ref.shape[1]
    hd = q_ref.shape[2]
    n_heads = hd // B_HEAD_DIM
    x = x_ref[0]
    conv = x * cw_ref[B_CONV - 1:B_CONV, :]
    if pre_shifted:
        for s in range(1, B_CONV):
            conv = conv + xs_ref[s - 1, 0] * cw_ref[B_CONV - 1 - s:B_CONV - s, :]
    else:
        full = jnp.concatenate([p8_ref[0], x], axis=0)
        t_glob = i * tm + lax.broadcasted_iota(jnp.int32, (tm, 1), 0)
        for s in range(1, B_CONV):
            xs = pltpu.roll(full, s, 0)[8:]
            xs = jnp.where(t_glob >= s, xs, 0.0)
            conv = conv + xs * cw_ref[B_CONV - 1 - s:B_CONV - s, :]
    act = _silu(conv)
    r = lax.broadcasted_iota(jnp.int32, (LANES, hd), 0)
    c_head = lax.broadcasted_iota(jnp.int32, (LANES, hd), 1) // B_HEAD_DIM
    e_b = (r == c_head).astype(BF16)
    e_a = (r == c_head + n_heads).astype(BF16)
    ba = ba_ref[0]
    beta = _sigmoid(_dot_xe(ba, e_b))
    g = -jnp.exp(alog_ref[...]) * _softplus(_dot_xe(ba, e_a) + dtb_ref[...])
    g_ref[0] = g
    ones = _group_matrix(MXU_DIM, B_HEAD_DIM, 1.0)
    aq = act[:, :hd]
    ak = act[:, hd:2 * hd]
    q_ref[0] = aq * lax.rsqrt(_group_reduce(aq * aq, ones) + L2_EPS) * (B_HEAD_DIM ** -0.5)
    k = ak * lax.rsqrt(_group_reduce(ak * ak, ones) + L2_EPS)
    k_ref[0] = k
    kb_ref[0] = k * beta
    vb_ref[0] = act[:, 2 * hd:] * beta


def _gdn_conv(xqkv, xs, conv_w, ba, a_log, dt_bias, hd, tm):
    bsz, t, n_conv = xqkv.shape
    alog = jnp.repeat(a_log, B_HEAD_DIM).reshape(1, hd)
    dtb = jnp.repeat(dt_bias, B_HEAD_DIM).reshape(1, hd)
    pre_shifted = xs is not None
    if pre_shifted:
        second = xs
        second_spec = pl.BlockSpec((B_CONV - 1, 1, tm, n_conv), lambda b, i: (0, b, i, 0))
    else:
        second = xqkv
        second_spec = _prev8_spec(tm, n_conv)
    shp = jax.ShapeDtypeStruct((bsz, t, hd), F32)
    return pl.pallas_call(
        functools.partial(_gdn_conv_kernel, pre_shifted=pre_shifted),
        grid=(bsz, t // tm),
        in_specs=[_tok_spec(tm, n_conv), second_spec, _const_spec(conv_w), _tok_spec(tm, LANES),
                  _const_spec(alog), _const_spec(dtb)],
        out_specs=[_tok_spec(tm, hd)] * 5,
        out_shape=[shp] * 5,
        compiler_params=_params("parallel", "parallel"),
        name="gdn_conv",
    )(xqkv, second, conv_w, ba, alog, dtb)


def _gdn_chunk_kernel(q_ref, k_ref, kb_ref, vb_ref, g_ref, s0_ref, o_ref, s_out_ref, s_ref, *,
                      n_heads, n_chunks):
    i = pl.program_id(2)
    c = CHUNK
    hw = B_HEAD_DIM

    @pl.when(i == 0)
    def _():
        s_ref[...] = s0_ref[0]

    row = lax.broadcasted_iota(jnp.int32, (c, c), 0)
    col = lax.broadcasted_iota(jnp.int32, (c, c), 1)
    lower = row >= col
    strict = row > col
    lower_b = lower.astype(BF16)
    ones_avg = jnp.full((c, LANES), 1.0 / LANES, BF16)

    def row_layout(g_cb):
        gh, gl = _split(g_cb)
        nt = lambda x: lax.dot_general(ones_avg, x, (((1,), (1,)), ((), ())),
                                       preferred_element_type=F32)
        return nt(gh) + nt(gl)

    sls = [(slice(n * c, (n + 1) * c), slice(h * hw, (h + 1) * hw))
           for h in range(n_heads) for n in range(n_chunks)]
    rng = range(len(sls))
    q = [q_ref[0, rs, ls] for rs, ls in sls]
    k = [k_ref[0, rs, ls] for rs, ls in sls]
    kb = [kb_ref[0, rs, ls] for rs, ls in sls]
    vb = [vb_ref[0, rs, ls] for rs, ls in sls]
    g_cb = [_dot_ex(lower_b, g_ref[0, rs, ls]) for rs, ls in sls]
    g_row = [row_layout(x) for x in g_cb]
    decay = [jnp.where(lower, jnp.exp(jnp.where(lower, g_cb[n][:, :c] - g_row[n], 0.0)), 0.0)
             for n in rng]
    a = [jnp.where(strict, decay[n] * _bdot_nt(kb[n], k[n]), 0.0) for n in rng]
    t_inv = _tri_inverses(a, 5)
    e_g = [jnp.exp(x) for x in g_cb]
    t_parts = [_split(x) for x in t_inv]
    w_mat = [_dot3_parts(t_parts[n], _split(kb[n] * e_g[n])) for n in rng]
    u0 = [_dot3_parts(t_parts[n], _split(vb[n])) for n in rng]
    qk = [decay[n] * _bdot_nt(q[n], k[n]) for n in rng]
    g_last = [x[c - 1:c, :] for x in g_cb]
    k_end = [jnp.exp(g_last[n] - g_cb[n]) * k[n] for n in rng]
    q_eff = [q[n] * e_g[n] - _bdot(qk[n], w_mat[n]) for n in rng]
    o0 = [_bdot(qk[n], u0[n]) for n in rng]
    m0 = [_bdot_tn(k_end[n], w_mat[n]) for n in rng]
    n0 = [_bdot_tn(k_end[n], u0[n]) for n in rng]

    for h in range(n_heads):
        s = s_ref[h]
        for n in range(h * n_chunks, (h + 1) * n_chunks):
            rs, ls = sls[n]
            o_ref[0, rs, ls] = _bdot(q_eff[n], s) + o0[n]
            s = jnp.exp(g_last[n]) * s - _bdot(m0[n], s) + n0[n]
        s_ref[h] = s
        s_out_ref[0, h] = s


def _gdn_chunk(q, k, kb, vb, g, s0, ct, heads_per_step):
    bsz, t, hd = q.shape
    hw = B_HEAD_DIM
    blk = pl.BlockSpec((1, ct, heads_per_step * hw), lambda b, h, i: (b, i, h))
    st = pl.BlockSpec((1, heads_per_step, hw, hw), lambda b, h, i: (b, h, 0, 0))
    return pl.pallas_call(
        functools.partial(_gdn_chunk_kernel, n_heads=heads_per_step, n_chunks=ct // CHUNK),
        grid=(bsz, hd // (heads_per_step * hw), t // ct),
        in_specs=[blk] * 5 + [st],
        out_specs=[blk, st],
        out_shape=[jax.ShapeDtypeStruct(q.shape, F32), jax.ShapeDtypeStruct(s0.shape, F32)],
        scratch_shapes=[pltpu.VMEM((heads_per_step, hw, hw), F32)],
        compiler_params=_params("parallel", "parallel", "arbitrary"),
        name="gdn_chunk",
    )(q, k, kb, vb, g, s0)


def _rwkv_in_kernel(*refs, seq_len, per_row_shift):
    (x_ref, p8_ref, g_ref, sc_ref, sh_ref, sh0_ref, mu_ref, wrkv_ref, w1_ref, w2_ref, a1_ref, a2_ref,
     g1_ref, g2_ref, w0_ref, a0_ref, kk_ref, ka_ref) = refs[:18]
    r_ref, lw_ref, k_ref, v_ref, kn_ref, bb_ref, gate_ref, hl_ref = refs[18:]
    i = pl.program_id(1)
    tm = x_ref.shape[1]
    g = g_ref[...]
    h = _normmod(x_ref[0], g, sc_ref[0], sh_ref[0])
    if per_row_shift:
        sc8, sh8 = sc_ref[0, 0:8], sh_ref[0, 0:8]
    else:
        sc8, sh8 = sc_ref[0], sh_ref[0]
    h8 = _normmod(p8_ref[0], g, sc8, sh8)
    prev = pltpu.roll(jnp.concatenate([h8, h], axis=0), 1, 0)[8:]
    t_glob = i * tm + lax.broadcasted_iota(jnp.int32, (tm, 1), 0)
    prev = jnp.where(t_glob % seq_len == 0, sh0_ref[0], prev)
    xx = prev - h
    mix = lambda n: (h + xx * mu_ref[n:n + 1, :]).astype(BF16)
    r = jnp.dot(mix(0), wrkv_ref[0], preferred_element_type=F32)
    k = jnp.dot(mix(1), wrkv_ref[1], preferred_element_type=F32)
    v_ref[0] = jnp.dot(mix(2), wrkv_ref[2], preferred_element_type=F32)
    r_ref[0] = r
    wl = w0_ref[...] + _bdot(jnp.tanh(jnp.dot(mix(3), w1_ref[...], preferred_element_type=F32)),
                             w2_ref[...])
    lw_ref[0] = -jnp.exp(-_softplus(-wl) - 0.5)
    a = _sigmoid(a0_ref[...] + _bdot(jnp.dot(mix(4), a1_ref[...], preferred_element_type=F32),
                                     a2_ref[...]))
    gate_ref[0] = _bdot(_sigmoid(jnp.dot(mix(5), g1_ref[...], preferred_element_type=F32)),
                        g2_ref[...])
    kx = k * kk_ref[...]
    ones = _group_matrix(MXU_DIM, C_HEAD_DIM, 1.0)
    kn = kx * lax.rsqrt(_group_reduce(kx * kx, ones) + L2_EPS)
    kn_ref[0] = kn
    bb_ref[0] = kn * a
    k_ref[0] = k * (1.0 + (a - 1.0) * ka_ref[...])
    hl_ref[0, 0] = h[tm - 8:tm]


def _rwkv_in(x, g, sc, sh, shift0, seq_len, P, tm):
    bsz, t, d = x.shape
    per_row = shift0.shape[1] != 1
    consts = [P['mu'], P['w_rkv'], P['w1'], P['w2'], P['a1'], P['a2'], P['g1'], P['g2'],
              P['w0'], P['a0'], P['k_k'], P['k_a']]
    shp = jax.ShapeDtypeStruct(x.shape, F32)
    n_t = t // tm
    outs = pl.pallas_call(
        functools.partial(_rwkv_in_kernel, seq_len=seq_len, per_row_shift=per_row),
        grid=(bsz, n_t),
        in_specs=[_tok_spec(tm, d), _prev8_spec(tm, d), _const_spec(g), _mod_spec(sc, tm),
                  _mod_spec(sh, tm), _mod_spec(shift0, tm)] + [_const_spec(c) for c in consts],
        out_specs=[_tok_spec(tm, d)] * 7 + [pl.BlockSpec((1, 1, 8, d), lambda b, i: (b, i, 0, 0))],
        out_shape=[shp] * 7 + [jax.ShapeDtypeStruct((bsz, n_t, 8, d), F32)],
        compiler_params=_params("parallel", "parallel"),
        name="rwkv_in",
    )(x, x, g, sc, sh, shift0, *consts)
    return outs


def _rwkv_scan_kernel(r_ref, lw_ref, k_ref, v_ref, kn_ref, bb_ref, s0_ref, y_ref, s_out_ref, s_ref,
                      *, n_pairs, n_chunks):
    i = pl.program_id(2)
    c = CHUNK
    n = 2 * c

    @pl.when(i == 0)
    def _():
        s_ref[...] = s0_ref[0]

    lane = lax.broadcasted_iota(jnp.int32, (1, LANES), 1)
    row = lax.broadcasted_iota(jnp.int32, (n, n), 0)
    col = lax.broadcasted_iota(jnp.int32, (n, n), 1)
    same = (row // c) == (col // c)
    incl = same & (row >= col)
    strict = same & (row > col)
    incl_b = incl.astype(BF16)

    def stack(x):
        return jnp.concatenate([jnp.where(lane < C_HEAD_DIM, x, 0.0),
                                jnp.where(lane >= C_HEAD_DIM, x, 0.0)], axis=0)

    sls = [(slice(m * c, (m + 1) * c), slice(p * LANES, (p + 1) * LANES))
           for p in range(n_pairs) for m in range(n_chunks)]
    rng = range(len(sls))
    lw = [stack(lw_ref[0, rs, ls]) for rs, ls in sls]
    lc = [_dot_ex(incl_b, x) for x in lw]
    lc_last = [x[c - 1:c, :] + x[n - 1:n, :] for x in lc]
    p_inv = [jnp.exp(-x) for x in lc]
    p_end = [jnp.exp(lc_last[m] - lc[m]) for m in rng]
    kn_t = [stack(kn_ref[0, sls[m][0], sls[m][1]]) * jnp.exp(lc[m] - lw[m]) for m in rng]
    k_s = [stack(k_ref[0, rs, ls]) for rs, ls in sls]
    b_s = [stack(bb_ref[0, rs, ls]) for rs, ls in sls]
    v_s = [stack(v_ref[0, rs, ls]) for rs, ls in sls]
    k_t = [k_s[m] * p_inv[m] for m in rng]
    b_t = [b_s[m] * p_inv[m] for m in rng]
    r_t = [stack(r_ref[0, sls[m][0], sls[m][1]]) * jnp.exp(lc[m]) for m in rng]
    a_kb = [jnp.where(strict, _bdot_nt(kn_t[m], b_t[m]), 0.0) for m in rng]
    a_kk = [jnp.where(strict, _bdot_nt(kn_t[m], k_t[m]), 0.0) for m in rng]
    a_rk = [jnp.where(incl, _bdot_nt(r_t[m], k_t[m]), 0.0) for m in rng]
    a_rb = [jnp.where(incl, _bdot_nt(r_t[m], b_t[m]), 0.0) for m in rng]
    t_inv = _tri_inverses(a_kb, 5)
    t_parts = [_split(x) for x in t_inv]
    kt = [_dot3_parts(t_parts[m], _split(kn_t[m])) for m in rng]
    u0 = [_dot3_parts(t_parts[m], _split(_bdot(a_kk[m], v_s[m]))) for m in rng]
    k_e = [k_s[m] * p_end[m] for m in rng]
    b_e = [b_s[m] * p_end[m] for m in rng]
    g_mat = [_bdot_tn(kt[m], b_e[m]) for m in rng]
    n0 = [_bdot_tn(v_s[m], k_e[m]) - _bdot_tn(u0[m], b_e[m]) for m in rng]
    r_eff = [r_t[m] - _bdot(a_rb[m], kt[m]) for m in rng]
    y0 = [_bdot(a_rk[m], v_s[m]) - _bdot(a_rb[m], u0[m]) for m in rng]

    for p in range(n_pairs):
        s = s_ref[p]
        for m in range(p * n_chunks, (p + 1) * n_chunks):
            rs, ls = sls[m]
            y = _bdot_nt(r_eff[m], s) + y0[m]
            y_ref[0, rs, ls] = y[:c] + y[c:]
            s = s * jnp.exp(lc_last[m]) - _bdot(s, g_mat[m]) + n0[m]
        s_ref[p] = s
        s_out_ref[0, p] = s


def _rwkv_scan(r, lw, k, v, kn, bb, s0_pairs, ct, pairs_per_step):
    bsz, t, d = r.shape
    blk = pl.BlockSpec((1, ct, pairs_per_step * LANES), lambda b, p, i: (b, i, p))
    st = pl.BlockSpec((1, pairs_per_step, LANES, LANES), lambda b, p, i: (b, p, 0, 0))
    return pl.pallas_call(
        functools.partial(_rwkv_scan_kernel, n_pairs=pairs_per_step, n_chunks=ct // CHUNK),
        grid=(bsz, d // (pairs_per_step * LANES), t // ct),
        in_specs=[blk] * 6 + [st],
        out_specs=[blk, st],
        out_shape=[jax.ShapeDtypeStruct(r.shape, F32), jax.ShapeDtypeStruct(s0_pairs.shape, F32)],
        scratch_shapes=[pltpu.VMEM((pairs_per_step, LANES, LANES), F32)],
        compiler_params=_params("parallel", "parallel", "arbitrary"),
        name="rwkv_scan",
    )(r, lw, k, v, kn, bb, s0_pairs)


def _pad_time(a, t_pad):
    return jnp.pad(a, ((0, 0), (0, t_pad - a.shape[1]), (0, 0)))


def _pairs_from_heads(s):
    bsz, n_h, n, _ = s.shape
    sp = s.reshape(bsz, n_h // 2, 2, n, n)
    z = jnp.zeros_like(sp[:, :, 0])
    top = jnp.concatenate([sp[:, :, 0], z], axis=-1)
    bot = jnp.concatenate([z, sp[:, :, 1]], axis=-1)
    return jnp.concatenate([top, bot], axis=-2)


def _heads_from_pairs(sp):
    bsz, n_p, n2, _ = sp.shape
    n = n2 // 2
    return jnp.stack([sp[:, :, :n, :n], sp[:, :, n:, n:]], axis=2).reshape(bsz, 2 * n_p, n, n)


def _run_trunk(x, mods, W, tm, seq_len, cache, state):
    bsz, t, d = x.shape
    n_seq = bsz * t // seq_len
    depth = len(mods)
    kv_all, bconv, bssm, cshift, cwkv = None, [], [], [], []
    n_a = W['a_w_qkv'].shape[0]
    for l in range(depth):
        kind, j = l % 3, l // 3
        sh1, sc1, gt1, sh2, sc2, gt2 = mods[l]
        g_mix = W['ln_mix'][l].reshape(1, d)
        if kind == 0:
            q, k_all, v_all = _qkv(x, g_mix, sc1, sh1, W['a_w_qkv'], j, W['a_q_norm'][j],
                                   W['a_k_norm'][j], kv_all, tm)
            kv_all = (k_all, v_all)
            if cache is None:
                o = _sb_prompt(q, k_all, v_all, j, W['a_logit_bias'][j],
                               qb=SB_QUERY_BLOCK, kb=SB_KEY_BLOCK)
            else:
                cache_k, cache_v, page_table = cache
                o = _sb_decode(q.reshape(n_seq, seq_len, d), k_all.reshape(n_a, n_seq, seq_len, d),
                               v_all.reshape(n_a, n_seq, seq_len, d), j, cache_k, cache_v, j,
                               page_table, W['a_logit_bias'][j],
                               pages_per_step=min(8, page_table.shape[1])).reshape(bsz, t, d)
            x = _proj_res(o, W['a_w_o'][j], x, gt1, tm)
        elif kind == 1:
            n_conv = 3 * d
            xqkv, z, ba = _gdn_in(x, g_mix, sc1, sh1, W['b_w_in'][j], n_conv, tm)
            xq_seq = xqkv.reshape(n_seq, seq_len, n_conv)
            if state is None:
                xs = None
                bconv.append(xq_seq[:, seq_len - (B_CONV - 1):])
                s0 = jnp.zeros((n_seq, d // B_HEAD_DIM, B_HEAD_DIM, B_HEAD_DIM), F32)
            else:
                xc = jnp.concatenate([state['b_conv'][j], xq_seq], axis=1)
                xs = jnp.stack([xc[:, B_CONV - 1 - s:B_CONV - 1 - s + seq_len].reshape(bsz, t, n_conv)
                                for s in range(1, B_CONV)])
                bconv.append(xc[:, seq_len:])
                s0 = state['b_ssm'][j]
            q, k, kb, vb, g = _gdn_conv(xqkv, xs, W['b_conv'][j], ba, W['b_a_log'][j],
                                        W['b_dt_bias'][j], d, min(tm, 256))
            t_pad = -(-seq_len // CHUNK) * CHUNK
            seqs = [_pad_time(a.reshape(n_seq, seq_len, d), t_pad) for a in (q, k, kb, vb, g)]
            ct = min(t_pad, CHAINS_PER_STEP * CHUNK)
            heads = max(1, min(d // B_HEAD_DIM, CHAINS_PER_STEP * CHUNK // ct))
            o, s_new = _gdn_chunk(*seqs, s0, ct=ct, heads_per_step=heads)
            bssm.append(s_new)
            o = o[:, :seq_len].reshape(bsz, t, d)
            x = _gdn_out(o, z, W['b_o_norm'][j], W['b_w_o'][j], x, gt1, tm)
        else:
            if state is None:
                shift0 = jnp.zeros((bsz, 1, d), F32)
                s0 = jnp.zeros((n_seq, d // C_HEAD_DIM, C_HEAD_DIM, C_HEAD_DIM), F32)
            else:
                shift0 = jnp.repeat(state['c_shift'][j], seq_len, axis=0).reshape(bsz, t, d)
                s0 = state['c_wkv'][j]
            tm_c = min(tm, 256)
            r, lw, k, v, kn, bb, gate, h_last = _rwkv_in(x, g_mix, sc1, sh1, shift0, seq_len,
                                                        W['c'][j], tm_c)
            if state is None:
                cshift.append(h_last[:, -1, 7])
            else:
                assert t == tm_c and seq_len == 8
                cshift.append(_last_rows(x, g_mix, sc1, sh1, seq_len))
            t_pad = -(-seq_len // CHUNK) * CHUNK
            seqs = [_pad_time(a.reshape(n_seq, seq_len, d), t_pad) for a in (r, lw, k, v, kn, bb)]
            ct = min(t_pad, CHAINS_PER_STEP * CHUNK)
            pairs = max(1, min(d // LANES, CHAINS_PER_STEP * CHUNK // ct))
            y, s_new = _rwkv_scan(*seqs, _pairs_from_heads(s0), ct=ct, pairs_per_step=pairs)
            cwkv.append(_heads_from_pairs(s_new))
            y = y[:, :seq_len].reshape(bsz, t, d)
            x = _rwkv_out(y, r, k, v, gate, W['c_r_k'][j], W['c_ln_g'][j], W['c_ln_b'][j],
                          W['c_w_o'][j], x, gt1, tm)
        x = _mlp(x, W['ln_mlp'][l].reshape(1, d), sc2, sh2, gt2, W['w_up'], W['w_down'], l,
                 MLP_TOKEN_TILE if t % MLP_TOKEN_TILE == 0 else tm)
    heads = (n_a, n_seq, seq_len, d // A_HEAD_DIM, A_HEAD_DIM)
    return x, kv_all[0].reshape(heads), kv_all[1].reshape(heads), bconv, bssm, cshift, cwkv


def _hmod_kernel(x_ref, g_ref, sc_ref, sh_ref, o_ref):
    o_ref[0] = _normmod(x_ref[0], g_ref[...], sc_ref[0], sh_ref[0])


def _last_rows(x, g, sc, sh, seq_len):
    bsz, t, d = x.shape
    n_seq = bsz * t // seq_len
    pick = lambda a: a.reshape(n_seq, seq_len, d)[:, seq_len - 1].reshape(1, n_seq, d)
    xs, scs, shs = pick(x), pick(sc), pick(sh)
    return pl.pallas_call(
        _hmod_kernel,
        grid=(1, 1),
        in_specs=[_tok_spec(n_seq, d), _const_spec(g), _tok_spec(n_seq, d), _tok_spec(n_seq, d)],
        out_specs=_tok_spec(n_seq, d),
        out_shape=jax.ShapeDtypeStruct((1, n_seq, d), F32),
        compiler_params=_params("parallel", "parallel"),
        name="last_rows",
    )(xs, g, scs, shs)[0]


def kernel(x_prompt, x_sample, c_prompt, c_sample, cache_k, cache_v, page_table, state_b_conv, state_b_ssm, state_c_shift, state_c_wkv, ln_mix, ln_mlp, w_ada, b_ada, w_up, w_down, a_w_qkv, a_q_norm, a_k_norm, a_logit_bias, a_w_o, b_w_in, b_conv, b_a_log, b_dt_bias, b_o_norm, b_w_o, c_mu, c_w_rkv, c_w0, c_w1, c_w2, c_a0, c_a1, c_a2, c_g1, c_g2, c_k_k, c_k_a, c_r_k, c_ln_g, c_ln_b, c_w_o):
    bp, t_p, d = x_prompt.shape
    bs, t_s, _ = x_sample.shape
    depth = ln_mix.shape[0]
    n_c = c_mu.shape[0]

    def lora_cols(w):
        return jnp.pad(w, ((0, 0), (0, 0), (0, LANES - w.shape[2]))).astype(BF16)

    def lora_rows(w):
        return jnp.pad(w, ((0, 0), (0, LANES - w.shape[1]), (0, 0))).astype(BF16)

    b_in = b_w_in.shape[2]
    b_in_pad = -(-b_in // LANES) * LANES
    W = {
        'ln_mix': ln_mix, 'ln_mlp': ln_mlp,
        'w_up': w_up.astype(BF16), 'w_down': w_down.astype(BF16),
        'a_w_qkv': a_w_qkv.astype(BF16), 'a_q_norm': a_q_norm, 'a_k_norm': a_k_norm,
        'a_logit_bias': a_logit_bias, 'a_w_o': a_w_o.astype(BF16),
        'b_w_in': jnp.pad(b_w_in, ((0, 0), (0, 0), (0, b_in_pad - b_in))).astype(BF16),
        'b_conv': b_conv, 'b_a_log': b_a_log, 'b_dt_bias': b_dt_bias, 'b_o_norm': b_o_norm,
        'b_w_o': b_w_o.astype(BF16),
        'c_r_k': c_r_k.reshape(n_c, d), 'c_ln_g': c_ln_g, 'c_ln_b': c_ln_b, 'c_w_o': c_w_o.astype(BF16),
    }
    w1, w2 = lora_cols(c_w1), lora_rows(c_w2)
    a1, a2 = lora_cols(c_a1), lora_rows(c_a2)
    g1, g2 = c_g1.astype(BF16), c_g2.astype(BF16)
    rkv = c_w_rkv.astype(BF16)
    W['c'] = [{'mu': c_mu[j], 'w_rkv': rkv[j], 'w1': w1[j], 'w2': w2[j], 'a1': a1[j], 'a2': a2[j],
               'g1': g1[j], 'g2': g2[j], 'w0': c_w0[j].reshape(1, d), 'a0': c_a0[j].reshape(1, d),
               'k_k': c_k_k[j].reshape(1, d), 'k_a': c_k_a[j].reshape(1, d)} for j in range(n_c)]

    mod_all = _ada(jnp.concatenate([c_prompt, c_sample], axis=0), w_ada, b_ada)
    mod_p = mod_all[:, :, :bp].reshape(depth, 6, bp, 1, d)
    mod_s = jnp.repeat(mod_all[:, :, bp:], t_s, axis=2).reshape(depth, 6, 1, bs * t_s, d)
    mods_p = [[mod_p[l, n] for n in range(6)] for l in range(depth)]
    mods_s = [[mod_s[l, n] for n in range(6)] for l in range(depth)]

    y_p, k_p, v_p, bconv_p, bssm_p, cshift_p, cwkv_p = _run_trunk(
        x_prompt, mods_p, W, 512, t_p, None, None)
    state = {'b_conv': state_b_conv, 'b_ssm': state_b_ssm, 'c_shift': state_c_shift,
             'c_wkv': state_c_wkv}
    y_s, k_s, v_s, bconv_s, bssm_s, cshift_s, cwkv_s = _run_trunk(
        x_sample.reshape(1, bs * t_s, d), mods_s, W, bs * t_s, t_s,
        (cache_k, cache_v, page_table), state)
    st = jnp.stack
    return (y_p, y_s.reshape(bs, t_s, d), k_p, v_p, k_s, v_s, st(bconv_p), st(bssm_p),
            st(bconv_s), st(bssm_s), st(cshift_p), st(cwkv_p), st(cshift_s), st(cwkv_s))
```

```python
import functools

import jax
import jax.numpy as jnp
from jax import lax
from jax.experimental import pallas as pl
from jax.experimental.pallas import tpu as pltpu

F32 = jnp.float32
BF16 = jnp.bfloat16

LANES = 128
MXU_DIM = 256
VMEM_LIMIT_BYTES = 56 * 1024 * 1024

NORM_EPS = 1e-6
L2_EPS = 1e-6
GN_EPS = 64e-5
LOG2E = 1.4426950408889634
A_HEAD_DIM = 64
B_HEAD_DIM = 128
C_HEAD_DIM = 64
B_CONV = 4
CHUNK = 64
CHAINS_PER_STEP = 16
HEADS_PER_STEP = 4
SB_QUERY_BLOCK = 512
SB_KEY_BLOCK = MXU_DIM
MLP_TOKEN_TILE = 1024


def _params(*sem):
    return pltpu.CompilerParams(dimension_semantics=sem, vmem_limit_bytes=VMEM_LIMIT_BYTES)


def _bdot(a, b):
    return jnp.dot(a.astype(BF16), b.astype(BF16), preferred_element_type=F32)


def _bdot_nt(a, b):
    return lax.dot_general(a.astype(BF16), b.astype(BF16), (((1,), (1,)), ((), ())),
                           preferred_element_type=F32)


def _bdot_tn(a, b):
    return lax.dot_general(a.astype(BF16), b.astype(BF16), (((0,), (0,)), ((), ())),
                           preferred_element_type=F32)


def _split(x):
    hi = x.astype(BF16)
    lo = (x - hi.astype(F32)).astype(BF16)
    return hi, lo


def _sum_dots(lhs, rhs):
    return jnp.dot(jnp.concatenate(lhs, axis=1), jnp.concatenate(rhs, axis=0),
                   preferred_element_type=F32)


def _dot_xe(x, e):
    hi, lo = _split(x)
    return _sum_dots([hi, lo], [e, e])


def _dot_ex(e, x):
    hi, lo = _split(x)
    return _sum_dots([e, e], [hi, lo])


def _dot3_parts(a_parts, b_parts):
    ah, al = a_parts
    bh, bl = b_parts
    return _sum_dots([ah, ah, al], [bh, bl, bh])


def _dot3(a, b):
    return _dot3_parts(_split(a), _split(b))


def _sigmoid(x):
    return 1.0 / (1.0 + jnp.exp(-x))


def _silu(x):
    return x * _sigmoid(x)


def _softplus(x):
    return jnp.maximum(x, 0.0) + jnp.log(1.0 + jnp.exp(-jnp.abs(x)))


def _neg_abs(x):
    bits = lax.bitcast_convert_type(x, jnp.uint32) | jnp.uint32(0x80000000)
    return lax.bitcast_convert_type(bits, F32)


def _sb_suffix(z, suffix2, mask):
    sp = jnp.maximum(z, 0.0) + jnp.log(1.0 + jnp.exp2(_neg_abs(z))) * LOG2E
    if mask is not None:
        sp = jnp.where(mask, sp, 0.0)
    hi, lo = _split(sp)
    return jnp.dot(jnp.concatenate([hi, lo], axis=1), suffix2, preferred_element_type=F32)


def _sb_local(z, s_incl, mask):
    w = jnp.exp2(z - s_incl)
    if mask is not None:
        w = jnp.where(mask, w, 0.0)
    return w.astype(BF16)


def _sb_weights(z, suffix2, mask):
    s_incl = _sb_suffix(z, suffix2, mask)
    return _sb_local(z, s_incl, mask), s_incl[:, 0:1]


def _normmod(x, g, sc, sh):
    ms = jnp.mean(x * x, axis=-1, keepdims=True)
    return x * lax.rsqrt(ms + NORM_EPS) * g * (1.0 + sc) + sh


def _group_matrix(n, group, value):
    r = lax.broadcasted_iota(jnp.int32, (n, n), 0) // group
    c = lax.broadcasted_iota(jnp.int32, (n, n), 1) // group
    return jnp.where(r == c, value, 0.0).astype(BF16)


def _group_reduce(x, gmat, split=False):
    n = gmat.shape[0]
    dot = _dot_xe if split else _bdot
    parts = [dot(x[:, s:s + n], gmat) for s in range(0, x.shape[1], n)]
    return parts[0] if len(parts) == 1 else jnp.concatenate(parts, axis=1)


def _tri_inverses(mats, n_steps):
    n = mats[0].shape[0]
    eye = (lax.broadcasted_iota(jnp.int32, (n, n), 0)
           == lax.broadcasted_iota(jnp.int32, (n, n), 1)).astype(F32)
    ps = [-a for a in mats]
    ts = [eye + p for p in ps]
    parts = [_split(p) for p in ps]
    for _ in range(n_steps):
        parts = [_split(_dot3_parts(s, s)) for s in parts]
        ts = [t + _dot3_parts(_split(t), s) for t, s in zip(ts, parts)]
    return ts


def _ada_kernel(c_ref, w_ref, b_ref, o_ref):
    ca = _silu(c_ref[...])
    o_ref[0, 0] = _bdot(ca, w_ref[0]) + b_ref[0]


def _ada(c_all, w_ada, b_ada):
    depth, d, n = w_ada.shape
    rows = c_all.shape[0]
    return pl.pallas_call(
        _ada_kernel,
        grid=(depth, n // d),
        in_specs=[pl.BlockSpec((rows, d), lambda l, j: (0, 0)),
                  pl.BlockSpec((1, d, d), lambda l, j: (l, 0, j)),
                  pl.BlockSpec((1, 1, d), lambda l, j: (l, 0, j))],
        out_specs=pl.BlockSpec((1, 1, rows, d), lambda l, j: (l, j, 0, 0)),
        out_shape=jax.ShapeDtypeStruct((depth, n // d, rows, d), F32),
        compiler_params=_params("parallel", "parallel"),
        name="ada",
    )(c_all, w_ada, b_ada.reshape(depth, 1, n))


def _tok_spec(tm, width):
    return pl.BlockSpec((1, tm, width), lambda b, i: (b, i, 0))


def _mod_spec(mod, tm):
    if mod.shape[1] == 1:
        return pl.BlockSpec((1, 1, mod.shape[2]), lambda b, i: (b, 0, 0))
    return pl.BlockSpec((1, tm, mod.shape[2]), lambda b, i: (b, i, 0))


def _const_spec(arr):
    nd = arr.ndim
    return pl.BlockSpec(arr.shape, lambda b, i: (0,) * nd)


def _prev8_spec(tm, width):
    blocks = tm // 8
    return pl.BlockSpec((1, 8, width), lambda b, i: (b, jnp.maximum(i * blocks - 1, 0), 0))


def _mlp_kernel(x_ref, g_ref, sc_ref, sh_ref, gt_ref, wu_ref, wd_ref, o_ref, h_ref, acc_ref):
    f = pl.program_id(2)

    @pl.when(f == 0)
    def _():
        h_ref[...] = _normmod(x_ref[0], g_ref[...], sc_ref[0], sh_ref[0]).astype(BF16)
        acc_ref[...] = jnp.zeros_like(acc_ref)

    u = jnp.maximum(jnp.dot(h_ref[...], wu_ref[0], preferred_element_type=F32), 0.0)
    acc_ref[...] += jnp.dot((u * u).astype(BF16), wd_ref[0], preferred_element_type=F32)

    @pl.when(f == pl.num_programs(2) - 1)
    def _():
        o_ref[0] = x_ref[0] + gt_ref[0] * acc_ref[...]


def _mlp(x, g, sc, sh, gt, w_up, w_down, layer, tm):
    bsz, t, d = x.shape
    ff = w_up.shape[2]
    tf = 1024

    def tok(b, i, f):
        return (b, i, 0)

    def mod_spec(m):
        if m.shape[1] == 1:
            return pl.BlockSpec((1, 1, d), lambda b, i, f: (b, 0, 0))
        return pl.BlockSpec((1, tm, d), tok)

    return pl.pallas_call(
        _mlp_kernel,
        grid=(bsz, t // tm, ff // tf),
        in_specs=[pl.BlockSpec((1, tm, d), tok),
                  pl.BlockSpec((1, d), lambda b, i, f: (0, 0)),
                  mod_spec(sc), mod_spec(sh), mod_spec(gt),
                  pl.BlockSpec((1, d, tf), lambda b, i, f: (layer, 0, f)),
                  pl.BlockSpec((1, tf, d), lambda b, i, f: (layer, f, 0))],
        out_specs=pl.BlockSpec((1, tm, d), tok),
        out_shape=jax.ShapeDtypeStruct(x.shape, F32),
        scratch_shapes=[pltpu.VMEM((tm, d), BF16), pltpu.VMEM((tm, d), F32)],
        compiler_params=_params("parallel", "parallel", "arbitrary"),
        name="mlp",
    )(x, g, sc, sh, gt, w_up, w_down)


def _qkv_kernel(x_ref, g_ref, sc_ref, sh_ref, w_ref, qg_ref, kg_ref, *rest):
    q_ref, k_ref, v_ref = rest[-3:]
    d = x_ref.shape[2]
    h = _normmod(x_ref[0], g_ref[...], sc_ref[0], sh_ref[0]).astype(BF16)
    gmat = _group_matrix(MXU_DIM, A_HEAD_DIM, 1.0 / A_HEAD_DIM)
    q_scale = A_HEAD_DIM ** -0.5 * LOG2E
    for s in range(0, d, MXU_DIM):
        sl = slice(s, s + MXU_DIM)
        q = jnp.dot(h, w_ref[0, :, s:s + MXU_DIM], preferred_element_type=F32)
        q_ref[0, :, sl] = q * lax.rsqrt(_bdot(q * q, gmat) + NORM_EPS) * qg_ref[...] * q_scale
        k = jnp.dot(h, w_ref[0, :, d + s:d + s + MXU_DIM], preferred_element_type=F32)
        k_ref[0, 0, :, sl] = k * lax.rsqrt(_bdot(k * k, gmat) + NORM_EPS) * kg_ref[...]
        v_ref[0, 0, :, sl] = jnp.dot(h, w_ref[0, :, 2 * d + s:2 * d + s + MXU_DIM],
                                     preferred_element_type=F32)


def _qkv(x, g, sc, sh, w_all, layer, q_gain, k_gain, kv_all, tm):
    bsz, t, d = x.shape
    n_layers = w_all.shape[0]
    qg = jnp.tile(q_gain, MXU_DIM // A_HEAD_DIM).reshape(1, MXU_DIM)
    kg = jnp.tile(k_gain, MXU_DIM // A_HEAD_DIM).reshape(1, MXU_DIM)
    kv_shape = jax.ShapeDtypeStruct((n_layers, bsz, t, d), F32)
    kv_spec = pl.BlockSpec((1, 1, tm, d), lambda b, i: (layer, b, i, 0))
    w_spec = pl.BlockSpec((1,) + w_all.shape[1:], lambda b, i: (layer, 0, 0))
    aliased = [] if kv_all is None else list(kv_all)
    n_in = 7
    return pl.pallas_call(
        _qkv_kernel,
        grid=(bsz, t // tm),
        in_specs=[_tok_spec(tm, d), _const_spec(g), _mod_spec(sc, tm), _mod_spec(sh, tm),
                  w_spec, _const_spec(qg), _const_spec(kg)]
                 + [pl.BlockSpec(memory_space=pl.ANY)] * len(aliased),
        out_specs=[_tok_spec(tm, d), kv_spec, kv_spec],
        out_shape=[jax.ShapeDtypeStruct(x.shape, F32), kv_shape, kv_shape],
        input_output_aliases={n_in + n: 1 + n for n in range(len(aliased))},
        compiler_params=_params("parallel", "parallel"),
        name="qkv",
    )(x, g, sc, sh, w_all, qg, kg, *aliased)


def _sb_prompt_kernel(bias_ref, q_ref, k_ref, v_ref, o_ref, z_ref, s_ref, acc_ref, c_ref, *, qb, kb):
    hp = pl.program_id(1)
    i = pl.program_id(2)
    q = q_ref[0]
    lane = lax.broadcasted_iota(jnp.int32, (1, LANES), 1)
    row = lax.broadcasted_iota(jnp.int32, (kb, kb), 0)
    col = lax.broadcasted_iota(jnp.int32, (kb, kb), 1)
    suffix = (row >= col).astype(BF16)
    suffix2 = jnp.concatenate([suffix, suffix], axis=0)
    qm = [jnp.where(lane // A_HEAD_DIM == e, q, 0.0).astype(BF16) for e in range(2)]
    bias = [bias_ref[2 * hp + e] * LOG2E for e in range(2)]
    per_q = qb // kb
    n_blocks = per_q * (i + 1)

    def rows(t):
        return pl.ds(pl.multiple_of(jnp.maximum(n_blocks - 1 - t, 0) * kb, kb), kb)

    def mask(t):
        if not isinstance(t, int) or t >= per_q:
            return None
        q_pos = lax.broadcasted_iota(jnp.int32, (qb, kb), 0)
        k_pos = lax.broadcasted_iota(jnp.int32, (qb, kb), 1) + (per_q - 1 - t) * kb
        return k_pos < q_pos

    def logits(t, slot, e):
        z_ref[slot, e] = _bdot_nt(qm[e], k_ref[0, 0, rows(t), :]) + bias[e]

    def suffix_sums(t, slot, e):
        s_ref[slot, e] = _sb_suffix(z_ref[slot, e], suffix2, mask(t))

    def consume(t, slot, e):
        s_incl = s_ref[slot, e]
        p = _bdot(_sb_local(z_ref[slot, e], s_incl, mask(t)), v_ref[0, 0, rows(t), :])
        acc_ref[e] += jnp.exp2(-c_ref[e]) * p
        c_ref[e] += s_incl[:, 0:1]

    def trip(t, slot):
        for e in range(2):
            consume(t, slot, e)
            logits(t + 2, slot, e)
            suffix_sums(t + 1, 1 - slot, e)

    acc_ref[...] = jnp.zeros_like(acc_ref)
    c_ref[...] = jnp.zeros_like(c_ref)
    for e in range(2):
        logits(0, 0, e)
        logits(1, 1, e)
        suffix_sums(0, 0, e)
    for t in range(per_q):
        trip(t, t % 2)

    def pair(u, _):
        trip(2 * u + per_q, 0)
        trip(2 * u + per_q + 1, 1)
        return 0

    lax.fori_loop(0, (per_q // 2) * i, pair, 0)
    o_ref[0] = jnp.where(lane < A_HEAD_DIM, acc_ref[0], acc_ref[1])


def _sb_prompt(q, k_all, v_all, layer, logit_bias, qb, kb):
    bsz, t, d = q.shape
    n_pairs = d // LANES
    assert qb % (2 * kb) == 0 and t % qb == 0
    kv_spec = pl.BlockSpec((1, 1, t, LANES), lambda b, p, i, bias: (layer, b, 0, p))
    grid_spec = pltpu.PrefetchScalarGridSpec(
        num_scalar_prefetch=1,
        grid=(bsz, n_pairs, t // qb),
        in_specs=[pl.BlockSpec((1, qb, LANES), lambda b, p, i, bias: (b, i, p)), kv_spec, kv_spec],
        out_specs=pl.BlockSpec((1, qb, LANES), lambda b, p, i, bias: (b, i, p)),
        scratch_shapes=[pltpu.VMEM((2, 2, qb, kb), F32), pltpu.VMEM((2, 2, qb, kb), F32),
                        pltpu.VMEM((2, qb, LANES), F32), pltpu.VMEM((2, qb, 1), F32)],
    )
    return pl.pallas_call(
        functools.partial(_sb_prompt_kernel, qb=qb, kb=kb),
        grid_spec=grid_spec,
        out_shape=jax.ShapeDtypeStruct(q.shape, F32),
        compiler_params=_params("parallel", "parallel", "arbitrary"),
        name="sb_prompt",
    )(logit_bias, q, k_all, v_all)


def _sb_decode_kernel(pt_ref, q_ref, bias_ref, kn_ref, vn_ref, *rest, pages_per_step, page, t_new):
    k_refs = rest[:pages_per_step]
    v_refs = rest[pages_per_step:2 * pages_per_step]
    o_ref = rest[2 * pages_per_step]
    acc_ref, c_ref, qbd_ref = rest[2 * pages_per_step + 1:]
    s = pl.program_id(1)
    d = q_ref.shape[2]
    n_heads = d // A_HEAD_DIM
    n_rows = n_heads * t_new
    r_head = lax.broadcasted_iota(jnp.int32, (n_rows, d), 0) // t_new
    c_head = lax.broadcasted_iota(jnp.int32, (n_rows, d), 1) // A_HEAD_DIM

    @pl.when(s == 0)
    def _():
        tiled = jnp.concatenate([q_ref[0]] * n_heads, axis=0)
        qbd_ref[...] = jnp.where(r_head == c_head, tiled, 0.0).astype(BF16)

    qbd = qbd_ref[...]
    bias = bias_ref[...]
    row = lax.broadcasted_iota(jnp.int32, (page, page), 0)
    col = lax.broadcasted_iota(jnp.int32, (page, page), 1)
    suffix = (row >= col).astype(BF16)
    suffix2 = jnp.concatenate([suffix, suffix], axis=0)

    def accumulate(p, tot):
        acc_ref[...] += jnp.exp2(-c_ref[...]) * p
        c_ref[...] += tot

    @pl.when(s == 0)
    def _():
        acc_ref[...] = jnp.zeros_like(acc_ref)
        c_ref[...] = jnp.zeros_like(c_ref)
        pad = jnp.zeros((page - t_new, d), F32)
        k_pad = jnp.concatenate([kn_ref[0, 0], pad], axis=0).astype(BF16)
        v_pad = jnp.concatenate([vn_ref[0, 0], pad], axis=0).astype(BF16)
        q_t = lax.broadcasted_iota(jnp.int32, (n_rows, page), 0) % t_new
        key_i = lax.broadcasted_iota(jnp.int32, (n_rows, page), 1)
        w, tot = _sb_weights(_bdot_nt(qbd, k_pad) + bias, suffix2, key_i < q_t)
        accumulate(jnp.dot(w, v_pad, preferred_element_type=F32), tot)

    zs = [jnp.dot(qbd, k_refs[r][0, 0].astype(BF16), preferred_element_type=F32) + bias
          for r in range(pages_per_step)]
    ws = [_sb_weights(z, suffix2, None) for z in zs]
    ps = [_bdot_nt(w, v_refs[r][0, 0]) for r, (w, _) in enumerate(ws)]
    for p, (_, tot) in zip(ps, ws):
        accumulate(p, tot)

    @pl.when(s == pl.num_programs(1) - 1)
    def _():
        diag = jnp.where(r_head == c_head, acc_ref[...], 0.0)
        out = diag[0:t_new]
        for h in range(1, n_heads):
            out = out + diag[h * t_new:(h + 1) * t_new]
        o_ref[0] = out


def _sb_decode(q, k_all, v_all, new_layer, cache_k, cache_v, layer, page_table, logit_bias,
               pages_per_step):
    bsz, t_new, d = q.shape
    n_heads = d // A_HEAD_DIM
    n_layers, n_pool, page = cache_k.shape[:3]
    n_pages = page_table.shape[1]
    n_rows = n_heads * t_new
    ck = cache_k.transpose(0, 1, 3, 4, 2).reshape(n_layers, n_pool, d, page)
    cv = cache_v.transpose(0, 1, 3, 4, 2).reshape(n_layers, n_pool, d, page)
    bias_rows = jnp.broadcast_to(jnp.repeat(logit_bias * LOG2E, t_new)[:, None], (n_rows, page))
    n_steps = n_pages // pages_per_step

    def page_map(r):
        def index(b, s, pt):
            return (layer, pt[b, n_pages - 1 - (s * pages_per_step + r)], 0, 0)
        return index

    kv_specs = [pl.BlockSpec((1, 1, d, page), page_map(r)) for r in range(pages_per_step)]
    grid_spec = pltpu.PrefetchScalarGridSpec(
        num_scalar_prefetch=1,
        grid=(bsz, n_steps),
        in_specs=[pl.BlockSpec((1, t_new, d), lambda b, s, pt: (b, 0, 0)),
                  pl.BlockSpec((n_rows, page), lambda b, s, pt: (0, 0)),
                  pl.BlockSpec((1, 1, t_new, d), lambda b, s, pt: (new_layer, b, 0, 0)),
                  pl.BlockSpec((1, 1, t_new, d), lambda b, s, pt: (new_layer, b, 0, 0))]
                 + kv_specs + kv_specs,
        out_specs=pl.BlockSpec((1, t_new, d), lambda b, s, pt: (b, 0, 0)),
        scratch_shapes=[pltpu.VMEM((n_rows, d), F32), pltpu.VMEM((n_rows, 1), F32),
                        pltpu.VMEM((n_rows, d), BF16)],
    )
    return pl.pallas_call(
        functools.partial(_sb_decode_kernel, pages_per_step=pages_per_step, page=page, t_new=t_new),
        grid_spec=grid_spec,
        out_shape=jax.ShapeDtypeStruct(q.shape, F32),
        compiler_params=_params("parallel", "arbitrary"),
        name="sb_decode",
    )(page_table, q, bias_rows, k_all, v_all, *([ck] * pages_per_step), *([cv] * pages_per_step))


def _proj_res_kernel(a_ref, w_ref, x_ref, gt_ref, o_ref):
    o_ref[0] = x_ref[0] + gt_ref[0] * _bdot(a_ref[0], w_ref[...])


def _proj_res(a, w, x, gt, tm):
    bsz, t, d = x.shape
    return pl.pallas_call(
        _proj_res_kernel,
        grid=(bsz, t // tm),
        in_specs=[_tok_spec(tm, d), _const_spec(w), _tok_spec(tm, d), _mod_spec(gt, tm)],
        out_specs=_tok_spec(tm, d),
        out_shape=jax.ShapeDtypeStruct(x.shape, F32),
        compiler_params=_params("parallel", "parallel"),
        name="proj_res",
    )(a, w, x, gt)


def _gdn_out_kernel(o_ref_in, z_ref, on_ref, w_ref, x_ref, gt_ref, o_ref):
    o = o_ref_in[0]
    gmat = _group_matrix(MXU_DIM, B_HEAD_DIM, 1.0 / B_HEAD_DIM)
    ms = _group_reduce(o * o, gmat)
    a = o * lax.rsqrt(ms + NORM_EPS) * on_ref[...] * _silu(z_ref[0])
    o_ref[0] = x_ref[0] + gt_ref[0] * _bdot(a, w_ref[...])


def _gdn_out(o, z, o_norm, w, x, gt, tm):
    bsz, t, d = x.shape
    on = jnp.tile(o_norm, d // B_HEAD_DIM).reshape(1, d)
    return pl.pallas_call(
        _gdn_out_kernel,
        grid=(bsz, t // tm),
        in_specs=[_tok_spec(tm, d), _tok_spec(tm, d), _const_spec(on), _const_spec(w),
                  _tok_spec(tm, d), _mod_spec(gt, tm)],
        out_specs=_tok_spec(tm, d),
        out_shape=jax.ShapeDtypeStruct(x.shape, F32),
        compiler_params=_params("parallel", "parallel"),
        name="gdn_out",
    )(o, z, on, w, x, gt)


def _rwkv_out_kernel(y_ref, r_ref, k_ref, v_ref, gate_ref, rk_ref, lg_ref, lb_ref, w_ref,
                     x_ref, gt_ref, o_ref):
    y = y_ref[0]
    mean_mat = _group_matrix(MXU_DIM, C_HEAD_DIM, 1.0 / C_HEAD_DIM)
    sum_mat = _group_matrix(MXU_DIM, C_HEAD_DIM, 1.0)
    yc = y - _group_reduce(y, mean_mat, split=True)
    var = _group_reduce(yc * yc, mean_mat)
    yn = yc * lax.rsqrt(var + GN_EPS) * lg_ref[...] + lb_ref[...]
    bonus = _group_reduce(r_ref[0] * k_ref[0] * rk_ref[...], sum_mat) * v_ref[0]
    a = (yn + bonus) * gate_ref[0]
    o_ref[0] = x_ref[0] + gt_ref[0] * _bdot(a, w_ref[...])


def _rwkv_out(y, r, k, v, gate, r_k, ln_g, ln_b, w, x, gt, tm):
    bsz, t, d = x.shape
    rk = r_k.reshape(1, d)
    lg = ln_g.reshape(1, d)
    lb = ln_b.reshape(1, d)
    tok = _tok_spec(tm, d)
    return pl.pallas_call(
        _rwkv_out_kernel,
        grid=(bsz, t // tm),
        in_specs=[tok, tok, tok, tok, tok, _const_spec(rk), _const_spec(lg), _const_spec(lb),
                  _const_spec(w), tok, _mod_spec(gt, tm)],
        out_specs=tok,
        out_shape=jax.ShapeDtypeStruct(x.shape, F32),
        compiler_params=_params("parallel", "parallel"),
        name="rwkv_out",
    )(y, r, k, v, gate, rk, lg, lb, w, x, gt)


def _gdn_in_kernel(x_ref, g_ref, sc_ref, sh_ref, w_ref, xqkv_ref, z_ref, ba_ref):
    n_conv = xqkv_ref.shape[2]
    n_z = z_ref.shape[2]
    h = _normmod(x_ref[0], g_ref[...], sc_ref[0], sh_ref[0]).astype(BF16)
    for s in range(0, n_conv, MXU_DIM):
        xqkv_ref[0, :, s:s + MXU_DIM] = jnp.dot(h, w_ref[:, s:s + MXU_DIM],
                                                preferred_element_type=F32)
    for s in range(0, n_z, MXU_DIM):
        z_ref[0, :, s:s + MXU_DIM] = jnp.dot(h, w_ref[:, n_conv + s:n_conv + s + MXU_DIM],
                                             preferred_element_type=F32)
    ba_ref[0] = jnp.dot(h, w_ref[:, n_conv + n_z:], preferred_element_type=F32)


def _gdn_in(x, g, sc, sh, w_pad, n_conv, tm):
    bsz, t, d = x.shape
    return pl.pallas_call(
        _gdn_in_kernel,
        grid=(bsz, t // tm),
        in_specs=[_tok_spec(tm, d), _const_spec(g), _mod_spec(sc, tm), _mod_spec(sh, tm),
                  _const_spec(w_pad)],
        out_specs=[_tok_spec(tm, n_conv), _tok_spec(tm, d), _tok_spec(tm, LANES)],
        out_shape=[jax.ShapeDtypeStruct((bsz, t, n_conv), F32),
                   jax.ShapeDtypeStruct((bsz, t, d), F32),
                   jax.ShapeDtypeStruct((bsz, t, LANES), F32)],
        compiler_params=_params("parallel", "parallel"),
        name="gdn_in",
    )(x, g, sc, sh, w_pad)


def _gdn_conv_kernel(*refs, pre_shifted):
    if pre_shifted:
        x_ref, xs_ref, cw_ref, ba_ref, alog_ref, dtb_ref = refs[:6]
    else:
        x_ref, p8_ref, cw_ref, ba_ref, alog_ref, dtb_ref = refs[:6]
    q_ref, k_ref, kb_ref, vb_ref, g_ref = refs[6:]
    i = pl.program_id(1)
    tm = x_ref.shape[1]
    hd = q_ref.shape[2]
    n_heads = hd // B_HEAD_DIM
    x = x_ref[0]
    conv = x * cw_ref[B_CONV - 1:B_CONV, :]
    if pre_shifted:
        for s in range(1, B_CONV):
            conv = conv + xs_ref[s - 1, 0] * cw_ref[B_CONV - 1 - s:B_CONV - s, :]
    else:
        full = jnp.concatenate([p8_ref[0], x], axis=0)
        t_glob = i * tm + lax.broadcasted_iota(jnp.int32, (tm, 1), 0)
        for s in range(1, B_CONV):
            xs = pltpu.roll(full, s, 0)[8:]
            xs = jnp.where(t_glob >= s, xs, 0.0)
            conv = conv + xs * cw_ref[B_CONV - 1 - s:B_CONV - s, :]
    act = _silu(conv)
    r = lax.broadcasted_iota(jnp.int32, (LANES, hd), 0)
    c_head = lax.broadcasted_iota(jnp.int32, (LANES, hd), 1) // B_HEAD_DIM
    e_b = (r == c_head).astype(BF16)
    e_a = (r == c_head + n_heads).astype(BF16)
    ba = ba_ref[0]
    beta = _sigmoid(_dot_xe(ba, e_b))
    g = -jnp.exp(alog_ref[...]) * _softplus(_dot_xe(ba, e_a) + dtb_ref[...])
    g_ref[0] = g
    ones = _group_matrix(MXU_DIM, B_HEAD_DIM, 1.0)
    aq = act[:, :hd]
    ak = act[:, hd:2 * hd]
    q_ref[0] = aq * lax.rsqrt(_group_reduce(aq * aq, ones) + L2_EPS) * (B_HEAD_DIM ** -0.5)
    k = ak * lax.rsqrt(_group_reduce(ak * ak, ones) + L2_EPS)
    k_ref[0] = k
    kb_ref[0] = k * beta
    vb_ref[0] = act[:, 2 * hd:] * beta


def _gdn_conv(xqkv, xs, conv_w, ba, a_log, dt_bias, hd, tm):
    bsz, t, n_conv = xqkv.shape
    alog = jnp.repeat(a_log, B_HEAD_DIM).reshape(1, hd)
    dtb = jnp.repeat(dt_bias, B_HEAD_DIM).reshape(1, hd)
    pre_shifted = xs is not None
    if pre_shifted:
        second = xs
        second_spec = pl.BlockSpec((B_CONV - 1, 1, tm, n_conv), lambda b, i: (0, b, i, 0))
    else:
        second = xqkv
        second_spec = _prev8_spec(tm, n_conv)
    shp = jax.ShapeDtypeStruct((bsz, t, hd), F32)
    return pl.pallas_call(
        functools.partial(_gdn_conv_kernel, pre_shifted=pre_shifted),
        grid=(bsz, t // tm),
        in_specs=[_tok_spec(tm, n_conv), second_spec, _const_spec(conv_w), _tok_spec(tm, LANES),
                  _const_spec(alog), _const_spec(dtb)],
        out_specs=[_tok_spec(tm, hd)] * 5,
        out_shape=[shp] * 5,
        compiler_params=_params("parallel", "parallel"),
        name="gdn_conv",
    )(xqkv, second, conv_w, ba, alog, dtb)


def _gdn_chunk_kernel(q_ref, k_ref, kb_ref, vb_ref, g_ref, s0_ref, o_ref, s_out_ref, s_ref, *,
                      n_heads, n_chunks):
    i = pl.program_id(2)
    c = CHUNK
    hw = B_HEAD_DIM

    @pl.when(i == 0)
    def _():
        s_ref[...] = s0_ref[0]

    row = lax.broadcasted_iota(jnp.int32, (c, c), 0)
    col = lax.broadcasted_iota(jnp.int32, (c, c), 1)
    lower = row >= col
    strict = row > col
    lower_b = lower.astype(BF16)
    ones_avg = jnp.full((c, LANES), 1.0 / LANES, BF16)

    def row_layout(g_cb):
        gh, gl = _split(g_cb)
        nt = lambda x: lax.dot_general(ones_avg, x, (((1,), (1,)), ((), ())),
                                       preferred_element_type=F32)
        return nt(gh) + nt(gl)

    sls = [(slice(n * c, (n + 1) * c), slice(h * hw, (h + 1) * hw))
           for h in range(n_heads) for n in range(n_chunks)]
    rng = range(len(sls))
    q = [q_ref[0, rs, ls] for rs, ls in sls]
    k = [k_ref[0, rs, ls] for rs, ls in sls]
    kb = [kb_ref[0, rs, ls] for rs, ls in sls]
    vb = [vb_ref[0, rs, ls] for rs, ls in sls]
    g_cb = [_dot_ex(lower_b, g_ref[0, rs, ls]) for rs, ls in sls]
    g_row = [row_layout(x) for x in g_cb]
    decay = [jnp.where(lower, jnp.exp(jnp.where(lower, g_cb[n][:, :c] - g_row[n], 0.0)), 0.0)
             for n in rng]
    a = [jnp.where(strict, decay[n] * _bdot_nt(kb[n], k[n]), 0.0) for n in rng]
    t_inv = _tri_inverses(a, 5)
    e_g = [jnp.exp(x) for x in g_cb]
    wu = [_dot3(t_inv[n], jnp.concatenate([kb[n] * e_g[n], vb[n]], axis=1)) for n in rng]
    qk = [decay[n] * _bdot_nt(q[n], k[n]) for n in rng]
    g_last = [x[c - 1:c, :] for x in g_cb]
    k_end = [jnp.exp(g_last[n] - g_cb[n]) * k[n] for n in rng]
    qw = [_bdot(qk[n], wu[n]) for n in rng]
    q_eff = [q[n] * e_g[n] - qw[n][:, :hw] for n in rng]
    o0 = [x[:, hw:] for x in qw]
    mn = [_bdot_tn(k_end[n], wu[n]) for n in rng]
    m0 = [x[:, :hw] for x in mn]
    n0 = [x[:, hw:] for x in mn]

    states = [s_ref[h] for h in range(n_heads)]
    for m in range(n_chunks):
        for h in range(n_heads):
            n = h * n_chunks + m
            rs, ls = sls[n]
            s = states[h]
            o_ref[0, rs, ls] = _bdot(q_eff[n], s) + o0[n]
            states[h] = jnp.exp(g_last[n]) * s - _bdot(m0[n], s) + n0[n]
    for h in range(n_heads):
        s_ref[h] = states[h]
        s_out_ref[0, h] = states[h]


def _gdn_chunk(q, k, kb, vb, g, s0, ct, heads_per_step):
    bsz, t, hd = q.shape
    hw = B_HEAD_DIM
    blk = pl.BlockSpec((1, ct, heads_per_step * hw), lambda b, h, i: (b, i, h))
    st = pl.BlockSpec((1, heads_per_step, hw, hw), lambda b, h, i: (b, h, 0, 0))
    return pl.pallas_call(
        functools.partial(_gdn_chunk_kernel, n_heads=heads_per_step, n_chunks=ct // CHUNK),
        grid=(bsz, hd // (heads_per_step * hw), t // ct),
        in_specs=[blk] * 5 + [st],
        out_specs=[blk, st],
        out_shape=[jax.ShapeDtypeStruct(q.shape, F32), jax.ShapeDtypeStruct(s0.shape, F32)],
        scratch_shapes=[pltpu.VMEM((heads_per_step, hw, hw), F32)],
        compiler_params=_params("parallel", "parallel", "arbitrary"),
        name="gdn_chunk",
    )(q, k, kb, vb, g, s0)


def _rwkv_in_kernel(*refs, seq_len, per_row_shift):
    (x_ref, p8_ref, g_ref, sc_ref, sh_ref, sh0_ref, mu_ref, wrkv_ref, w1_ref, w2_ref, a1_ref, a2_ref,
     g1_ref, g2_ref, w0_ref, a0_ref, kk_ref, ka_ref) = refs[:18]
    r_ref, lw_ref, k_ref, v_ref, kn_ref, bb_ref, gate_ref, hl_ref = refs[18:]
    i = pl.program_id(1)
    tm = x_ref.shape[1]
    g = g_ref[...]
    h = _normmod(x_ref[0], g, sc_ref[0], sh_ref[0])
    if per_row_shift:
        sc8, sh8 = sc_ref[0, 0:8], sh_ref[0, 0:8]
    else:
        sc8, sh8 = sc_ref[0], sh_ref[0]
    h8 = _normmod(p8_ref[0], g, sc8, sh8)
    prev = pltpu.roll(jnp.concatenate([h8, h], axis=0), 1, 0)[8:]
    t_glob = i * tm + lax.broadcasted_iota(jnp.int32, (tm, 1), 0)
    prev = jnp.where(t_glob % seq_len == 0, sh0_ref[0], prev)
    xx = prev - h
    mix = lambda n: (h + xx * mu_ref[n:n + 1, :]).astype(BF16)
    r = jnp.dot(mix(0), wrkv_ref[0], preferred_element_type=F32)
    k = jnp.dot(mix(1), wrkv_ref[1], preferred_element_type=F32)
    v_ref[0] = jnp.dot(mix(2), wrkv_ref[2], preferred_element_type=F32)
    r_ref[0] = r
    wl = w0_ref[...] + _bdot(jnp.tanh(jnp.dot(mix(3), w1_ref[...], preferred_element_type=F32)),
                             w2_ref[...])
    lw_ref[0] = -jnp.exp(-_softplus(-wl) - 0.5)
    a = _sigmoid(a0_ref[...] + _bdot(jnp.dot(mix(4), a1_ref[...], preferred_element_type=F32),
                                     a2_ref[...]))
    gate_ref[0] = _bdot(_sigmoid(jnp.dot(mix(5), g1_ref[...], preferred_element_type=F32)),
                        g2_ref[...])
    kx = k * kk_ref[...]
    ones = _group_matrix(MXU_DIM, C_HEAD_DIM, 1.0)
    kn = kx * lax.rsqrt(_group_reduce(kx * kx, ones) + L2_EPS)
    kn_ref[0] = kn
    bb_ref[0] = kn * a
    k_ref[0] = k * (1.0 + (a - 1.0) * ka_ref[...])
    hl_ref[0, 0] = h[tm - 8:tm]


def _rwkv_in(x, g, sc, sh, shift0, seq_len, P, tm):
    bsz, t, d = x.shape
    per_row = shift0.shape[1] != 1
    consts = [P['mu'], P['w_rkv'], P['w1'], P['w2'], P['a1'], P['a2'], P['g1'], P['g2'],
              P['w0'], P['a0'], P['k_k'], P['k_a']]
    shp = jax.ShapeDtypeStruct(x.shape, F32)
    n_t = t // tm
    outs = pl.pallas_call(
        functools.partial(_rwkv_in_kernel, seq_len=seq_len, per_row_shift=per_row),
        grid=(bsz, n_t),
        in_specs=[_tok_spec(tm, d), _prev8_spec(tm, d), _const_spec(g), _mod_spec(sc, tm),
                  _mod_spec(sh, tm), _mod_spec(shift0, tm)] + [_const_spec(c) for c in consts],
        out_specs=[_tok_spec(tm, d)] * 7 + [pl.BlockSpec((1, 1, 8, d), lambda b, i: (b, i, 0, 0))],
        out_shape=[shp] * 7 + [jax.ShapeDtypeStruct((bsz, n_t, 8, d), F32)],
        compiler_params=_params("parallel", "parallel"),
        name="rwkv_in",
    )(x, x, g, sc, sh, shift0, *consts)
    return outs


def _rwkv_scan_kernel(r_ref, lw_ref, k_ref, v_ref, kn_ref, bb_ref, s0_ref, y_ref, s_out_ref, s_ref,
                      *, n_pairs, n_chunks):
    i = pl.program_id(2)
    c = CHUNK
    n = 2 * c

    @pl.when(i == 0)
    def _():
        s_ref[...] = s0_ref[0]

    lane = lax.broadcasted_iota(jnp.int32, (1, LANES), 1)
    row = lax.broadcasted_iota(jnp.int32, (n, n), 0)
    col = lax.broadcasted_iota(jnp.int32, (n, n), 1)
    same = (row // c) == (col // c)
    incl = same & (row >= col)
    strict = same & (row > col)
    incl_b = incl.astype(BF16)

    def stack(x):
        return jnp.concatenate([jnp.where(lane < C_HEAD_DIM, x, 0.0),
                                jnp.where(lane >= C_HEAD_DIM, x, 0.0)], axis=0)

    sls = [(slice(m * c, (m + 1) * c), slice(p * LANES, (p + 1) * LANES))
           for p in range(n_pairs) for m in range(n_chunks)]
    rng = range(len(sls))
    lw = [stack(lw_ref[0, rs, ls]) for rs, ls in sls]
    lc = [_dot_ex(incl_b, x) for x in lw]
    lc_last = [x[c - 1:c, :] + x[n - 1:n, :] for x in lc]
    p_inv = [jnp.exp(-x) for x in lc]
    p_end = [jnp.exp(lc_last[m] - lc[m]) for m in rng]
    kn_t = [stack(kn_ref[0, sls[m][0], sls[m][1]]) * jnp.exp(lc[m] - lw[m]) for m in rng]
    k_s = [stack(k_ref[0, rs, ls]) for rs, ls in sls]
    b_s = [stack(bb_ref[0, rs, ls]) for rs, ls in sls]
    v_s = [stack(v_ref[0, rs, ls]) for rs, ls in sls]
    k_t = [k_s[m] * p_inv[m] for m in rng]
    b_t = [b_s[m] * p_inv[m] for m in rng]
    r_t = [stack(r_ref[0, sls[m][0], sls[m][1]]) * jnp.exp(lc[m]) for m in rng]
    bk_t = [jnp.concatenate([b_t[m], k_t[m]], axis=0) for m in rng]
    a_k = [_bdot_nt(kn_t[m], bk_t[m]) for m in rng]
    a_r = [_bdot_nt(r_t[m], bk_t[m]) for m in rng]
    a_kb = [jnp.where(strict, x[:, :n], 0.0) for x in a_k]
    a_kk = [jnp.where(strict, x[:, n:], 0.0) for x in a_k]
    a_rb = [jnp.where(incl, x[:, :n], 0.0) for x in a_r]
    a_rk = [jnp.where(incl, x[:, n:], 0.0) for x in a_r]
    t_inv = _tri_inverses(a_kb, 5)
    ku = [_dot3(t_inv[m], jnp.concatenate([kn_t[m], _bdot(a_kk[m], v_s[m])], axis=1))
          for m in rng]
    kt = [x[:, :LANES] for x in ku]
    u0 = [x[:, LANES:] for x in ku]
    k_e = [k_s[m] * p_end[m] for m in rng]
    b_e = [b_s[m] * p_end[m] for m in rng]
    g_mat = [_bdot_tn(kt[m], b_e[m]) for m in rng]
    n0 = [_bdot_tn(v_s[m], k_e[m]) - _bdot_tn(u0[m], b_e[m]) for m in rng]
    rb = [_bdot(a_rb[m], ku[m]) for m in rng]
    r_eff = [r_t[m] - rb[m][:, :LANES] for m in rng]
    y0 = [_bdot(a_rk[m], v_s[m]) - rb[m][:, LANES:] for m in rng]

    states = [s_ref[p] for p in range(n_pairs)]
    for j in range(n_chunks):
        for p in range(n_pairs):
            m = p * n_chunks + j
            rs, ls = sls[m]
            s = states[p]
            y = _bdot_nt(r_eff[m], s) + y0[m]
            y_ref[0, rs, ls] = y[:c] + y[c:]
            states[p] = s * jnp.exp(lc_last[m]) - _bdot(s, g_mat[m]) + n0[m]
    for p in range(n_pairs):
        s_ref[p] = states[p]
        s_out_ref[0, p] = states[p]


def _rwkv_scan(r, lw, k, v, kn, bb, s0_pairs, ct, pairs_per_step):
    bsz, t, d = r.shape
    blk = pl.BlockSpec((1, ct, pairs_per_step * LANES), lambda b, p, i: (b, i, p))
    st = pl.BlockSpec((1, pairs_per_step, LANES, LANES), lambda b, p, i: (b, p, 0, 0))
    return pl.pallas_call(
        functools.partial(_rwkv_scan_kernel, n_pairs=pairs_per_step, n_chunks=ct // CHUNK),
        grid=(bsz, d // (pairs_per_step * LANES), t // ct),
        in_specs=[blk] * 6 + [st],
        out_specs=[blk, st],
        out_shape=[jax.ShapeDtypeStruct(r.shape, F32), jax.ShapeDtypeStruct(s0_pairs.shape, F32)],
        scratch_shapes=[pltpu.VMEM((pairs_per_step, LANES, LANES), F32)],
        compiler_params=_params("parallel", "parallel", "arbitrary"),
        name="rwkv_scan",
    )(r, lw, k, v, kn, bb, s0_pairs)


def _chain_split(t_pad, n_heads):
    heads = min(n_heads, HEADS_PER_STEP)
    ct = min(t_pad, CHAINS_PER_STEP // heads * CHUNK)
    heads = max(heads, min(n_heads, CHAINS_PER_STEP * CHUNK // ct))
    return ct, heads


def _pad_time(a, t_pad):
    return jnp.pad(a, ((0, 0), (0, t_pad - a.shape[1]), (0, 0)))


def _pairs_from_heads(s):
    bsz, n_h, n, _ = s.shape
    sp = s.reshape(bsz, n_h // 2, 2, n, n)
    z = jnp.zeros_like(sp[:, :, 0])
    top = jnp.concatenate([sp[:, :, 0], z], axis=-1)
    bot = jnp.concatenate([z, sp[:, :, 1]], axis=-1)
    return jnp.concatenate([top, bot], axis=-2)


def _heads_from_pairs(sp):
    bsz, n_p, n2, _ = sp.shape
    n = n2 // 2
    return jnp.stack([sp[:, :, :n, :n], sp[:, :, n:, n:]], axis=2).reshape(bsz, 2 * n_p, n, n)


def _run_trunk(x, mods, W, tm, seq_len, cache, state):
    bsz, t, d = x.shape
    n_seq = bsz * t // seq_len
    depth = len(mods)
    kv_all, bconv, bssm, cshift, cwkv = None, [], [], [], []
    n_a = W['a_w_qkv'].shape[0]
    for l in range(depth):
        kind, j = l % 3, l // 3
        sh1, sc1, gt1, sh2, sc2, gt2 = mods[l]
        g_mix = W['ln_mix'][l].reshape(1, d)
        if kind == 0:
            q, k_all, v_all = _qkv(x, g_mix, sc1, sh1, W['a_w_qkv'], j, W['a_q_norm'][j],
                                   W['a_k_norm'][j], kv_all, tm)
            kv_all = (k_all, v_all)
            if cache is None:
                o = _sb_prompt(q, k_all, v_all, j, W['a_logit_bias'][j],
                               qb=SB_QUERY_BLOCK, kb=SB_KEY_BLOCK)
            else:
                cache_k, cache_v, page_table = cache
                o = _sb_decode(q.reshape(n_seq, seq_len, d), k_all.reshape(n_a, n_seq, seq_len, d),
                               v_all.reshape(n_a, n_seq, seq_len, d), j, cache_k, cache_v, j,
                               page_table, W['a_logit_bias'][j],
                               pages_per_step=min(8, page_table.shape[1])).reshape(bsz, t, d)
            x = _proj_res(o, W['a_w_o'][j], x, gt1, tm)
        elif kind == 1:
            n_conv = 3 * d
            xqkv, z, ba = _gdn_in(x, g_mix, sc1, sh1, W['b_w_in'][j], n_conv, tm)
            xq_seq = xqkv.reshape(n_seq, seq_len, n_conv)
            if state is None:
                xs = None
                bconv.append(xq_seq[:, seq_len - (B_CONV - 1):])
                s0 = jnp.zeros((n_seq, d // B_HEAD_DIM, B_HEAD_DIM, B_HEAD_DIM), F32)
            else:
                xc = jnp.concatenate([state['b_conv'][j], xq_seq], axis=1)
                xs = jnp.stack([xc[:, B_CONV - 1 - s:B_CONV - 1 - s + seq_len].reshape(bsz, t, n_conv)
                                for s in range(1, B_CONV)])
                bconv.append(xc[:, seq_len:])
                s0 = state['b_ssm'][j]
            q, k, kb, vb, g = _gdn_conv(xqkv, xs, W['b_conv'][j], ba, W['b_a_log'][j],
                                        W['b_dt_bias'][j], d, min(tm, 256))
            t_pad = -(-seq_len // CHUNK) * CHUNK
            seqs = [_pad_time(a.reshape(n_seq, seq_len, d), t_pad) for a in (q, k, kb, vb, g)]
            ct, heads = _chain_split(t_pad, d // B_HEAD_DIM)
            o, s_new = _gdn_chunk(*seqs, s0, ct=ct, heads_per_step=heads)
            bssm.append(s_new)
            o = o[:, :seq_len].reshape(bsz, t, d)
            x = _gdn_out(o, z, W['b_o_norm'][j], W['b_w_o'][j], x, gt1, tm)
        else:
            if state is None:
                shift0 = jnp.zeros((bsz, 1, d), F32)
                s0 = jnp.zeros((n_seq, d // C_HEAD_DIM, C_HEAD_DIM, C_HEAD_DIM), F32)
            else:
                shift0 = jnp.repeat(state['c_shift'][j], seq_len, axis=0).reshape(bsz, t, d)
                s0 = state['c_wkv'][j]
            tm_c = min(tm, 256)
            r, lw, k, v, kn, bb, gate, h_last = _rwkv_in(x, g_mix, sc1, sh1, shift0, seq_len,
                                                        W['c'][j], tm_c)
            if state is None:
                cshift.append(h_last[:, -1, 7])
            else:
                assert t == tm_c and seq_len == 8
                cshift.append(_last_rows(x, g_mix, sc1, sh1, seq_len))
            t_pad = -(-seq_len // CHUNK) * CHUNK
            seqs = [_pad_time(a.reshape(n_seq, seq_len, d), t_pad) for a in (r, lw, k, v, kn, bb)]
            ct, pairs = _chain_split(t_pad, d // LANES)
            y, s_new = _rwkv_scan(*seqs, _pairs_from_heads(s0), ct=ct, pairs_per_step=pairs)
            cwkv.append(_heads_from_pairs(s_new))
            y = y[:, :seq_len].reshape(bsz, t, d)
            x = _rwkv_out(y, r, k, v, gate, W['c_r_k'][j], W['c_ln_g'][j], W['c_ln_b'][j],
                          W['c_w_o'][j], x, gt1, tm)
        x = _mlp(x, W['ln_mlp'][l].reshape(1, d), sc2, sh2, gt2, W['w_up'], W['w_down'], l,
                 MLP_TOKEN_TILE if t % MLP_TOKEN_TILE == 0 else tm)
    heads = (n_a, n_seq, seq_len, d // A_HEAD_DIM, A_HEAD_DIM)
    return x, kv_all[0].reshape(heads), kv_all[1].reshape(heads), bconv, bssm, cshift, cwkv


def _hmod_kernel(x_ref, g_ref, sc_ref, sh_ref, o_ref):
    o_ref[0] = _normmod(x_ref[0], g_ref[...], sc_ref[0], sh_ref[0])


def _last_rows(x, g, sc, sh, seq_len):
    bsz, t, d = x.shape
    n_seq = bsz * t // seq_len
    pick = lambda a: a.reshape(n_seq, seq_len, d)[:, seq_len - 1].reshape(1, n_seq, d)
    xs, scs, shs = pick(x), pick(sc), pick(sh)
    return pl.pallas_call(
        _hmod_kernel,
        grid=(1, 1),
        in_specs=[_tok_spec(n_seq, d), _const_spec(g), _tok_spec(n_seq, d), _tok_spec(n_seq, d)],
        out_specs=_tok_spec(n_seq, d),
        out_shape=jax.ShapeDtypeStruct((1, n_seq, d), F32),
        compiler_params=_params("parallel", "parallel"),
        name="last_rows",
    )(xs, g, scs, shs)[0]


def kernel(x_prompt, x_sample, c_prompt, c_sample, cache_k, cache_v, page_table, state_b_conv, state_b_ssm, state_c_shift, state_c_wkv, ln_mix, ln_mlp, w_ada, b_ada, w_up, w_down, a_w_qkv, a_q_norm, a_k_norm, a_logit_bias, a_w_o, b_w_in, b_conv, b_a_log, b_dt_bias, b_o_norm, b_w_o, c_mu, c_w_rkv, c_w0, c_w1, c_w2, c_a0, c_a1, c_a2, c_g1, c_g2, c_k_k, c_k_a, c_r_k, c_ln_g, c_ln_b, c_w_o):
    bp, t_p, d = x_prompt.shape
    bs, t_s, _ = x_sample.shape
    depth = ln_mix.shape[0]
    n_c = c_mu.shape[0]

    def lora_cols(w):
        return jnp.pad(w, ((0, 0), (0, 0), (0, LANES - w.shape[2]))).astype(BF16)

    def lora_rows(w):
        return jnp.pad(w, ((0, 0), (0, LANES - w.shape[1]), (0, 0))).astype(BF16)

    b_in = b_w_in.shape[2]
    b_in_pad = -(-b_in // LANES) * LANES
    W = {
        'ln_mix': ln_mix, 'ln_mlp': ln_mlp,
        'w_up': w_up.astype(BF16), 'w_down': w_down.astype(BF16),
        'a_w_qkv': a_w_qkv.astype(BF16), 'a_q_norm': a_q_norm, 'a_k_norm': a_k_norm,
        'a_logit_bias': a_logit_bias, 'a_w_o': a_w_o.astype(BF16),
        'b_w_in': jnp.pad(b_w_in, ((0, 0), (0, 0), (0, b_in_pad - b_in))).astype(BF16),
        'b_conv': b_conv, 'b_a_log': b_a_log, 'b_dt_bias': b_dt_bias, 'b_o_norm': b_o_norm,
        'b_w_o': b_w_o.astype(BF16),
        'c_r_k': c_r_k.reshape(n_c, d), 'c_ln_g': c_ln_g, 'c_ln_b': c_ln_b, 'c_w_o': c_w_o.astype(BF16),
    }
    w1, w2 = lora_cols(c_w1), lora_rows(c_w2)
    a1, a2 = lora_cols(c_a1), lora_rows(c_a2)
    g1, g2 = c_g1.astype(BF16), c_g2.astype(BF16)
    rkv = c_w_rkv.astype(BF16)
    W['c'] = [{'mu': c_mu[j], 'w_rkv': rkv[j], 'w1': w1[j], 'w2': w2[j], 'a1': a1[j], 'a2': a2[j],
               'g1': g1[j], 'g2': g2[j], 'w0': c_w0[j].reshape(1, d), 'a0': c_a0[j].reshape(1, d),
               'k_k': c_k_k[j].reshape(1, d), 'k_a': c_k_a[j].reshape(1, d)} for j in range(n_c)]

    mod_all = _ada(jnp.concatenate([c_prompt, c_sample], axis=0), w_ada, b_ada)
    mod_p = mod_all[:, :, :bp].reshape(depth, 6, bp, 1, d)
    mod_s = jnp.repeat(mod_all[:, :, bp:], t_s, axis=2).reshape(depth, 6, 1, bs * t_s, d)
    mods_p = [[mod_p[l, n] for n in range(6)] for l in range(depth)]
    mods_s = [[mod_s[l, n] for n in range(6)] for l in range(depth)]

    y_p, k_p, v_p, bconv_p, bssm_p, cshift_p, cwkv_p = _run_trunk(
        x_prompt, mods_p, W, 512, t_p, None, None)
    state = {'b_conv': state_b_conv, 'b_ssm': state_b_ssm, 'c_shift': state_c_shift,
             'c_wkv': state_c_wkv}
    y_s, k_s, v_s, bconv_s, bssm_s, cshift_s, cwkv_s = _run_trunk(
        x_sample.reshape(1, bs * t_s, d), mods_s, W, bs * t_s, t_s,
        (cache_k, cache_v, page_table), state)
    st = jnp.stack
    return (y_p, y_s.reshape(bs, t_s, d), k_p, v_p, k_s, v_s, st(bconv_p), st(bssm_p),
            st(bconv_s), st(bssm_s), st(cshift_p), st(cwkv_p), st(cshift_s), st(cwkv_s))
```

```python
import functools

import jax
import jax.numpy as jnp
from jax import lax
from jax.experimental import pallas as pl
from jax.experimental.pallas import tpu as pltpu

F32 = jnp.float32
BF16 = jnp.bfloat16

LANES = 128
MXU_DIM = 256
VMEM_LIMIT_BYTES = 56 * 1024 * 1024

NORM_EPS = 1e-6
L2_EPS = 1e-6
GN_EPS = 64e-5
LOG2E = 1.4426950408889634
A_HEAD_DIM = 64
B_HEAD_DIM = 128
C_HEAD_DIM = 64
B_CONV = 4
CHUNK = 64
CHAINS_PER_STEP = 16
HEADS_PER_STEP = 4
SB_QUERY_BLOCK = 512
SB_KEY_BLOCK = MXU_DIM
MLP_TOKEN_TILE = 1024


def _params(*sem):
    return pltpu.CompilerParams(dimension_semantics=sem, vmem_limit_bytes=VMEM_LIMIT_BYTES)


def _bdot(a, b):
    return jnp.dot(a.astype(BF16), b.astype(BF16), preferred_element_type=F32)


def _bdot_nt(a, b):
    return lax.dot_general(a.astype(BF16), b.astype(BF16), (((1,), (1,)), ((), ())),
                           preferred_element_type=F32)


def _bdot_tn(a, b):
    return lax.dot_general(a.astype(BF16), b.astype(BF16), (((0,), (0,)), ((), ())),
                           preferred_element_type=F32)


def _split(x):
    hi = x.astype(BF16)
    lo = (x - hi.astype(F32)).astype(BF16)
    return hi, lo


def _sum_dots(lhs, rhs):
    return jnp.dot(jnp.concatenate(lhs, axis=1), jnp.concatenate(rhs, axis=0),
                   preferred_element_type=F32)


def _dot_xe(x, e):
    hi, lo = _split(x)
    return _sum_dots([hi, lo], [e, e])


def _dot_ex(e, x):
    hi, lo = _split(x)
    return _sum_dots([e, e], [hi, lo])


def _dot3_parts(a_parts, b_parts):
    ah, al = a_parts
    bh, bl = b_parts
    return _sum_dots([ah, ah, al], [bh, bl, bh])


def _dot3(a, b):
    return _dot3_parts(_split(a), _split(b))


def _sigmoid(x):
    return 1.0 / (1.0 + jnp.exp(-x))


def _silu(x):
    return x * _sigmoid(x)


def _softplus(x):
    return jnp.maximum(x, 0.0) + jnp.log(1.0 + jnp.exp(-jnp.abs(x)))


def _neg_abs(x):
    bits = lax.bitcast_convert_type(x, jnp.uint32) | jnp.uint32(0x80000000)
    return lax.bitcast_convert_type(bits, F32)


def _sb_suffix(z, suffix2, mask):
    sp = jnp.maximum(z, 0.0) + jnp.log(1.0 + jnp.exp2(_neg_abs(z))) * LOG2E
    if mask is not None:
        sp = jnp.where(mask, sp, 0.0)
    hi, lo = _split(sp)
    return jnp.dot(jnp.concatenate([hi, lo], axis=1), suffix2, preferred_element_type=F32)


def _sb_local(z, s_incl, mask):
    w = jnp.exp2(z - s_incl)
    if mask is not None:
        w = jnp.where(mask, w, 0.0)
    return w.astype(BF16)


def _sb_weights(z, suffix2, mask):
    s_incl = _sb_suffix(z, suffix2, mask)
    return _sb_local(z, s_incl, mask), s_incl[:, 0:1]


def _normmod(x, g, sc, sh):
    ms = jnp.mean(x * x, axis=-1, keepdims=True)
    return x * lax.rsqrt(ms + NORM_EPS) * g * (1.0 + sc) + sh


def _group_matrix(n, group, value):
    r = lax.broadcasted_iota(jnp.int32, (n, n), 0) // group
    c = lax.broadcasted_iota(jnp.int32, (n, n), 1) // group
    return jnp.where(r == c, value, 0.0).astype(BF16)


def _group_reduce(x, gmat, split=False):
    n = gmat.shape[0]
    dot = _dot_xe if split else _bdot
    parts = [dot(x[:, s:s + n], gmat) for s in range(0, x.shape[1], n)]
    return parts[0] if len(parts) == 1 else jnp.concatenate(parts, axis=1)


def _tri_inverses(mats, n_steps):
    n = mats[0].shape[0]
    eye = (lax.broadcasted_iota(jnp.int32, (n, n), 0)
           == lax.broadcasted_iota(jnp.int32, (n, n), 1)).astype(F32)
    ps = [-a for a in mats]
    ts = [eye + p for p in ps]
    parts = [_split(p) for p in ps]
    for _ in range(n_steps):
        parts = [_split(_dot3_parts(s, s)) for s in parts]
        ts = [t + _dot3_parts(_split(t), s) for t, s in zip(ts, parts)]
    return ts


def _ada_kernel(c_ref, w_ref, b_ref, o_ref):
    ca = _silu(c_ref[...])
    o_ref[0, 0] = _bdot(ca, w_ref[0]) + b_ref[0]


def _ada(c_all, w_ada, b_ada):
    depth, d, n = w_ada.shape
    rows = c_all.shape[0]
    return pl.pallas_call(
        _ada_kernel,
        grid=(depth, n // d),
        in_specs=[pl.BlockSpec((rows, d), lambda l, j: (0, 0)),
                  pl.BlockSpec((1, d, d), lambda l, j: (l, 0, j)),
                  pl.BlockSpec((1, 1, d), lambda l, j: (l, 0, j))],
        out_specs=pl.BlockSpec((1, 1, rows, d), lambda l, j: (l, j, 0, 0)),
        out_shape=jax.ShapeDtypeStruct((depth, n // d, rows, d), F32),
        compiler_params=_params("parallel", "parallel"),
        name="ada",
    )(c_all, w_ada, b_ada.reshape(depth, 1, n))


def _tok_spec(tm, width):
    return pl.BlockSpec((1, tm, width), lambda b, i: (b, i, 0))


def _mod_spec(mod, tm):
    if mod.shape[1] == 1:
        return pl.BlockSpec((1, 1, mod.shape[2]), lambda b, i: (b, 0, 0))
    return pl.BlockSpec((1, tm, mod.shape[2]), lambda b, i: (b, i, 0))


def _const_spec(arr):
    nd = arr.ndim
    return pl.BlockSpec(arr.shape, lambda b, i: (0,) * nd, pipeline_mode=pl.Buffered(1))


def _prev8_spec(tm, width):
    blocks = tm // 8
    return pl.BlockSpec((1, 8, width), lambda b, i: (b, jnp.maximum(i * blocks - 1, 0), 0))


def _mlp_kernel(x_ref, g_ref, sc_ref, sh_ref, gt_ref, wu_ref, wd_ref, o_ref, h_ref, acc_ref):
    f = pl.program_id(2)

    @pl.when(f == 0)
    def _():
        h_ref[...] = _normmod(x_ref[0], g_ref[...], sc_ref[0], sh_ref[0]).astype(BF16)
        acc_ref[...] = jnp.zeros_like(acc_ref)

    u = jnp.maximum(jnp.dot(h_ref[...], wu_ref[0], preferred_element_type=F32), 0.0)
    acc_ref[...] += jnp.dot((u * u).astype(BF16), wd_ref[0], preferred_element_type=F32)

    @pl.when(f == pl.num_programs(2) - 1)
    def _():
        o_ref[0] = x_ref[0] + gt_ref[0] * acc_ref[...]


def _mlp(x, g, sc, sh, gt, w_up, w_down, layer, tm):
    bsz, t, d = x.shape
    ff = w_up.shape[2]
    tf = 1024

    def tok(b, i, f):
        return (b, i, 0)

    def mod_spec(m):
        if m.shape[1] == 1:
            return pl.BlockSpec((1, 1, d), lambda b, i, f: (b, 0, 0))
        return pl.BlockSpec((1, tm, d), tok)

    return pl.pallas_call(
        _mlp_kernel,
        grid=(bsz, t // tm, ff // tf),
        in_specs=[pl.BlockSpec((1, tm, d), tok),
                  pl.BlockSpec((1, d), lambda b, i, f: (0, 0)),
                  mod_spec(sc), mod_spec(sh), mod_spec(gt),
                  pl.BlockSpec((1, d, tf), lambda b, i, f: (layer, 0, f)),
                  pl.BlockSpec((1, tf, d), lambda b, i, f: (layer, f, 0))],
        out_specs=pl.BlockSpec((1, tm, d), tok),
        out_shape=jax.ShapeDtypeStruct(x.shape, F32),
        scratch_shapes=[pltpu.VMEM((tm, d), BF16), pltpu.VMEM((tm, d), F32)],
        compiler_params=_params("parallel", "parallel", "arbitrary"),
        name="mlp",
    )(x, g, sc, sh, gt, w_up, w_down)


def _qkv_kernel(x_ref, g_ref, sc_ref, sh_ref, w_ref, qg_ref, kg_ref, *rest):
    q_ref, k_ref, v_ref = rest[-3:]
    d = x_ref.shape[2]
    h = _normmod(x_ref[0], g_ref[...], sc_ref[0], sh_ref[0]).astype(BF16)
    gmat = _group_matrix(MXU_DIM, A_HEAD_DIM, 1.0 / A_HEAD_DIM)
    q_scale = A_HEAD_DIM ** -0.5 * LOG2E
    for s in range(0, d, MXU_DIM):
        sl = slice(s, s + MXU_DIM)
        q = jnp.dot(h, w_ref[0, :, s:s + MXU_DIM], preferred_element_type=F32)
        q_ref[0, :, sl] = q * lax.rsqrt(_bdot(q * q, gmat) + NORM_EPS) * qg_ref[...] * q_scale
        k = jnp.dot(h, w_ref[0, :, d + s:d + s + MXU_DIM], preferred_element_type=F32)
        k = k * lax.rsqrt(_bdot(k * k, gmat) + NORM_EPS) * kg_ref[...]
        v = jnp.dot(h, w_ref[0, :, 2 * d + s:2 * d + s + MXU_DIM], preferred_element_type=F32)
        for slab in range(k_ref.shape[0]):
            k_ref[slab, 0, :, sl] = k
            v_ref[slab, 0, :, sl] = v


def _qkv(x, g, sc, sh, w_all, layer, q_gain, k_gain, kv_all, tm):
    bsz, t, d = x.shape
    n_layers = w_all.shape[0]
    qg = jnp.tile(q_gain, MXU_DIM // A_HEAD_DIM).reshape(1, MXU_DIM)
    kg = jnp.tile(k_gain, MXU_DIM // A_HEAD_DIM).reshape(1, MXU_DIM)
    kv_shape = jax.ShapeDtypeStruct((n_layers, bsz, t, d), F32)
    if kv_all is None:
        kv_spec = pl.BlockSpec((n_layers, 1, tm, d), lambda b, i: (0, b, i, 0))
    else:
        kv_spec = pl.BlockSpec((1, 1, tm, d), lambda b, i: (layer, b, i, 0))
    w_spec = pl.BlockSpec((1,) + w_all.shape[1:], lambda b, i: (layer, 0, 0),
                          pipeline_mode=pl.Buffered(1))
    aliased = [] if kv_all is None else list(kv_all)
    n_in = 7
    return pl.pallas_call(
        _qkv_kernel,
        grid=(bsz, t // tm),
        in_specs=[_tok_spec(tm, d), _const_spec(g), _mod_spec(sc, tm), _mod_spec(sh, tm),
                  w_spec, _const_spec(qg), _const_spec(kg)]
                 + [pl.BlockSpec(memory_space=pl.ANY)] * len(aliased),
        out_specs=[_tok_spec(tm, d), kv_spec, kv_spec],
        out_shape=[jax.ShapeDtypeStruct(x.shape, F32), kv_shape, kv_shape],
        input_output_aliases={n_in + n: 1 + n for n in range(len(aliased))},
        compiler_params=_params("parallel", "parallel"),
        name="qkv",
    )(x, g, sc, sh, w_all, qg, kg, *aliased)


def _sb_prompt_kernel(bias_ref, q_ref, k_ref, v_ref, o_ref, z_ref, s_ref, acc_ref, c_ref, *, qb, kb):
    hp = pl.program_id(1)
    i = pl.program_id(2)
    q = q_ref[0]
    lane = lax.broadcasted_iota(jnp.int32, (1, LANES), 1)
    row = lax.broadcasted_iota(jnp.int32, (kb, kb), 0)
    col = lax.broadcasted_iota(jnp.int32, (kb, kb), 1)
    suffix = (row >= col).astype(BF16)
    suffix2 = jnp.concatenate([suffix, suffix], axis=0)
    qm = [jnp.where(lane // A_HEAD_DIM == e, q, 0.0).astype(BF16) for e in range(2)]
    bias = [bias_ref[2 * hp + e] * LOG2E for e in range(2)]
    per_q = qb // kb
    n_blocks = per_q * (i + 1)

    def rows(t):
        return pl.ds(pl.multiple_of(jnp.maximum(n_blocks - 1 - t, 0) * kb, kb), kb)

    def first_row(t):
        return (per_q - 1 - t) * kb if isinstance(t, int) and t < per_q else 0

    def mask(t):
        if not isinstance(t, int) or t >= per_q:
            return None
        shape = (qb - first_row(t), kb)
        return lax.broadcasted_iota(jnp.int32, shape, 1) < lax.broadcasted_iota(jnp.int32, shape, 0)

    def logits(t, slot, e):
        r0 = first_row(t)
        z_ref[slot, e, r0:, :] = _bdot_nt(qm[e][r0:], k_ref[0, 0, rows(t), :]) + bias[e]

    def suffix_sums(t, slot, e):
        r0 = first_row(t)
        s_ref[slot, e, r0:, :] = _sb_suffix(z_ref[slot, e, r0:, :], suffix2, mask(t))

    def consume(t, slot, e):
        r0 = first_row(t)
        s_incl = s_ref[slot, e, r0:, :]
        p = _bdot(_sb_local(z_ref[slot, e, r0:, :], s_incl, mask(t)), v_ref[0, 0, rows(t), :])
        acc_ref[e, r0:, :] += jnp.exp2(-c_ref[e, r0:, :]) * p
        c_ref[e, r0:, :] += s_incl[:, 0:1]

    def trip(t, slot):
        for e in range(2):
            consume(t, slot, e)
            logits(t + 2, slot, e)
            suffix_sums(t + 1, 1 - slot, e)

    acc_ref[...] = jnp.zeros_like(acc_ref)
    c_ref[...] = jnp.zeros_like(c_ref)
    for e in range(2):
        logits(0, 0, e)
        logits(1, 1, e)
        suffix_sums(0, 0, e)
    for t in range(per_q):
        trip(t, t % 2)

    def pair(u, _):
        trip(2 * u + per_q, 0)
        trip(2 * u + per_q + 1, 1)
        return 0

    lax.fori_loop(0, (per_q // 2) * i, pair, 0)
    o_ref[0] = jnp.where(lane < A_HEAD_DIM, acc_ref[0], acc_ref[1])


def _sb_prompt(q, k_all, v_all, layer, logit_bias, qb, kb):
    bsz, t, d = q.shape
    n_pairs = d // LANES
    assert qb % (2 * kb) == 0 and t % qb == 0
    kv_spec = pl.BlockSpec((1, 1, t, LANES), lambda b, p, i, bias: (layer, b, 0, p))
    grid_spec = pltpu.PrefetchScalarGridSpec(
        num_scalar_prefetch=1,
        grid=(bsz, n_pairs, t // qb),
        in_specs=[pl.BlockSpec((1, qb, LANES), lambda b, p, i, bias: (b, i, p)), kv_spec, kv_spec],
        out_specs=pl.BlockSpec((1, qb, LANES), lambda b, p, i, bias: (b, i, p)),
        scratch_shapes=[pltpu.VMEM((2, 2, qb, kb), F32), pltpu.VMEM((2, 2, qb, kb), F32),
                        pltpu.VMEM((2, qb, LANES), F32), pltpu.VMEM((2, qb, 1), F32)],
    )
    return pl.pallas_call(
        functools.partial(_sb_prompt_kernel, qb=qb, kb=kb),
        grid_spec=grid_spec,
        out_shape=jax.ShapeDtypeStruct(q.shape, F32),
        compiler_params=_params("parallel", "parallel", "arbitrary"),
        name="sb_prompt",
    )(logit_bias, q, k_all, v_all)


def _sb_decode_kernel(pt_ref, q_ref, bias_ref, kn_ref, vn_ref, *rest, pages_per_step, page, t_new):
    k_refs = rest[:pages_per_step]
    v_refs = rest[pages_per_step:2 * pages_per_step]
    o_ref = rest[2 * pages_per_step]
    acc_ref, c_ref, qbd_ref = rest[2 * pages_per_step + 1:]
    s = pl.program_id(1)
    d = q_ref.shape[2]
    n_heads = d // A_HEAD_DIM
    n_rows = n_heads * t_new
    r_head = lax.broadcasted_iota(jnp.int32, (n_rows, d), 0) // t_new
    c_head = lax.broadcasted_iota(jnp.int32, (n_rows, d), 1) // A_HEAD_DIM

    @pl.when(s == 0)
    def _():
        tiled = jnp.concatenate([q_ref[0]] * n_heads, axis=0)
        qbd_ref[...] = jnp.where(r_head == c_head, tiled, 0.0).astype(BF16)

    qbd = qbd_ref[...]
    bias = bias_ref[...]
    row = lax.broadcasted_iota(jnp.int32, (page, page), 0)
    col = lax.broadcasted_iota(jnp.int32, (page, page), 1)
    suffix = (row >= col).astype(BF16)
    suffix2 = jnp.concatenate([suffix, suffix], axis=0)

    def accumulate(p, tot):
        acc_ref[...] += jnp.exp2(-c_ref[...]) * p
        c_ref[...] += tot

    @pl.when(s == 0)
    def _():
        acc_ref[...] = jnp.zeros_like(acc_ref)
        c_ref[...] = jnp.zeros_like(c_ref)
        pad = jnp.zeros((page - t_new, d), F32)
        k_pad = jnp.concatenate([kn_ref[0, 0], pad], axis=0).astype(BF16)
        v_pad = jnp.concatenate([vn_ref[0, 0], pad], axis=0).astype(BF16)
        q_t = lax.broadcasted_iota(jnp.int32, (n_rows, page), 0) % t_new
        key_i = lax.broadcasted_iota(jnp.int32, (n_rows, page), 1)
        w, tot = _sb_weights(_bdot_nt(qbd, k_pad) + bias, suffix2, key_i < q_t)
        accumulate(jnp.dot(w, v_pad, preferred_element_type=F32), tot)

    zs = [jnp.dot(qbd, k_refs[r][0, 0].astype(BF16), preferred_element_type=F32) + bias
          for r in range(pages_per_step)]
    ws = [_sb_weights(z, suffix2, None) for z in zs]
    ps = [_bdot_nt(w, v_refs[r][0, 0]) for r, (w, _) in enumerate(ws)]
    for p, (_, tot) in zip(ps, ws):
        accumulate(p, tot)

    @pl.when(s == pl.num_programs(1) - 1)
    def _():
        diag = jnp.where(r_head == c_head, acc_ref[...], 0.0)
        out = diag[0:t_new]
        for h in range(1, n_heads):
            out = out + diag[h * t_new:(h + 1) * t_new]
        o_ref[0] = out


def _sb_decode(q, k_all, v_all, new_layer, cache_k, cache_v, layer, page_table, logit_bias,
               pages_per_step):
    bsz, t_new, d = q.shape
    n_heads = d // A_HEAD_DIM
    n_layers, n_pool, page = cache_k.shape[:3]
    n_pages = page_table.shape[1]
    n_rows = n_heads * t_new
    ck = cache_k.transpose(0, 1, 3, 4, 2).reshape(n_layers, n_pool, d, page)
    cv = cache_v.transpose(0, 1, 3, 4, 2).reshape(n_layers, n_pool, d, page)
    bias_rows = jnp.broadcast_to(jnp.repeat(logit_bias * LOG2E, t_new)[:, None], (n_rows, page))
    n_steps = n_pages // pages_per_step

    def page_map(r):
        def index(b, s, pt):
            return (layer, pt[b, n_pages - 1 - (s * pages_per_step + r)], 0, 0)
        return index

    kv_specs = [pl.BlockSpec((1, 1, d, page), page_map(r)) for r in range(pages_per_step)]
    grid_spec = pltpu.PrefetchScalarGridSpec(
        num_scalar_prefetch=1,
        grid=(bsz, n_steps),
        in_specs=[pl.BlockSpec((1, t_new, d), lambda b, s, pt: (b, 0, 0)),
                  pl.BlockSpec((n_rows, page), lambda b, s, pt: (0, 0)),
                  pl.BlockSpec((1, 1, t_new, d), lambda b, s, pt: (new_layer, b, 0, 0)),
                  pl.BlockSpec((1, 1, t_new, d), lambda b, s, pt: (new_layer, b, 0, 0))]
                 + kv_specs + kv_specs,
        out_specs=pl.BlockSpec((1, t_new, d), lambda b, s, pt: (b, 0, 0)),
        scratch_shapes=[pltpu.VMEM((n_rows, d), F32), pltpu.VMEM((n_rows, 1), F32),
                        pltpu.VMEM((n_rows, d), BF16)],
    )
    return pl.pallas_call(
        functools.partial(_sb_decode_kernel, pages_per_step=pages_per_step, page=page, t_new=t_new),
        grid_spec=grid_spec,
        out_shape=jax.ShapeDtypeStruct(q.shape, F32),
        compiler_params=_params("parallel", "arbitrary"),
        name="sb_decode",
    )(page_table, q, bias_rows, k_all, v_all, *([ck] * pages_per_step), *([cv] * pages_per_step))


def _proj_res_kernel(a_ref, w_ref, x_ref, gt_ref, o_ref):
    o_ref[0] = x_ref[0] + gt_ref[0] * _bdot(a_ref[0], w_ref[...])


def _proj_res(a, w, x, gt, tm):
    bsz, t, d = x.shape
    return pl.pallas_call(
        _proj_res_kernel,
        grid=(bsz, t // tm),
        in_specs=[_tok_spec(tm, d), _const_spec(w), _tok_spec(tm, d), _mod_spec(gt, tm)],
        out_specs=_tok_spec(tm, d),
        out_shape=jax.ShapeDtypeStruct(x.shape, F32),
        compiler_params=_params("parallel", "parallel"),
        name="proj_res",
    )(a, w, x, gt)


def _gdn_out_kernel(o_ref_in, z_ref, on_ref, w_ref, x_ref, gt_ref, o_ref):
    o = o_ref_in[0]
    gmat = _group_matrix(MXU_DIM, B_HEAD_DIM, 1.0 / B_HEAD_DIM)
    ms = _group_reduce(o * o, gmat)
    a = o * lax.rsqrt(ms + NORM_EPS) * on_ref[...] * _silu(z_ref[0])
    o_ref[0] = x_ref[0] + gt_ref[0] * _bdot(a, w_ref[...])


def _gdn_out(o, z, o_norm, w, x, gt, tm):
    bsz, t, d = x.shape
    on = jnp.tile(o_norm, d // B_HEAD_DIM).reshape(1, d)
    return pl.pallas_call(
        _gdn_out_kernel,
        grid=(bsz, t // tm),
        in_specs=[_tok_spec(tm, d), _tok_spec(tm, d), _const_spec(on), _const_spec(w),
                  _tok_spec(tm, d), _mod_spec(gt, tm)],
        out_specs=_tok_spec(tm, d),
        out_shape=jax.ShapeDtypeStruct(x.shape, F32),
        compiler_params=_params("parallel", "parallel"),
        name="gdn_out",
    )(o, z, on, w, x, gt)


def _rwkv_out_kernel(y_ref, r_ref, k_ref, v_ref, gate_ref, rk_ref, lg_ref, lb_ref, w_ref,
                     x_ref, gt_ref, o_ref):
    y = y_ref[0]
    mean_mat = _group_matrix(MXU_DIM, C_HEAD_DIM, 1.0 / C_HEAD_DIM)
    sum_mat = _group_matrix(MXU_DIM, C_HEAD_DIM, 1.0)
    yc = y - _group_reduce(y, mean_mat, split=True)
    var = _group_reduce(yc * yc, mean_mat)
    yn = yc * lax.rsqrt(var + GN_EPS) * lg_ref[...] + lb_ref[...]
    bonus = _group_reduce(r_ref[0] * k_ref[0] * rk_ref[...], sum_mat) * v_ref[0]
    a = (yn + bonus) * gate_ref[0]
    o_ref[0] = x_ref[0] + gt_ref[0] * _bdot(a, w_ref[...])


def _rwkv_out(y, r, k, v, gate, r_k, ln_g, ln_b, w, x, gt, tm):
    bsz, t, d = x.shape
    rk = r_k.reshape(1, d)
    lg = ln_g.reshape(1, d)
    lb = ln_b.reshape(1, d)
    tok = _tok_spec(tm, d)
    return pl.pallas_call(
        _rwkv_out_kernel,
        grid=(bsz, t // tm),
        in_specs=[tok, tok, tok, tok, tok, _const_spec(rk), _const_spec(lg), _const_spec(lb),
                  _const_spec(w), tok, _mod_spec(gt, tm)],
        out_specs=tok,
        out_shape=jax.ShapeDtypeStruct(x.shape, F32),
        compiler_params=_params("parallel", "parallel"),
        name="rwkv_out",
    )(y, r, k, v, gate, rk, lg, lb, w, x, gt)


def _gdn_in_kernel(x_ref, g_ref, sc_ref, sh_ref, w_ref, xqkv_ref, z_ref, ba_ref):
    n_conv = xqkv_ref.shape[2]
    n_z = z_ref.shape[2]
    h = _normmod(x_ref[0], g_ref[...], sc_ref[0], sh_ref[0]).astype(BF16)
    for s in range(0, n_conv, MXU_DIM):
        xqkv_ref[0, :, s:s + MXU_DIM] = jnp.dot(h, w_ref[:, s:s + MXU_DIM],
                                                preferred_element_type=F32)
    for s in range(0, n_z, MXU_DIM):
        z_ref[0, :, s:s + MXU_DIM] = jnp.dot(h, w_ref[:, n_conv + s:n_conv + s + MXU_DIM],
                                             preferred_element_type=F32)
    ba_ref[0] = jnp.dot(h, w_ref[:, n_conv + n_z:], preferred_element_type=F32)


def _gdn_in(x, g, sc, sh, w_pad, n_conv, tm):
    bsz, t, d = x.shape
    return pl.pallas_call(
        _gdn_in_kernel,
        grid=(bsz, t // tm),
        in_specs=[_tok_spec(tm, d), _const_spec(g), _mod_spec(sc, tm), _mod_spec(sh, tm),
                  _const_spec(w_pad)],
        out_specs=[_tok_spec(tm, n_conv), _tok_spec(tm, d), _tok_spec(tm, LANES)],
        out_shape=[jax.ShapeDtypeStruct((bsz, t, n_conv), F32),
                   jax.ShapeDtypeStruct((bsz, t, d), F32),
                   jax.ShapeDtypeStruct((bsz, t, LANES), F32)],
        compiler_params=_params("parallel", "parallel"),
        name="gdn_in",
    )(x, g, sc, sh, w_pad)


def _gdn_conv_kernel(*refs, pre_shifted):
    if pre_shifted:
        x_ref, xs_ref, cw_ref, ba_ref, alog_ref, dtb_ref = refs[:6]
    else:
        x_ref, p8_ref, cw_ref, ba_ref, alog_ref, dtb_ref = refs[:6]
    q_ref, k_ref, kb_ref, vb_ref, g_ref = refs[6:]
    i = pl.program_id(1)
    tm = x_ref.shape[1]
    hd = q_ref.shape[2]
    n_heads = hd // B_HEAD_DIM
    x = x_ref[0]
    conv = x * cw_ref[B_CONV - 1:B_CONV, :]
    if pre_shifted:
        for s in range(1, B_CONV):
            conv = conv + xs_ref[s - 1, 0] * cw_ref[B_CONV - 1 - s:B_CONV - s, :]
    else:
        halo = jnp.where(i > 0, p8_ref[0], 0.0)
        full = jnp.concatenate([halo, x], axis=0)
        for s in range(1, B_CONV):
            conv = conv + pltpu.roll(full, s, 0)[8:] * cw_ref[B_CONV - 1 - s:B_CONV - s, :]
    act = _silu(conv)
    r = lax.broadcasted_iota(jnp.int32, (LANES, hd), 0)
    c_head = lax.broadcasted_iota(jnp.int32, (LANES, hd), 1) // B_HEAD_DIM
    e_b = (r == c_head).astype(BF16)
    e_a = (r == c_head + n_heads).astype(BF16)
    ba = ba_ref[0]
    beta = _sigmoid(_dot_xe(ba, e_b))
    g = -jnp.exp(alog_ref[...]) * _softplus(_dot_xe(ba, e_a) + dtb_ref[...])
    g_ref[0] = g
    ones = _group_matrix(MXU_DIM, B_HEAD_DIM, 1.0)
    aq = act[:, :hd]
    ak = act[:, hd:2 * hd]
    q_ref[0] = aq * lax.rsqrt(_group_reduce(aq * aq, ones) + L2_EPS) * (B_HEAD_DIM ** -0.5)
    k = ak * lax.rsqrt(_group_reduce(ak * ak, ones) + L2_EPS)
    k_ref[0] = k
    kb_ref[0] = k * beta
    vb_ref[0] = act[:, 2 * hd:] * beta


def _gdn_conv(xqkv, xs, conv_w, ba, a_log, dt_bias, hd, tm):
    bsz, t, n_conv = xqkv.shape
    alog = jnp.repeat(a_log, B_HEAD_DIM).reshape(1, hd)
    dtb = jnp.repeat(dt_bias, B_HEAD_DIM).reshape(1, hd)
    pre_shifted = xs is not None
    if pre_shifted:
        second = xs
        second_spec = pl.BlockSpec((B_CONV - 1, 1, tm, n_conv), lambda b, i: (0, b, i, 0))
    else:
        second = xqkv
        second_spec = _prev8_spec(tm, n_conv)
    shp = jax.ShapeDtypeStruct((bsz, t, hd), F32)
    return pl.pallas_call(
        functools.partial(_gdn_conv_kernel, pre_shifted=pre_shifted),
        grid=(bsz, t // tm),
        in_specs=[_tok_spec(tm, n_conv), second_spec, _const_spec(conv_w), _tok_spec(tm, LANES),
                  _const_spec(alog), _const_spec(dtb)],
        out_specs=[_tok_spec(tm, hd)] * 5,
        out_shape=[shp] * 5,
        compiler_params=_params("parallel", "parallel"),
        name="gdn_conv",
    )(xqkv, second, conv_w, ba, alog, dtb)


def _gdn_chunk_kernel(q_ref, k_ref, kb_ref, vb_ref, g_ref, s0_ref, o_ref, s_out_ref, s_ref, *,
                      n_heads, n_chunks):
    i = pl.program_id(2)
    c = CHUNK
    hw = B_HEAD_DIM

    @pl.when(i == 0)
    def _():
        s_ref[...] = s0_ref[0]

    row = lax.broadcasted_iota(jnp.int32, (c, c), 0)
    col = lax.broadcasted_iota(jnp.int32, (c, c), 1)
    lower = row >= col
    strict = row > col
    lower_b = lower.astype(BF16)
    ones_avg = jnp.full((c, LANES), 1.0 / LANES, BF16)

    def row_layout(g_cb):
        gh, gl = _split(g_cb)
        nt = lambda x: lax.dot_general(ones_avg, x, (((1,), (1,)), ((), ())),
                                       preferred_element_type=F32)
        return nt(gh) + nt(gl)

    sls = [(slice(n * c, (n + 1) * c), slice(h * hw, (h + 1) * hw))
           for h in range(n_heads) for n in range(n_chunks)]
    rng = range(len(sls))
    q = [q_ref[0, rs, ls] for rs, ls in sls]
    k = [k_ref[0, rs, ls] for rs, ls in sls]
    kb = [kb_ref[0, rs, ls] for rs, ls in sls]
    vb = [vb_ref[0, rs, ls] for rs, ls in sls]
    g_cb = [_dot_ex(lower_b, g_ref[0, rs, ls]) for rs, ls in sls]
    g_row = [row_layout(x) for x in g_cb]
    decay = [jnp.where(lower, jnp.exp(jnp.where(lower, g_cb[n][:, :c] - g_row[n], 0.0)), 0.0)
             for n in rng]
    a = [jnp.where(strict, decay[n] * _bdot_nt(kb[n], k[n]), 0.0) for n in rng]
    t_inv = _tri_inverses(a, 5)
    e_g = [jnp.exp(x) for x in g_cb]
    wu = [_dot3(t_inv[n], jnp.concatenate([kb[n] * e_g[n], vb[n]], axis=1)) for n in rng]
    qk = [decay[n] * _bdot_nt(q[n], k[n]) for n in rng]
    g_last = [x[c - 1:c, :] for x in g_cb]
    k_end = [jnp.exp(g_last[n] - g_cb[n]) * k[n] for n in rng]
    qw = [_bdot(qk[n], wu[n]) for n in rng]
    q_eff = [q[n] * e_g[n] - qw[n][:, :hw] for n in rng]
    o0 = [x[:, hw:] for x in qw]
    mn = [_bdot_tn(k_end[n], wu[n]) for n in rng]
    m0 = [x[:, :hw] for x in mn]
    n0 = [x[:, hw:] for x in mn]

    states = [s_ref[h] for h in range(n_heads)]
    for m in range(n_chunks):
        for h in range(n_heads):
            n = h * n_chunks + m
            rs, ls = sls[n]
            s = states[h]
            o_ref[0, rs, ls] = _bdot(q_eff[n], s) + o0[n]
            states[h] = jnp.exp(g_last[n]) * s - _bdot(m0[n], s) + n0[n]
    for h in range(n_heads):
        s_ref[h] = states[h]
        s_out_ref[0, h] = states[h]


def _gdn_chunk(q, k, kb, vb, g, s0, ct, heads_per_step):
    bsz, t, hd = q.shape
    hw = B_HEAD_DIM
    blk = pl.BlockSpec((1, ct, heads_per_step * hw), lambda b, h, i: (b, i, h))
    st = pl.BlockSpec((1, heads_per_step, hw, hw), lambda b, h, i: (b, h, 0, 0))
    return pl.pallas_call(
        functools.partial(_gdn_chunk_kernel, n_heads=heads_per_step, n_chunks=ct // CHUNK),
        grid=(bsz, hd // (heads_per_step * hw), t // ct),
        in_specs=[blk] * 5 + [st],
        out_specs=[blk, st],
        out_shape=[jax.ShapeDtypeStruct(q.shape, F32), jax.ShapeDtypeStruct(s0.shape, F32)],
        scratch_shapes=[pltpu.VMEM((heads_per_step, hw, hw), F32)],
        compiler_params=_params("parallel", "parallel", "arbitrary"),
        name="gdn_chunk",
    )(q, k, kb, vb, g, s0)


def _rwkv_in_kernel(*refs, seq_len, per_row_shift):
    (x_ref, p8_ref, g_ref, sc_ref, sh_ref, sh0_ref, mu_ref, wrkv_ref, w1_ref, w2_ref, a1_ref, a2_ref,
     g1_ref, g2_ref, w0_ref, a0_ref, kk_ref, ka_ref) = refs[:18]
    r_ref, lw_ref, k_ref, v_ref, kn_ref, bb_ref, gate_ref, hl_ref = refs[18:]
    i = pl.program_id(1)
    tm = x_ref.shape[1]
    g = g_ref[...]
    h = _normmod(x_ref[0], g, sc_ref[0], sh_ref[0])
    if per_row_shift:
        sc8, sh8 = sc_ref[0, 0:8], sh_ref[0, 0:8]
    else:
        sc8, sh8 = sc_ref[0], sh_ref[0]
    h8 = _normmod(p8_ref[0], g, sc8, sh8)
    prev = pltpu.roll(jnp.concatenate([h8, h], axis=0), 1, 0)[8:]
    t_glob = i * tm + lax.broadcasted_iota(jnp.int32, (tm, 1), 0)
    prev = jnp.where(t_glob % seq_len == 0, sh0_ref[0], prev)
    xx = prev - h
    mix = lambda n: (h + xx * mu_ref[n:n + 1, :]).astype(BF16)
    r = jnp.dot(mix(0), wrkv_ref[0], preferred_element_type=F32)
    k = jnp.dot(mix(1), wrkv_ref[1], preferred_element_type=F32)
    v_ref[0] = jnp.dot(mix(2), wrkv_ref[2], preferred_element_type=F32)
    r_ref[0] = r
    wl = w0_ref[...] + _bdot(jnp.tanh(jnp.dot(mix(3), w1_ref[...], preferred_element_type=F32)),
                             w2_ref[...])
    lw_ref[0] = -jnp.exp(-_softplus(-wl) - 0.5)
    a = _sigmoid(a0_ref[...] + _bdot(jnp.dot(mix(4), a1_ref[...], preferred_element_type=F32),
                                     a2_ref[...]))
    gate_ref[0] = _bdot(_sigmoid(jnp.dot(mix(5), g1_ref[...], preferred_element_type=F32)),
                        g2_ref[...])
    kx = k * kk_ref[...]
    ones = _group_matrix(MXU_DIM, C_HEAD_DIM, 1.0)
    kn = kx * lax.rsqrt(_group_reduce(kx * kx, ones) + L2_EPS)
    kn_ref[0] = kn
    bb_ref[0] = kn * a
    k_ref[0] = k * (1.0 + (a - 1.0) * ka_ref[...])
    hl_ref[0, 0] = h[tm - 8:tm]


def _rwkv_in(x, g, sc, sh, shift0, seq_len, P, tm):
    bsz, t, d = x.shape
    per_row = shift0.shape[1] != 1
    consts = [P['mu'], P['w_rkv'], P['w1'], P['w2'], P['a1'], P['a2'], P['g1'], P['g2'],
              P['w0'], P['a0'], P['k_k'], P['k_a']]
    shp = jax.ShapeDtypeStruct(x.shape, F32)
    n_t = t // tm
    outs = pl.pallas_call(
        functools.partial(_rwkv_in_kernel, seq_len=seq_len, per_row_shift=per_row),
        grid=(bsz, n_t),
        in_specs=[_tok_spec(tm, d), _prev8_spec(tm, d), _const_spec(g), _mod_spec(sc, tm),
                  _mod_spec(sh, tm), _mod_spec(shift0, tm)] + [_const_spec(c) for c in consts],
        out_specs=[_tok_spec(tm, d)] * 7 + [pl.BlockSpec((1, 1, 8, d), lambda b, i: (b, i, 0, 0))],
        out_shape=[shp] * 7 + [jax.ShapeDtypeStruct((bsz, n_t, 8, d), F32)],
        compiler_params=_params("parallel", "parallel"),
        name="rwkv_in",
    )(x, x, g, sc, sh, shift0, *consts)
    return outs


def _rwkv_scan_kernel(r_ref, lw_ref, k_ref, v_ref, kn_ref, bb_ref, s0_ref, y_ref, s_out_ref, s_ref,
                      *, n_pairs, n_chunks):
    i = pl.program_id(2)
    c = CHUNK
    n = 2 * c

    @pl.when(i == 0)
    def _():
        s_ref[...] = s0_ref[0]

    lane = lax.broadcasted_iota(jnp.int32, (1, LANES), 1)
    row = lax.broadcasted_iota(jnp.int32, (n, n), 0)
    col = lax.broadcasted_iota(jnp.int32, (n, n), 1)
    same = (row // c) == (col // c)
    incl = same & (row >= col)
    strict = same & (row > col)
    incl_b = incl.astype(BF16)

    def stack(x):
        return jnp.concatenate([jnp.where(lane < C_HEAD_DIM, x, 0.0),
                                jnp.where(lane >= C_HEAD_DIM, x, 0.0)], axis=0)

    sls = [(slice(m * c, (m + 1) * c), slice(p * LANES, (p + 1) * LANES))
           for p in range(n_pairs) for m in range(n_chunks)]
    rng = range(len(sls))
    lw = [stack(lw_ref[0, rs, ls]) for rs, ls in sls]
    lc = [_dot_ex(incl_b, x) for x in lw]
    lc_last = [x[c - 1:c, :] + x[n - 1:n, :] for x in lc]
    p_inv = [jnp.exp(-x) for x in lc]
    p_end = [jnp.exp(lc_last[m] - lc[m]) for m in rng]
    kn_t = [stack(kn_ref[0, sls[m][0], sls[m][1]]) * jnp.exp(lc[m] - lw[m]) for m in rng]
    k_s = [stack(k_ref[0, rs, ls]) for rs, ls in sls]
    b_s = [stack(bb_ref[0, rs, ls]) for rs, ls in sls]
    v_s = [stack(v_ref[0, rs, ls]) for rs, ls in sls]
    k_t = [k_s[m] * p_inv[m] for m in rng]
    b_t = [b_s[m] * p_inv[m] for m in rng]
    r_t = [stack(r_ref[0, sls[m][0], sls[m][1]]) * jnp.exp(lc[m]) for m in rng]
    bk_t = [jnp.concatenate([b_t[m], k_t[m]], axis=0) for m in rng]
    a_k = [_bdot_nt(kn_t[m], bk_t[m]) for m in rng]
    a_r = [_bdot_nt(r_t[m], bk_t[m]) for m in rng]
    a_kb = [jnp.where(strict, x[:, :n], 0.0) for x in a_k]
    a_kk = [jnp.where(strict, x[:, n:], 0.0) for x in a_k]
    a_rb = [jnp.where(incl, x[:, :n], 0.0) for x in a_r]
    a_rk = [jnp.where(incl, x[:, n:], 0.0) for x in a_r]
    t_inv = _tri_inverses(a_kb, 5)
    ku = [_dot3(t_inv[m], jnp.concatenate([kn_t[m], _bdot(a_kk[m], v_s[m])], axis=1))
          for m in rng]
    kt = [x[:, :LANES] for x in ku]
    u0 = [x[:, LANES:] for x in ku]
    k_e = [k_s[m] * p_end[m] for m in rng]
    b_e = [b_s[m] * p_end[m] for m in rng]
    g_mat = [_bdot_tn(kt[m], b_e[m]) for m in rng]
    n0 = [_bdot_tn(v_s[m], k_e[m]) - _bdot_tn(u0[m], b_e[m]) for m in rng]
    rb = [_bdot(a_rb[m], ku[m]) for m in rng]
    r_eff = [r_t[m] - rb[m][:, :LANES] for m in rng]
    y0 = [_bdot(a_rk[m], v_s[m]) - rb[m][:, LANES:] for m in rng]

    states = [s_ref[p] for p in range(n_pairs)]
    for j in range(n_chunks):
        for p in range(n_pairs):
            m = p * n_chunks + j
            rs, ls = sls[m]
            s = states[p]
            y = _bdot_nt(r_eff[m], s) + y0[m]
            y_ref[0, rs, ls] = y[:c] + y[c:]
            states[p] = s * jnp.exp(lc_last[m]) - _bdot(s, g_mat[m]) + n0[m]
    for p in range(n_pairs):
        s_ref[p] = states[p]
        s_out_ref[0, p] = states[p]


def _rwkv_scan(r, lw, k, v, kn, bb, s0_pairs, ct, pairs_per_step):
    bsz, t, d = r.shape
    blk = pl.BlockSpec((1, ct, pairs_per_step * LANES), lambda b, p, i: (b, i, p))
    st = pl.BlockSpec((1, pairs_per_step, LANES, LANES), lambda b, p, i: (b, p, 0, 0))
    return pl.pallas_call(
        functools.partial(_rwkv_scan_kernel, n_pairs=pairs_per_step, n_chunks=ct // CHUNK),
        grid=(bsz, d // (pairs_per_step * LANES), t // ct),
        in_specs=[blk] * 6 + [st],
        out_specs=[blk, st],
        out_shape=[jax.ShapeDtypeStruct(r.shape, F32), jax.ShapeDtypeStruct(s0_pairs.shape, F32)],
        scratch_shapes=[pltpu.VMEM((pairs_per_step, LANES, LANES), F32)],
        compiler_params=_params("parallel", "parallel", "arbitrary"),
        name="rwkv_scan",
    )(r, lw, k, v, kn, bb, s0_pairs)


def _chain_split(t_pad, n_heads):
    heads = min(n_heads, HEADS_PER_STEP)
    ct = min(t_pad, CHAINS_PER_STEP // heads * CHUNK)
    heads = max(heads, min(n_heads, CHAINS_PER_STEP * CHUNK // ct))
    return ct, heads


def _pad_time(a, t_pad):
    return jnp.pad(a, ((0, 0), (0, t_pad - a.shape[1]), (0, 0)))


def _pairs_from_heads(s):
    bsz, n_h, n, _ = s.shape
    sp = s.reshape(bsz, n_h // 2, 2, n, n)
    z = jnp.zeros_like(sp[:, :, 0])
    top = jnp.concatenate([sp[:, :, 0], z], axis=-1)
    bot = jnp.concatenate([z, sp[:, :, 1]], axis=-1)
    return jnp.concatenate([top, bot], axis=-2)


def _heads_from_pairs(sp):
    bsz, n_p, n2, _ = sp.shape
    n = n2 // 2
    return jnp.stack([sp[:, :, :n, :n], sp[:, :, n:, n:]], axis=2).reshape(bsz, 2 * n_p, n, n)


def _run_trunk(x, mods, W, tm, seq_len, cache, state):
    bsz, t, d = x.shape
    n_seq = bsz * t // seq_len
    depth = len(mods)
    kv_all, bconv, bssm, cshift, cwkv = None, [], [], [], []
    n_a = W['a_w_qkv'].shape[0]
    for l in range(depth):
        kind, j = l % 3, l // 3
        sh1, sc1, gt1, sh2, sc2, gt2 = mods[l]
        g_mix = W['ln_mix'][l].reshape(1, d)
        if kind == 0:
            q, k_all, v_all = _qkv(x, g_mix, sc1, sh1, W['a_w_qkv'], j, W['a_q_norm'][j],
                                   W['a_k_norm'][j], kv_all, tm)
            kv_all = (k_all, v_all)
            if cache is None:
                o = _sb_prompt(q, k_all, v_all, j, W['a_logit_bias'][j],
                               qb=SB_QUERY_BLOCK, kb=SB_KEY_BLOCK)
            else:
                cache_k, cache_v, page_table = cache
                o = _sb_decode(q.reshape(n_seq, seq_len, d), k_all.reshape(n_a, n_seq, seq_len, d),
                               v_all.reshape(n_a, n_seq, seq_len, d), j, cache_k, cache_v, j,
                               page_table, W['a_logit_bias'][j],
                               pages_per_step=min(8, page_table.shape[1])).reshape(bsz, t, d)
            x = _proj_res(o, W['a_w_o'][j], x, gt1, tm)
        elif kind == 1:
            n_conv = 3 * d
            xqkv, z, ba = _gdn_in(x, g_mix, sc1, sh1, W['b_w_in'][j], n_conv, tm)
            xq_seq = xqkv.reshape(n_seq, seq_len, n_conv)
            if state is None:
                xs = None
                bconv.append(xq_seq[:, seq_len - (B_CONV - 1):])
                s0 = jnp.zeros((n_seq, d // B_HEAD_DIM, B_HEAD_DIM, B_HEAD_DIM), F32)
            else:
                xc = jnp.concatenate([state['b_conv'][j], xq_seq], axis=1)
                xs = jnp.stack([xc[:, B_CONV - 1 - s:B_CONV - 1 - s + seq_len].reshape(bsz, t, n_conv)
                                for s in range(1, B_CONV)])
                bconv.append(xc[:, seq_len:])
                s0 = state['b_ssm'][j]
            q, k, kb, vb, g = _gdn_conv(xqkv, xs, W['b_conv'][j], ba, W['b_a_log'][j],
                                        W['b_dt_bias'][j], d, min(tm, 256))
            t_pad = -(-seq_len // CHUNK) * CHUNK
            seqs = [_pad_time(a.reshape(n_seq, seq_len, d), t_pad) for a in (q, k, kb, vb, g)]
            ct, heads = _chain_split(t_pad, d // B_HEAD_DIM)
            o, s_new = _gdn_chunk(*seqs, s0, ct=ct, heads_per_step=heads)
            bssm.append(s_new)
            o = o[:, :seq_len].reshape(bsz, t, d)
            x = _gdn_out(o, z, W['b_o_norm'][j], W['b_w_o'][j], x, gt1, tm)
        else:
            if state is None:
                shift0 = jnp.zeros((bsz, 1, d), F32)
                s0 = jnp.zeros((n_seq, d // C_HEAD_DIM, C_HEAD_DIM, C_HEAD_DIM), F32)
            else:
                shift0 = jnp.repeat(state['c_shift'][j], seq_len, axis=0).reshape(bsz, t, d)
                s0 = state['c_wkv'][j]
            tm_c = min(tm, 512)
            r, lw, k, v, kn, bb, gate, h_last = _rwkv_in(x, g_mix, sc1, sh1, shift0, seq_len,
                                                        W['c'][j], tm_c)
            if state is None:
                cshift.append(h_last[:, -1, 7])
            else:
                assert t == tm_c and seq_len == 8
                cshift.append(_last_rows(x, g_mix, sc1, sh1, seq_len))
            t_pad = -(-seq_len // CHUNK) * CHUNK
            seqs = [_pad_time(a.reshape(n_seq, seq_len, d), t_pad) for a in (r, lw, k, v, kn, bb)]
            ct, pairs = _chain_split(t_pad, d // LANES)
            y, s_new = _rwkv_scan(*seqs, _pairs_from_heads(s0), ct=ct, pairs_per_step=pairs)
            cwkv.append(_heads_from_pairs(s_new))
            y = y[:, :seq_len].reshape(bsz, t, d)
            x = _rwkv_out(y, r, k, v, gate, W['c_r_k'][j], W['c_ln_g'][j], W['c_ln_b'][j],
                          W['c_w_o'][j], x, gt1, tm)
        x = _mlp(x, W['ln_mlp'][l].reshape(1, d), sc2, sh2, gt2, W['w_up'], W['w_down'], l,
                 MLP_TOKEN_TILE if t % MLP_TOKEN_TILE == 0 else tm)
    heads = (n_a, n_seq, seq_len, d // A_HEAD_DIM, A_HEAD_DIM)
    return x, kv_all[0].reshape(heads), kv_all[1].reshape(heads), bconv, bssm, cshift, cwkv


def _hmod_kernel(x_ref, g_ref, sc_ref, sh_ref, o_ref):
    o_ref[0] = _normmod(x_ref[0], g_ref[...], sc_ref[0], sh_ref[0])


def _last_rows(x, g, sc, sh, seq_len):
    bsz, t, d = x.shape
    n_seq = bsz * t // seq_len
    pick = lambda a: a.reshape(n_seq, seq_len, d)[:, seq_len - 1].reshape(1, n_seq, d)
    xs, scs, shs = pick(x), pick(sc), pick(sh)
    return pl.pallas_call(
        _hmod_kernel,
        grid=(1, 1),
        in_specs=[_tok_spec(n_seq, d), _const_spec(g), _tok_spec(n_seq, d), _tok_spec(n_seq, d)],
        out_specs=_tok_spec(n_seq, d),
        out_shape=jax.ShapeDtypeStruct((1, n_seq, d), F32),
        compiler_params=_params("parallel", "parallel"),
        name="last_rows",
    )(xs, g, scs, shs)[0]


def kernel(x_prompt, x_sample, c_prompt, c_sample, cache_k, cache_v, page_table, state_b_conv, state_b_ssm, state_c_shift, state_c_wkv, ln_mix, ln_mlp, w_ada, b_ada, w_up, w_down, a_w_qkv, a_q_norm, a_k_norm, a_logit_bias, a_w_o, b_w_in, b_conv, b_a_log, b_dt_bias, b_o_norm, b_w_o, c_mu, c_w_rkv, c_w0, c_w1, c_w2, c_a0, c_a1, c_a2, c_g1, c_g2, c_k_k, c_k_a, c_r_k, c_ln_g, c_ln_b, c_w_o):
    bp, t_p, d = x_prompt.shape
    bs, t_s, _ = x_sample.shape
    depth = ln_mix.shape[0]
    n_c = c_mu.shape[0]

    def lora_cols(w):
        return jnp.pad(w, ((0, 0), (0, 0), (0, LANES - w.shape[2]))).astype(BF16)

    def lora_rows(w):
        return jnp.pad(w, ((0, 0), (0, LANES - w.shape[1]), (0, 0))).astype(BF16)

    b_in = b_w_in.shape[2]
    b_in_pad = -(-b_in // LANES) * LANES
    W = {
        'ln_mix': ln_mix, 'ln_mlp': ln_mlp,
        'w_up': w_up.astype(BF16), 'w_down': w_down.astype(BF16),
        'a_w_qkv': a_w_qkv.astype(BF16), 'a_q_norm': a_q_norm, 'a_k_norm': a_k_norm,
        'a_logit_bias': a_logit_bias, 'a_w_o': a_w_o.astype(BF16),
        'b_w_in': jnp.pad(b_w_in, ((0, 0), (0, 0), (0, b_in_pad - b_in))).astype(BF16),
        'b_conv': b_conv, 'b_a_log': b_a_log, 'b_dt_bias': b_dt_bias, 'b_o_norm': b_o_norm,
        'b_w_o': b_w_o.astype(BF16),
        'c_r_k': c_r_k.reshape(n_c, d), 'c_ln_g': c_ln_g, 'c_ln_b': c_ln_b, 'c_w_o': c_w_o.astype(BF16),
    }
    w1, w2 = lora_cols(c_w1), lora_rows(c_w2)
    a1, a2 = lora_cols(c_a1), lora_rows(c_a2)
    g1, g2 = c_g1.astype(BF16), c_g2.astype(BF16)
    rkv = c_w_rkv.astype(BF16)
    W['c'] = [{'mu': c_mu[j], 'w_rkv': rkv[j], 'w1': w1[j], 'w2': w2[j], 'a1': a1[j], 'a2': a2[j],
               'g1': g1[j], 'g2': g2[j], 'w0': c_w0[j].reshape(1, d), 'a0': c_a0[j].reshape(1, d),
               'k_k': c_k_k[j].reshape(1, d), 'k_a': c_k_a[j].reshape(1, d)} for j in range(n_c)]

    mod_all = _ada(jnp.concatenate([c_prompt, c_sample], axis=0), w_ada, b_ada)
    mod_p = mod_all[:, :, :bp].reshape(depth, 6, bp, 1, d)
    mod_s = jnp.repeat(mod_all[:, :, bp:], t_s, axis=2).reshape(depth, 6, 1, bs * t_s, d)
    mods_p = [[mod_p[l, n] for n in range(6)] for l in range(depth)]
    mods_s = [[mod_s[l, n] for n in range(6)] for l in range(depth)]

    y_p, k_p, v_p, bconv_p, bssm_p, cshift_p, cwkv_p = _run_trunk(
        x_prompt, mods_p, W, 512, t_p, None, None)
    state = {'b_conv': state_b_conv, 'b_ssm': state_b_ssm, 'c_shift': state_c_shift,
             'c_wkv': state_c_wkv}
    y_s, k_s, v_s, bconv_s, bssm_s, cshift_s, cwkv_s = _run_trunk(
        x_sample.reshape(1, bs * t_s, d), mods_s, W, bs * t_s, t_s,
        (cache_k, cache_v, page_table), state)
    st = jnp.stack
    return (y_p, y_s.reshape(bs, t_s, d), k_p, v_p, k_s, v_s, st(bconv_p), st(bssm_p),
            st(bconv_s), st(bssm_s), st(cshift_p), st(cwkv_p), st(cshift_s), st(cwkv_s))
```

```python
import functools

import jax
import jax.numpy as jnp
from jax import lax
from jax.experimental import pallas as pl
from jax.experimental.pallas import tpu as pltpu

F32 = jnp.float32
BF16 = jnp.bfloat16

LANES = 128
MXU_DIM = 256
VMEM_LIMIT_BYTES = 56 * 1024 * 1024

NORM_EPS = 1e-6
L2_EPS = 1e-6
GN_EPS = 64e-5
LOG2E = 1.4426950408889634
A_HEAD_DIM = 64
B_HEAD_DIM = 128
C_HEAD_DIM = 64
B_CONV = 4
CHUNK = 64
CHAINS_PER_STEP = 16
HEADS_PER_STEP = 4
SB_QUERY_BLOCK = 512
SB_KEY_BLOCK = MXU_DIM
MLP_TOKEN_TILE = 1024


def _params(*sem):
    return pltpu.CompilerParams(dimension_semantics=sem, vmem_limit_bytes=VMEM_LIMIT_BYTES)


def _bdot(a, b):
    return jnp.dot(a.astype(BF16), b.astype(BF16), preferred_element_type=F32)


def _bdot_nt(a, b):
    return lax.dot_general(a.astype(BF16), b.astype(BF16), (((1,), (1,)), ((), ())),
                           preferred_element_type=F32)


def _bdot_tn(a, b):
    return lax.dot_general(a.astype(BF16), b.astype(BF16), (((0,), (0,)), ((), ())),
                           preferred_element_type=F32)


def _split(x):
    hi = x.astype(BF16)
    lo = (x - hi.astype(F32)).astype(BF16)
    return hi, lo


def _sum_dots(lhs, rhs):
    return jnp.dot(jnp.concatenate(lhs, axis=1), jnp.concatenate(rhs, axis=0),
                   preferred_element_type=F32)


def _dot_xe(x, e):
    hi, lo = _split(x)
    return _sum_dots([hi, lo], [e, e])


def _dot_ex(e, x):
    hi, lo = _split(x)
    return _sum_dots([e, e], [hi, lo])


def _dot3_parts(a_parts, b_parts):
    ah, al = a_parts
    bh, bl = b_parts
    return _sum_dots([ah, ah, al], [bh, bl, bh])


def _dot3(a, b):
    return _dot3_parts(_split(a), _split(b))


def _sigmoid(x):
    return 1.0 / (1.0 + jnp.exp(-x))


def _silu(x):
    return x * _sigmoid(x)


def _softplus(x):
    return jnp.maximum(x, 0.0) + jnp.log(1.0 + jnp.exp(-jnp.abs(x)))


def _neg_abs(x):
    bits = lax.bitcast_convert_type(x, jnp.uint32) | jnp.uint32(0x80000000)
    return lax.bitcast_convert_type(bits, F32)


def _sb_suffix(z, suffix2, mask):
    sp = jnp.maximum(z, 0.0) + jnp.log(1.0 + jnp.exp2(_neg_abs(z))) * LOG2E
    if mask is not None:
        sp = jnp.where(mask, sp, 0.0)
    hi, lo = _split(sp)
    return jnp.dot(jnp.concatenate([hi, lo], axis=1), suffix2, preferred_element_type=F32)


def _sb_local(z, s_incl, mask):
    w = jnp.exp2(z - s_incl)
    if mask is not None:
        w = jnp.where(mask, w, 0.0)
    return w.astype(BF16)


def _sb_weights(z, suffix2, mask):
    s_incl = _sb_suffix(z, suffix2, mask)
    return _sb_local(z, s_incl, mask), s_incl[:, 0:1]


def _normmod(x, g, sc, sh):
    ms = jnp.mean(x * x, axis=-1, keepdims=True)
    return x * lax.rsqrt(ms + NORM_EPS) * g * (1.0 + sc) + sh


def _group_matrix(n, group, value):
    r = lax.broadcasted_iota(jnp.int32, (n, n), 0) // group
    c = lax.broadcasted_iota(jnp.int32, (n, n), 1) // group
    return jnp.where(r == c, value, 0.0).astype(BF16)


def _group_reduce(x, gmat, split=False):
    n = gmat.shape[0]
    dot = _dot_xe if split else _bdot
    parts = [dot(x[:, s:s + n], gmat) for s in range(0, x.shape[1], n)]
    return parts[0] if len(parts) == 1 else jnp.concatenate(parts, axis=1)


def _tri_inverses(mats, n_steps):
    n = mats[0].shape[0]
    eye = (lax.broadcasted_iota(jnp.int32, (n, n), 0)
           == lax.broadcasted_iota(jnp.int32, (n, n), 1)).astype(F32)
    ps = [-a for a in mats]
    ts = [eye + p for p in ps]
    parts = [_split(p) for p in ps]
    for _ in range(n_steps):
        parts = [_split(_dot3_parts(s, s)) for s in parts]
        ts = [t + _dot3_parts(_split(t), s) for t, s in zip(ts, parts)]
    return ts


def _ada_kernel(c_ref, w_ref, b_ref, o_ref):
    ca = _silu(c_ref[...])
    o_ref[0, 0] = _bdot(ca, w_ref[0]) + b_ref[0]


def _ada(c_all, w_ada, b_ada):
    depth, d, n = w_ada.shape
    rows = c_all.shape[0]
    return pl.pallas_call(
        _ada_kernel,
        grid=(depth, n // d),
        in_specs=[pl.BlockSpec((rows, d), lambda l, j: (0, 0)),
                  pl.BlockSpec((1, d, d), lambda l, j: (l, 0, j)),
                  pl.BlockSpec((1, 1, d), lambda l, j: (l, 0, j))],
        out_specs=pl.BlockSpec((1, 1, rows, d), lambda l, j: (l, j, 0, 0)),
        out_shape=jax.ShapeDtypeStruct((depth, n // d, rows, d), F32),
        compiler_params=_params("parallel", "parallel"),
        name="ada",
    )(c_all, w_ada, b_ada.reshape(depth, 1, n))


def _tok_spec(tm, width):
    return pl.BlockSpec((1, tm, width), lambda b, i: (b, i, 0))


def _mod_spec(mod, tm):
    if mod.shape[1] == 1:
        return pl.BlockSpec((1, 1, mod.shape[2]), lambda b, i: (b, 0, 0))
    return pl.BlockSpec((1, tm, mod.shape[2]), lambda b, i: (b, i, 0))


def _const_spec(arr):
    nd = arr.ndim
    return pl.BlockSpec(arr.shape, lambda b, i: (0,) * nd, pipeline_mode=pl.Buffered(1))


def _prev8_spec(tm, width):
    blocks = tm // 8
    return pl.BlockSpec((1, 8, width), lambda b, i: (b, jnp.maximum(i * blocks - 1, 0), 0))


def _mlp_kernel(x_ref, g_ref, sc_ref, sh_ref, gt_ref, wu_ref, wd_ref, o_ref, *, n_split, tf):
    tm = x_ref.shape[1]
    ff = wu_ref.shape[2]
    rows = [slice(r * tm // n_split, (r + 1) * tm // n_split) for r in range(n_split)]

    def mod_rows(ref, sl):
        m = ref[0]
        return m if m.shape[0] == 1 else m[sl]

    hs = [_normmod(x_ref[0, sl, :], g_ref[...], mod_rows(sc_ref, sl), mod_rows(sh_ref, sl)).astype(BF16)
          for sl in rows]
    accs = [None] * n_split
    for f in range(0, ff, tf):
        for r in range(n_split):
            u = jnp.maximum(jnp.dot(hs[r], wu_ref[0, :, f:f + tf], preferred_element_type=F32), 0.0)
            part = jnp.dot((u * u).astype(BF16), wd_ref[0, f:f + tf, :], preferred_element_type=F32)
            accs[r] = part if accs[r] is None else accs[r] + part
    for r, sl in enumerate(rows):
        o_ref[0, sl, :] = x_ref[0, sl, :] + mod_rows(gt_ref, sl) * accs[r]


def _mlp(x, g, sc, sh, gt, w_up, w_down, layer, tm):
    bsz, t, d = x.shape
    ff = w_up.shape[2]
    resident = dict(pipeline_mode=pl.Buffered(1))
    return pl.pallas_call(
        functools.partial(_mlp_kernel, n_split=2, tf=1024),
        grid=(bsz, t // tm),
        in_specs=[_tok_spec(tm, d), _const_spec(g),
                  _mod_spec(sc, tm), _mod_spec(sh, tm), _mod_spec(gt, tm),
                  pl.BlockSpec((1, d, ff), lambda b, i: (layer, 0, 0), **resident),
                  pl.BlockSpec((1, ff, d), lambda b, i: (layer, 0, 0), **resident)],
        out_specs=_tok_spec(tm, d),
        out_shape=jax.ShapeDtypeStruct(x.shape, F32),
        compiler_params=_params("parallel", "parallel"),
        name="mlp",
    )(x, g, sc, sh, gt, w_up, w_down)


def _qkv_kernel(x_ref, g_ref, sc_ref, sh_ref, w_ref, qg_ref, kg_ref, *rest):
    q_ref, k_ref, v_ref = rest[-3:]
    d = x_ref.shape[2]
    h = _normmod(x_ref[0], g_ref[...], sc_ref[0], sh_ref[0]).astype(BF16)
    gmat = _group_matrix(MXU_DIM, A_HEAD_DIM, 1.0 / A_HEAD_DIM)
    q_scale = A_HEAD_DIM ** -0.5 * LOG2E
    for s in range(0, d, MXU_DIM):
        sl = slice(s, s + MXU_DIM)
        q = jnp.dot(h, w_ref[0, :, s:s + MXU_DIM], preferred_element_type=F32)
        q_ref[0, :, sl] = q * lax.rsqrt(_bdot(q * q, gmat) + NORM_EPS) * qg_ref[...] * q_scale
        k = jnp.dot(h, w_ref[0, :, d + s:d + s + MXU_DIM], preferred_element_type=F32)
        k = k * lax.rsqrt(_bdot(k * k, gmat) + NORM_EPS) * kg_ref[...]
        v = jnp.dot(h, w_ref[0, :, 2 * d + s:2 * d + s + MXU_DIM], preferred_element_type=F32)
        for slab in range(k_ref.shape[0]):
            k_ref[slab, 0, :, sl] = k
            v_ref[slab, 0, :, sl] = v


def _qkv(x, g, sc, sh, w_all, layer, q_gain, k_gain, kv_all, tm):
    bsz, t, d = x.shape
    n_layers = w_all.shape[0]
    qg = jnp.tile(q_gain, MXU_DIM // A_HEAD_DIM).reshape(1, MXU_DIM)
    kg = jnp.tile(k_gain, MXU_DIM // A_HEAD_DIM).reshape(1, MXU_DIM)
    kv_shape = jax.ShapeDtypeStruct((n_layers, bsz, t, d), F32)
    if kv_all is None:
        kv_spec = pl.BlockSpec((n_layers, 1, tm, d), lambda b, i: (0, b, i, 0))
    else:
        kv_spec = pl.BlockSpec((1, 1, tm, d), lambda b, i: (layer, b, i, 0))
    w_spec = pl.BlockSpec((1,) + w_all.shape[1:], lambda b, i: (layer, 0, 0),
                          pipeline_mode=pl.Buffered(1))
    aliased = [] if kv_all is None else list(kv_all)
    n_in = 7
    return pl.pallas_call(
        _qkv_kernel,
        grid=(bsz, t // tm),
        in_specs=[_tok_spec(tm, d), _const_spec(g), _mod_spec(sc, tm), _mod_spec(sh, tm),
                  w_spec, _const_spec(qg), _const_spec(kg)]
                 + [pl.BlockSpec(memory_space=pl.ANY)] * len(aliased),
        out_specs=[_tok_spec(tm, d), kv_spec, kv_spec],
        out_shape=[jax.ShapeDtypeStruct(x.shape, F32), kv_shape, kv_shape],
        input_output_aliases={n_in + n: 1 + n for n in range(len(aliased))},
        compiler_params=_params("parallel", "parallel"),
        name="qkv",
    )(x, g, sc, sh, w_all, qg, kg, *aliased)


def _sb_prompt_kernel(bias_ref, q_ref, k_ref, v_ref, o_ref, z_ref, s_ref, acc_ref, c_ref, *, qb, kb):
    hp = pl.program_id(1)
    i = pl.program_id(2)
    q = q_ref[0]
    lane = lax.broadcasted_iota(jnp.int32, (1, LANES), 1)
    row = lax.broadcasted_iota(jnp.int32, (kb, kb), 0)
    col = lax.broadcasted_iota(jnp.int32, (kb, kb), 1)
    suffix = (row >= col).astype(BF16)
    suffix2 = jnp.concatenate([suffix, suffix], axis=0)
    qm = [jnp.where(lane // A_HEAD_DIM == e, q, 0.0).astype(BF16) for e in range(2)]
    bias = [bias_ref[2 * hp + e] * LOG2E for e in range(2)]
    per_q = qb // kb
    n_blocks = per_q * (i + 1)

    def rows(t):
        return pl.ds(pl.multiple_of(jnp.maximum(n_blocks - 1 - t, 0) * kb, kb), kb)

    def first_row(t):
        return (per_q - 1 - t) * kb if isinstance(t, int) and t < per_q else 0

    def mask(t):
        if not isinstance(t, int) or t >= per_q:
            return None
        shape = (qb - first_row(t), kb)
        return lax.broadcasted_iota(jnp.int32, shape, 1) < lax.broadcasted_iota(jnp.int32, shape, 0)

    def logits(t, slot, e):
        r0 = first_row(t)
        z_ref[slot, e, r0:, :] = _bdot_nt(qm[e][r0:], k_ref[0, 0, rows(t), :]) + bias[e]

    def suffix_sums(t, slot, e):
        r0 = first_row(t)
        s_ref[slot, e, r0:, :] = _sb_suffix(z_ref[slot, e, r0:, :], suffix2, mask(t))

    def consume(t, slot, e):
        r0 = first_row(t)
        s_incl = s_ref[slot, e, r0:, :]
        p = _bdot(_sb_local(z_ref[slot, e, r0:, :], s_incl, mask(t)), v_ref[0, 0, rows(t), :])
        acc_ref[e, r0:, :] += jnp.exp2(-c_ref[e, r0:, :]) * p
        c_ref[e, r0:, :] += s_incl[:, 0:1]

    def trip(t, slot):
        for e in range(2):
            consume(t, slot, e)
            logits(t + 2, slot, e)
            suffix_sums(t + 1, 1 - slot, e)

    acc_ref[...] = jnp.zeros_like(acc_ref)
    c_ref[...] = jnp.zeros_like(c_ref)
    for e in range(2):
        logits(0, 0, e)
        logits(1, 1, e)
        suffix_sums(0, 0, e)
    for t in range(per_q):
        trip(t, t % 2)

    def pair(u, _):
        trip(2 * u + per_q, 0)
        trip(2 * u + per_q + 1, 1)
        return 0

    lax.fori_loop(0, (per_q // 2) * i, pair, 0)
    o_ref[0] = jnp.where(lane < A_HEAD_DIM, acc_ref[0], acc_ref[1])


def _sb_prompt(q, k_all, v_all, layer, logit_bias, qb, kb):
    bsz, t, d = q.shape
    n_pairs = d // LANES
    assert qb % (2 * kb) == 0 and t % qb == 0
    kv_spec = pl.BlockSpec((1, 1, t, LANES), lambda b, p, i, bias: (layer, b, 0, p))
    grid_spec = pltpu.PrefetchScalarGridSpec(
        num_scalar_prefetch=1,
        grid=(bsz, n_pairs, t // qb),
        in_specs=[pl.BlockSpec((1, qb, LANES), lambda b, p, i, bias: (b, i, p)), kv_spec, kv_spec],
        out_specs=pl.BlockSpec((1, qb, LANES), lambda b, p, i, bias: (b, i, p)),
        scratch_shapes=[pltpu.VMEM((2, 2, qb, kb), F32), pltpu.VMEM((2, 2, qb, kb), F32),
                        pltpu.VMEM((2, qb, LANES), F32), pltpu.VMEM((2, qb, 1), F32)],
    )
    return pl.pallas_call(
        functools.partial(_sb_prompt_kernel, qb=qb, kb=kb),
        grid_spec=grid_spec,
        out_shape=jax.ShapeDtypeStruct(q.shape, F32),
        compiler_params=_params("parallel", "parallel", "arbitrary"),
        name="sb_prompt",
    )(logit_bias, q, k_all, v_all)


def _sb_decode_kernel(pt_ref, q_ref, bias_ref, kn_ref, vn_ref, *rest, pages_per_step, page, t_new):
    k_refs = rest[:pages_per_step]
    v_refs = rest[pages_per_step:2 * pages_per_step]
    o_ref = rest[2 * pages_per_step]
    acc_ref, c_ref, qbd_ref = rest[2 * pages_per_step + 1:]
    s = pl.program_id(1)
    d = q_ref.shape[2]
    n_heads = d // A_HEAD_DIM
    n_rows = n_heads * t_new
    r_head = lax.broadcasted_iota(jnp.int32, (n_rows, d), 0) // t_new
    c_head = lax.broadcasted_iota(jnp.int32, (n_rows, d), 1) // A_HEAD_DIM

    @pl.when(s == 0)
    def _():
        tiled = jnp.concatenate([q_ref[0]] * n_heads, axis=0)
        qbd_ref[...] = jnp.where(r_head == c_head, tiled, 0.0).astype(BF16)

    qbd = qbd_ref[...]
    bias = bias_ref[...]
    row = lax.broadcasted_iota(jnp.int32, (page, page), 0)
    col = lax.broadcasted_iota(jnp.int32, (page, page), 1)
    suffix = (row >= col).astype(BF16)
    suffix2 = jnp.concatenate([suffix, suffix], axis=0)

    def accumulate(p, tot):
        acc_ref[...] += jnp.exp2(-c_ref[...]) * p
        c_ref[...] += tot

    @pl.when(s == 0)
    def _():
        acc_ref[...] = jnp.zeros_like(acc_ref)
        c_ref[...] = jnp.zeros_like(c_ref)
        pad = jnp.zeros((page - t_new, d), F32)
        k_pad = jnp.concatenate([kn_ref[0, 0], pad], axis=0).astype(BF16)
        v_pad = jnp.concatenate([vn_ref[0, 0], pad], axis=0).astype(BF16)
        q_t = lax.broadcasted_iota(jnp.int32, (n_rows, page), 0) % t_new
        key_i = lax.broadcasted_iota(jnp.int32, (n_rows, page), 1)
        w, tot = _sb_weights(_bdot_nt(qbd, k_pad) + bias, suffix2, key_i < q_t)
        accumulate(jnp.dot(w, v_pad, preferred_element_type=F32), tot)

    zs = [jnp.dot(qbd, k_refs[r][0, 0].astype(BF16), preferred_element_type=F32) + bias
          for r in range(pages_per_step)]
    ws = [_sb_weights(z, suffix2, None) for z in zs]
    ps = [_bdot_nt(w, v_refs[r][0, 0]) for r, (w, _) in enumerate(ws)]
    for p, (_, tot) in zip(ps, ws):
        accumulate(p, tot)

    @pl.when(s == pl.num_programs(1) - 1)
    def _():
        diag = jnp.where(r_head == c_head, acc_ref[...], 0.0)
        out = diag[0:t_new]
        for h in range(1, n_heads):
            out = out + diag[h * t_new:(h + 1) * t_new]
        o_ref[0] = out


def _sb_decode(q, k_all, v_all, new_layer, cache_k, cache_v, layer, page_table, logit_bias,
               pages_per_step):
    bsz, t_new, d = q.shape
    n_heads = d // A_HEAD_DIM
    n_layers, n_pool, page = cache_k.shape[:3]
    n_pages = page_table.shape[1]
    n_rows = n_heads * t_new
    ck = cache_k.transpose(0, 1, 3, 4, 2).reshape(n_layers, n_pool, d, page)
    cv = cache_v.transpose(0, 1, 3, 4, 2).reshape(n_layers, n_pool, d, page)
    bias_rows = jnp.broadcast_to(jnp.repeat(logit_bias * LOG2E, t_new)[:, None], (n_rows, page))
    n_steps = n_pages // pages_per_step

    def page_map(r):
        def index(b, s, pt):
            return (layer, pt[b, n_pages - 1 - (s * pages_per_step + r)], 0, 0)
        return index

    kv_specs = [pl.BlockSpec((1, 1, d, page), page_map(r)) for r in range(pages_per_step)]
    grid_spec = pltpu.PrefetchScalarGridSpec(
        num_scalar_prefetch=1,
        grid=(bsz, n_steps),
        in_specs=[pl.BlockSpec((1, t_new, d), lambda b, s, pt: (b, 0, 0)),
                  pl.BlockSpec((n_rows, page), lambda b, s, pt: (0, 0)),
                  pl.BlockSpec((1, 1, t_new, d), lambda b, s, pt: (new_layer, b, 0, 0)),
                  pl.BlockSpec((1, 1, t_new, d), lambda b, s, pt: (new_layer, b, 0, 0))]
                 + kv_specs + kv_specs,
        out_specs=pl.BlockSpec((1, t_new, d), lambda b, s, pt: (b, 0, 0)),
        scratch_shapes=[pltpu.VMEM((n_rows, d), F32), pltpu.VMEM((n_rows, 1), F32),
                        pltpu.VMEM((n_rows, d), BF16)],
    )
    return pl.pallas_call(
        functools.partial(_sb_decode_kernel, pages_per_step=pages_per_step, page=page, t_new=t_new),
        grid_spec=grid_spec,
        out_shape=jax.ShapeDtypeStruct(q.shape, F32),
        compiler_params=_params("parallel", "arbitrary"),
        name="sb_decode",
    )(page_table, q, bias_rows, k_all, v_all, *([ck] * pages_per_step), *([cv] * pages_per_step))


def _proj_res_kernel(a_ref, w_ref, x_ref, gt_ref, o_ref):
    o_ref[0] = x_ref[0] + gt_ref[0] * _bdot(a_ref[0], w_ref[...])


def _proj_res(a, w, x, gt, tm):
    bsz, t, d = x.shape
    return pl.pallas_call(
        _proj_res_kernel,
        grid=(bsz, t // tm),
        in_specs=[_tok_spec(tm, d), _const_spec(w), _tok_spec(tm, d), _mod_spec(gt, tm)],
        out_specs=_tok_spec(tm, d),
        out_shape=jax.ShapeDtypeStruct(x.shape, F32),
        compiler_params=_params("parallel", "parallel"),
        name="proj_res",
    )(a, w, x, gt)


def _gdn_out_kernel(o_ref_in, z_ref, on_ref, w_ref, x_ref, gt_ref, o_ref):
    o = o_ref_in[0]
    gmat = _group_matrix(MXU_DIM, B_HEAD_DIM, 1.0 / B_HEAD_DIM)
    ms = _group_reduce(o * o, gmat)
    a = o * lax.rsqrt(ms + NORM_EPS) * on_ref[...] * _silu(z_ref[0])
    o_ref[0] = x_ref[0] + gt_ref[0] * _bdot(a, w_ref[...])


def _gdn_out(o, z, o_norm, w, x, gt, tm):
    bsz, t, d = x.shape
    on = jnp.tile(o_norm, d // B_HEAD_DIM).reshape(1, d)
    return pl.pallas_call(
        _gdn_out_kernel,
        grid=(bsz, t // tm),
        in_specs=[_tok_spec(tm, d), _tok_spec(tm, d), _const_spec(on), _const_spec(w),
                  _tok_spec(tm, d), _mod_spec(gt, tm)],
        out_specs=_tok_spec(tm, d),
        out_shape=jax.ShapeDtypeStruct(x.shape, F32),
        compiler_params=_params("parallel", "parallel"),
        name="gdn_out",
    )(o, z, on, w, x, gt)


def _rwkv_out_kernel(y_ref, r_ref, k_ref, v_ref, gate_ref, rk_ref, lg_ref, lb_ref, w_ref,
                     x_ref, gt_ref, o_ref):
    y = y_ref[0]
    mean_mat = _group_matrix(MXU_DIM, C_HEAD_DIM, 1.0 / C_HEAD_DIM)
    sum_mat = _group_matrix(MXU_DIM, C_HEAD_DIM, 1.0)
    yc = y - _group_reduce(y, mean_mat, split=True)
    var = _group_reduce(yc * yc, mean_mat)
    yn = yc * lax.rsqrt(var + GN_EPS) * lg_ref[...] + lb_ref[...]
    bonus = _group_reduce(r_ref[0] * k_ref[0] * rk_ref[...], sum_mat) * v_ref[0]
    a = (yn + bonus) * gate_ref[0]
    o_ref[0] = x_ref[0] + gt_ref[0] * _bdot(a, w_ref[...])


def _rwkv_out(y, r, k, v, gate, r_k, ln_g, ln_b, w, x, gt, tm):
    bsz, t, d = x.shape
    rk = r_k.reshape(1, d)
    lg = ln_g.reshape(1, d)
    lb = ln_b.reshape(1, d)
    tok = _tok_spec(tm, d)
    return pl.pallas_call(
        _rwkv_out_kernel,
        grid=(bsz, t // tm),
        in_specs=[tok, tok, tok, tok, tok, _const_spec(rk), _const_spec(lg), _const_spec(lb),
                  _const_spec(w), tok, _mod_spec(gt, tm)],
        out_specs=tok,
        out_shape=jax.ShapeDtypeStruct(x.shape, F32),
        compiler_params=_params("parallel", "parallel"),
        name="rwkv_out",
    )(y, r, k, v, gate, rk, lg, lb, w, x, gt)


def _gdn_in_kernel(x_ref, g_ref, sc_ref, sh_ref, w_ref, xqkv_ref, z_ref, ba_ref):
    n_conv = xqkv_ref.shape[2]
    n_z = z_ref.shape[2]
    h = _normmod(x_ref[0], g_ref[...], sc_ref[0], sh_ref[0]).astype(BF16)
    for s in range(0, n_conv, MXU_DIM):
        xqkv_ref[0, :, s:s + MXU_DIM] = jnp.dot(h, w_ref[:, s:s + MXU_DIM],
                                                preferred_element_type=F32)
    for s in range(0, n_z, MXU_DIM):
        z_ref[0, :, s:s + MXU_DIM] = jnp.dot(h, w_ref[:, n_conv + s:n_conv + s + MXU_DIM],
                                             preferred_element_type=F32)
    ba_ref[0] = jnp.dot(h, w_ref[:, n_conv + n_z:], preferred_element_type=F32)


def _gdn_in(x, g, sc, sh, w_pad, n_conv, tm):
    bsz, t, d = x.shape
    return pl.pallas_call(
        _gdn_in_kernel,
        grid=(bsz, t // tm),
        in_specs=[_tok_spec(tm, d), _const_spec(g), _mod_spec(sc, tm), _mod_spec(sh, tm),
                  _const_spec(w_pad)],
        out_specs=[_tok_spec(tm, n_conv), _tok_spec(tm, d), _tok_spec(tm, LANES)],
        out_shape=[jax.ShapeDtypeStruct((bsz, t, n_conv), F32),
                   jax.ShapeDtypeStruct((bsz, t, d), F32),
                   jax.ShapeDtypeStruct((bsz, t, LANES), F32)],
        compiler_params=_params("parallel", "parallel"),
        name="gdn_in",
    )(x, g, sc, sh, w_pad)


def _gdn_conv_kernel(*refs, pre_shifted):
    if pre_shifted:
        x_ref, xs_ref, cw_ref, ba_ref, alog_ref, dtb_ref = refs[:6]
    else:
        x_ref, p8_ref, cw_ref, ba_ref, alog_ref, dtb_ref = refs[:6]
    q_ref, k_ref, kb_ref, vb_ref, g_ref = refs[6:]
    i = pl.program_id(1)
    tm = x_ref.shape[1]
    hd = q_ref.shape[2]
    n_heads = hd // B_HEAD_DIM
    x = x_ref[0]
    conv = x * cw_ref[B_CONV - 1:B_CONV, :]
    if pre_shifted:
        for s in range(1, B_CONV):
            conv = conv + xs_ref[s - 1, 0] * cw_ref[B_CONV - 1 - s:B_CONV - s, :]
    else:
        halo = jnp.where(i > 0, p8_ref[0], 0.0)
        full = jnp.concatenate([halo, x], axis=0)
        for s in range(1, B_CONV):
            conv = conv + pltpu.roll(full, s, 0)[8:] * cw_ref[B_CONV - 1 - s:B_CONV - s, :]
    act = _silu(conv)
    r = lax.broadcasted_iota(jnp.int32, (LANES, hd), 0)
    c_head = lax.broadcasted_iota(jnp.int32, (LANES, hd), 1) // B_HEAD_DIM
    e_b = (r == c_head).astype(BF16)
    e_a = (r == c_head + n_heads).astype(BF16)
    ba = ba_ref[0]
    beta = _sigmoid(_dot_xe(ba, e_b))
    g = -jnp.exp(alog_ref[...]) * _softplus(_dot_xe(ba, e_a) + dtb_ref[...])
    g_ref[0] = g
    ones = _group_matrix(MXU_DIM, B_HEAD_DIM, 1.0)
    aq = act[:, :hd]
    ak = act[:, hd:2 * hd]
    q_ref[0] = aq * lax.rsqrt(_group_reduce(aq * aq, ones) + L2_EPS) * (B_HEAD_DIM ** -0.5)
    k = ak * lax.rsqrt(_group_reduce(ak * ak, ones) + L2_EPS)
    k_ref[0] = k
    kb_ref[0] = k * beta
    vb_ref[0] = act[:, 2 * hd:] * beta


def _gdn_conv(xqkv, xs, conv_w, ba, a_log, dt_bias, hd, tm):
    bsz, t, n_conv = xqkv.shape
    alog = jnp.repeat(a_log, B_HEAD_DIM).reshape(1, hd)
    dtb = jnp.repeat(dt_bias, B_HEAD_DIM).reshape(1, hd)
    pre_shifted = xs is not None
    if pre_shifted:
        second = xs
        second_spec = pl.BlockSpec((B_CONV - 1, 1, tm, n_conv), lambda b, i: (0, b, i, 0))
    else:
        second = xqkv
        second_spec = _prev8_spec(tm, n_conv)
    shp = jax.ShapeDtypeStruct((bsz, t, hd), F32)
    return pl.pallas_call(
        functools.partial(_gdn_conv_kernel, pre_shifted=pre_shifted),
        grid=(bsz, t // tm),
        in_specs=[_tok_spec(tm, n_conv), second_spec, _const_spec(conv_w), _tok_spec(tm, LANES),
                  _const_spec(alog), _const_spec(dtb)],
        out_specs=[_tok_spec(tm, hd)] * 5,
        out_shape=[shp] * 5,
        compiler_params=_params("parallel", "parallel"),
        name="gdn_conv",
    )(xqkv, second, conv_w, ba, alog, dtb)


def _gdn_chunk_kernel(q_ref, k_ref, kb_ref, vb_ref, g_ref, s0_ref, o_ref, s_out_ref, s_ref, *,
                      n_heads, n_chunks):
    i = pl.program_id(2)
    c = CHUNK
    hw = B_HEAD_DIM

    @pl.when(i == 0)
    def _():
        s_ref[...] = s0_ref[0]

    row = lax.broadcasted_iota(jnp.int32, (c, c), 0)
    col = lax.broadcasted_iota(jnp.int32, (c, c), 1)
    lower = row >= col
    strict = row > col
    lower_b = lower.astype(BF16)
    ones_avg = jnp.full((c, LANES), 1.0 / LANES, BF16)

    def row_layout(g_cb):
        gh, gl = _split(g_cb)
        nt = lambda x: lax.dot_general(ones_avg, x, (((1,), (1,)), ((), ())),
                                       preferred_element_type=F32)
        return nt(gh) + nt(gl)

    sls = [(slice(n * c, (n + 1) * c), slice(h * hw, (h + 1) * hw))
           for h in range(n_heads) for n in range(n_chunks)]
    rng = range(len(sls))
    q = [q_ref[0, rs, ls] for rs, ls in sls]
    k = [k_ref[0, rs, ls] for rs, ls in sls]
    kb = [kb_ref[0, rs, ls] for rs, ls in sls]
    vb = [vb_ref[0, rs, ls] for rs, ls in sls]
    g_cb = [_dot_ex(lower_b, g_ref[0, rs, ls]) for rs, ls in sls]
    g_row = [row_layout(x) for x in g_cb]
    decay = [jnp.where(lower, jnp.exp(jnp.where(lower, g_cb[n][:, :c] - g_row[n], 0.0)), 0.0)
             for n in rng]
    a = [jnp.where(strict, decay[n] * _bdot_nt(kb[n], k[n]), 0.0) for n in rng]
    t_inv = _tri_inverses(a, 5)
    e_g = [jnp.exp(x) for x in g_cb]
    wu = [_dot3(t_inv[n], jnp.concatenate([kb[n] * e_g[n], vb[n]], axis=1)) for n in rng]
    qk = [decay[n] * _bdot_nt(q[n], k[n]) for n in rng]
    g_last = [x[c - 1:c, :] for x in g_cb]
    k_end = [jnp.exp(g_last[n] - g_cb[n]) * k[n] for n in rng]
    qw = [_bdot(qk[n], wu[n]) for n in rng]
    q_eff = [q[n] * e_g[n] - qw[n][:, :hw] for n in rng]
    o0 = [x[:, hw:] for x in qw]
    mn = [_bdot_tn(k_end[n], wu[n]) for n in rng]
    m0 = [x[:, :hw] for x in mn]
    n0 = [x[:, hw:] for x in mn]

    states = [s_ref[h] for h in range(n_heads)]
    for m in range(n_chunks):
        for h in range(n_heads):
            n = h * n_chunks + m
            rs, ls = sls[n]
            s = states[h]
            o_ref[0, rs, ls] = _bdot(q_eff[n], s) + o0[n]
            states[h] = jnp.exp(g_last[n]) * s - _bdot(m0[n], s) + n0[n]
    for h in range(n_heads):
        s_ref[h] = states[h]
        s_out_ref[0, h] = states[h]


def _gdn_chunk(q, k, kb, vb, g, s0, ct, heads_per_step):
    bsz, t, hd = q.shape
    hw = B_HEAD_DIM
    blk = pl.BlockSpec((1, ct, heads_per_step * hw), lambda b, h, i: (b, i, h))
    st = pl.BlockSpec((1, heads_per_step, hw, hw), lambda b, h, i: (b, h, 0, 0))
    return pl.pallas_call(
        functools.partial(_gdn_chunk_kernel, n_heads=heads_per_step, n_chunks=ct // CHUNK),
        grid=(bsz, hd // (heads_per_step * hw), t // ct),
        in_specs=[blk] * 5 + [st],
        out_specs=[blk, st],
        out_shape=[jax.ShapeDtypeStruct(q.shape, F32), jax.ShapeDtypeStruct(s0.shape, F32)],
        scratch_shapes=[pltpu.VMEM((heads_per_step, hw, hw), F32)],
        compiler_params=_params("parallel", "parallel", "arbitrary"),
        name="gdn_chunk",
    )(q, k, kb, vb, g, s0)


def _rwkv_in_kernel(*refs, seq_len, per_row_shift):
    (x_ref, p8_ref, g_ref, sc_ref, sh_ref, sh0_ref, mu_ref, wrkv_ref, w1_ref, w2_ref, a1_ref, a2_ref,
     g1_ref, g2_ref, w0_ref, a0_ref, kk_ref, ka_ref) = refs[:18]
    r_ref, lw_ref, k_ref, v_ref, kn_ref, bb_ref, gate_ref, hl_ref = refs[18:]
    i = pl.program_id(1)
    tm = x_ref.shape[1]
    g = g_ref[...]
    h = _normmod(x_ref[0], g, sc_ref[0], sh_ref[0])
    if per_row_shift:
        sc8, sh8 = sc_ref[0, 0:8], sh_ref[0, 0:8]
    else:
        sc8, sh8 = sc_ref[0], sh_ref[0]
    h8 = _normmod(p8_ref[0], g, sc8, sh8)
    prev = pltpu.roll(jnp.concatenate([h8, h], axis=0), 1, 0)[8:]
    t_glob = i * tm + lax.broadcasted_iota(jnp.int32, (tm, 1), 0)
    prev = jnp.where(t_glob % seq_len == 0, sh0_ref[0], prev)
    xx = prev - h
    mix = lambda n: (h + xx * mu_ref[n:n + 1, :]).astype(BF16)
    r = jnp.dot(mix(0), wrkv_ref[0], preferred_element_type=F32)
    k = jnp.dot(mix(1), wrkv_ref[1], preferred_element_type=F32)
    v_ref[0] = jnp.dot(mix(2), wrkv_ref[2], preferred_element_type=F32)
    r_ref[0] = r
    wl = w0_ref[...] + _bdot(jnp.tanh(jnp.dot(mix(3), w1_ref[...], preferred_element_type=F32)),
                             w2_ref[...])
    lw_ref[0] = -jnp.exp(-_softplus(-wl) - 0.5)
    a = _sigmoid(a0_ref[...] + _bdot(jnp.dot(mix(4), a1_ref[...], preferred_element_type=F32),
                                     a2_ref[...]))
    gate_ref[0] = _bdot(_sigmoid(jnp.dot(mix(5), g1_ref[...], preferred_element_type=F32)),
                        g2_ref[...])
    kx = k * kk_ref[...]
    ones = _group_matrix(MXU_DIM, C_HEAD_DIM, 1.0)
    kn = kx * lax.rsqrt(_group_reduce(kx * kx, ones) + L2_EPS)
    kn_ref[0] = kn
    bb_ref[0] = kn * a
    k_ref[0] = k * (1.0 + (a - 1.0) * ka_ref[...])
    hl_ref[0, 0] = h[tm - 8:tm]


def _rwkv_in(x, g, sc, sh, shift0, seq_len, P, tm):
    bsz, t, d = x.shape
    per_row = shift0.shape[1] != 1
    consts = [P['mu'], P['w_rkv'], P['w1'], P['w2'], P['a1'], P['a2'], P['g1'], P['g2'],
              P['w0'], P['a0'], P['k_k'], P['k_a']]
    shp = jax.ShapeDtypeStruct(x.shape, F32)
    n_t = t // tm
    outs = pl.pallas_call(
        functools.partial(_rwkv_in_kernel, seq_len=seq_len, per_row_shift=per_row),
        grid=(bsz, n_t),
        in_specs=[_tok_spec(tm, d), _prev8_spec(tm, d), _const_spec(g), _mod_spec(sc, tm),
                  _mod_spec(sh, tm), _mod_spec(shift0, tm)] + [_const_spec(c) for c in consts],
        out_specs=[_tok_spec(tm, d)] * 7 + [pl.BlockSpec((1, 1, 8, d), lambda b, i: (b, i, 0, 0))],
        out_shape=[shp] * 7 + [jax.ShapeDtypeStruct((bsz, n_t, 8, d), F32)],
        compiler_params=_params("parallel", "parallel"),
        name="rwkv_in",
    )(x, x, g, sc, sh, shift0, *consts)
    return outs


def _rwkv_scan_kernel(r_ref, lw_ref, k_ref, v_ref, kn_ref, bb_ref, s0_ref, y_ref, s_out_ref, s_ref,
                      *, n_pairs, n_chunks):
    i = pl.program_id(2)
    c = CHUNK
    n = 2 * c

    @pl.when(i == 0)
    def _():
        s_ref[...] = s0_ref[0]

    lane = lax.broadcasted_iota(jnp.int32, (1, LANES), 1)
    row = lax.broadcasted_iota(jnp.int32, (n, n), 0)
    col = lax.broadcasted_iota(jnp.int32, (n, n), 1)
    same = (row // c) == (col // c)
    incl = same & (row >= col)
    strict = same & (row > col)
    incl_b = incl.astype(BF16)

    def stack(x):
        return jnp.concatenate([jnp.where(lane < C_HEAD_DIM, x, 0.0),
                                jnp.where(lane >= C_HEAD_DIM, x, 0.0)], axis=0)

    sls = [(slice(m * c, (m + 1) * c), slice(p * LANES, (p + 1) * LANES))
           for p in range(n_pairs) for m in range(n_chunks)]
    rng = range(len(sls))
    lw = [stack(lw_ref[0, rs, ls]) for rs, ls in sls]
    lc = [_dot_ex(incl_b, x) for x in lw]
    lc_last = [x[c - 1:c, :] + x[n - 1:n, :] for x in lc]
    p_inv = [jnp.exp(-x) for x in lc]
    p_end = [jnp.exp(lc_last[m] - lc[m]) for m in rng]
    kn_t = [stack(kn_ref[0, sls[m][0], sls[m][1]]) * jnp.exp(lc[m] - lw[m]) for m in rng]
    k_s = [stack(k_ref[0, rs, ls]) for rs, ls in sls]
    b_s = [stack(bb_ref[0, rs, ls]) for rs, ls in sls]
    v_s = [stack(v_ref[0, rs, ls]) for rs, ls in sls]
    k_t = [k_s[m] * p_inv[m] for m in rng]
    b_t = [b_s[m] * p_inv[m] for m in rng]
    r_t = [stack(r_ref[0, sls[m][0], sls[m][1]]) * jnp.exp(lc[m]) for m in rng]
    bk_t = [jnp.concatenate([b_t[m], k_t[m]], axis=0) for m in rng]
    a_k = [_bdot_nt(kn_t[m], bk_t[m]) for m in rng]
    a_r = [_bdot_nt(r_t[m], bk_t[m]) for m in rng]
    a_kb = [jnp.where(strict, x[:, :n], 0.0) for x in a_k]
    a_kk = [jnp.where(strict, x[:, n:], 0.0) for x in a_k]
    a_rb = [jnp.where(incl, x[:, :n], 0.0) for x in a_r]
    a_rk = [jnp.where(incl, x[:, n:], 0.0) for x in a_r]
    t_inv = _tri_inverses(a_kb, 5)
    ku = [_dot3(t_inv[m], jnp.concatenate([kn_t[m], _bdot(a_kk[m], v_s[m])], axis=1))
          for m in rng]
    kt = [x[:, :LANES] for x in ku]
    u0 = [x[:, LANES:] for x in ku]
    k_e = [k_s[m] * p_end[m] for m in rng]
    b_e = [b_s[m] * p_end[m] for m in rng]
    g_mat = [_bdot_tn(kt[m], b_e[m]) for m in rng]
    n0 = [_bdot_tn(v_s[m], k_e[m]) - _bdot_tn(u0[m], b_e[m]) for m in rng]
    rb = [_bdot(a_rb[m], ku[m]) for m in rng]
    r_eff = [r_t[m] - rb[m][:, :LANES] for m in rng]
    y0 = [_bdot(a_rk[m], v_s[m]) - rb[m][:, LANES:] for m in rng]

    states = [s_ref[p] for p in range(n_pairs)]
    for j in range(n_chunks):
        for p in range(n_pairs):
            m = p * n_chunks + j
            rs, ls = sls[m]
            s = states[p]
            y = _bdot_nt(r_eff[m], s) + y0[m]
            y_ref[0, rs, ls] = y[:c] + y[c:]
            states[p] = s * jnp.exp(lc_last[m]) - _bdot(s, g_mat[m]) + n0[m]
    for p in range(n_pairs):
        s_ref[p] = states[p]
        s_out_ref[0, p] = states[p]


def _rwkv_scan(r, lw, k, v, kn, bb, s0_pairs, ct, pairs_per_step):
    bsz, t, d = r.shape
    blk = pl.BlockSpec((1, ct, pairs_per_step * LANES), lambda b, p, i: (b, i, p))
    st = pl.BlockSpec((1, pairs_per_step, LANES, LANES), lambda b, p, i: (b, p, 0, 0))
    return pl.pallas_call(
        functools.partial(_rwkv_scan_kernel, n_pairs=pairs_per_step, n_chunks=ct // CHUNK),
        grid=(bsz, d // (pairs_per_step * LANES), t // ct),
        in_specs=[blk] * 6 + [st],
        out_specs=[blk, st],
        out_shape=[jax.ShapeDtypeStruct(r.shape, F32), jax.ShapeDtypeStruct(s0_pairs.shape, F32)],
        scratch_shapes=[pltpu.VMEM((pairs_per_step, LANES, LANES), F32)],
        compiler_params=_params("parallel", "parallel", "arbitrary"),
        name="rwkv_scan",
    )(r, lw, k, v, kn, bb, s0_pairs)


def _chain_split(t_pad, n_heads):
    heads = min(n_heads, HEADS_PER_STEP)
    ct = min(t_pad, CHAINS_PER_STEP // heads * CHUNK)
    heads = max(heads, min(n_heads, CHAINS_PER_STEP * CHUNK // ct))
    return ct, heads


def _pad_time(a, t_pad):
    return jnp.pad(a, ((0, 0), (0, t_pad - a.shape[1]), (0, 0)))


def _pairs_from_heads(s):
    bsz, n_h, n, _ = s.shape
    sp = s.reshape(bsz, n_h // 2, 2, n, n)
    z = jnp.zeros_like(sp[:, :, 0])
    top = jnp.concatenate([sp[:, :, 0], z], axis=-1)
    bot = jnp.concatenate([z, sp[:, :, 1]], axis=-1)
    return jnp.concatenate([top, bot], axis=-2)


def _heads_from_pairs(sp):
    bsz, n_p, n2, _ = sp.shape
    n = n2 // 2
    return jnp.stack([sp[:, :, :n, :n], sp[:, :, n:, n:]], axis=2).reshape(bsz, 2 * n_p, n, n)


def _run_trunk(x, mods, W, tm, seq_len, cache, state):
    bsz, t, d = x.shape
    n_seq = bsz * t // seq_len
    depth = len(mods)
    kv_all, bconv, bssm, cshift, cwkv = None, [], [], [], []
    n_a = W['a_w_qkv'].shape[0]
    for l in range(depth):
        kind, j = l % 3, l // 3
        sh1, sc1, gt1, sh2, sc2, gt2 = mods[l]
        g_mix = W['ln_mix'][l].reshape(1, d)
        if kind == 0:
            q, k_all, v_all = _qkv(x, g_mix, sc1, sh1, W['a_w_qkv'], j, W['a_q_norm'][j],
                                   W['a_k_norm'][j], kv_all, tm)
            kv_all = (k_all, v_all)
            if cache is None:
                o = _sb_prompt(q, k_all, v_all, j, W['a_logit_bias'][j],
                               qb=SB_QUERY_BLOCK, kb=SB_KEY_BLOCK)
            else:
                cache_k, cache_v, page_table = cache
                o = _sb_decode(q.reshape(n_seq, seq_len, d), k_all.reshape(n_a, n_seq, seq_len, d),
                               v_all.reshape(n_a, n_seq, seq_len, d), j, cache_k, cache_v, j,
                               page_table, W['a_logit_bias'][j],
                               pages_per_step=min(8, page_table.shape[1])).reshape(bsz, t, d)
            x = _proj_res(o, W['a_w_o'][j], x, gt1, tm)
        elif kind == 1:
            n_conv = 3 * d
            xqkv, z, ba = _gdn_in(x, g_mix, sc1, sh1, W['b_w_in'][j], n_conv, tm)
            xq_seq = xqkv.reshape(n_seq, seq_len, n_conv)
            if state is None:
                xs = None
                bconv.append(xq_seq[:, seq_len - (B_CONV - 1):])
                s0 = jnp.zeros((n_seq, d // B_HEAD_DIM, B_HEAD_DIM, B_HEAD_DIM), F32)
            else:
                xc = jnp.concatenate([state['b_conv'][j], xq_seq], axis=1)
                xs = jnp.stack([xc[:, B_CONV - 1 - s:B_CONV - 1 - s + seq_len].reshape(bsz, t, n_conv)
                                for s in range(1, B_CONV)])
                bconv.append(xc[:, seq_len:])
                s0 = state['b_ssm'][j]
            q, k, kb, vb, g = _gdn_conv(xqkv, xs, W['b_conv'][j], ba, W['b_a_log'][j],
                                        W['b_dt_bias'][j], d, min(tm, 256))
            t_pad = -(-seq_len // CHUNK) * CHUNK
            seqs = [_pad_time(a.reshape(n_seq, seq_len, d), t_pad) for a in (q, k, kb, vb, g)]
            ct, heads = _chain_split(t_pad, d // B_HEAD_DIM)
            o, s_new = _gdn_chunk(*seqs, s0, ct=ct, heads_per_step=heads)
            bssm.append(s_new)
            o = o[:, :seq_len].reshape(bsz, t, d)
            x = _gdn_out(o, z, W['b_o_norm'][j], W['b_w_o'][j], x, gt1, tm)
        else:
            if state is None:
                shift0 = jnp.zeros((bsz, 1, d), F32)
                s0 = jnp.zeros((n_seq, d // C_HEAD_DIM, C_HEAD_DIM, C_HEAD_DIM), F32)
            else:
                shift0 = jnp.repeat(state['c_shift'][j], seq_len, axis=0).reshape(bsz, t, d)
                s0 = state['c_wkv'][j]
            tm_c = min(tm, 512)
            r, lw, k, v, kn, bb, gate, h_last = _rwkv_in(x, g_mix, sc1, sh1, shift0, seq_len,
                                                        W['c'][j], tm_c)
            if state is None:
                cshift.append(h_last[:, -1, 7])
            else:
                assert t == tm_c and seq_len == 8
                cshift.append(_last_rows(x, g_mix, sc1, sh1, seq_len))
            t_pad = -(-seq_len // CHUNK) * CHUNK
            seqs = [_pad_time(a.reshape(n_seq, seq_len, d), t_pad) for a in (r, lw, k, v, kn, bb)]
            ct, pairs = _chain_split(t_pad, d // LANES)
            y, s_new = _rwkv_scan(*seqs, _pairs_from_heads(s0), ct=ct, pairs_per_step=pairs)
            cwkv.append(_heads_from_pairs(s_new))
            y = y[:, :seq_len].reshape(bsz, t, d)
            x = _rwkv_out(y, r, k, v, gate, W['c_r_k'][j], W['c_ln_g'][j], W['c_ln_b'][j],
                          W['c_w_o'][j], x, gt1, tm)
        x = _mlp(x, W['ln_mlp'][l].reshape(1, d), sc2, sh2, gt2, W['w_up'], W['w_down'], l,
                 MLP_TOKEN_TILE if t % MLP_TOKEN_TILE == 0 else tm)
    heads = (n_a, n_seq, seq_len, d // A_HEAD_DIM, A_HEAD_DIM)
    return x, kv_all[0].reshape(heads), kv_all[1].reshape(heads), bconv, bssm, cshift, cwkv


def _hmod_kernel(x_ref, g_ref, sc_ref, sh_ref, o_ref):
    o_ref[0] = _normmod(x_ref[0], g_ref[...], sc_ref[0], sh_ref[0])


def _last_rows(x, g, sc, sh, seq_len):
    bsz, t, d = x.shape
    n_seq = bsz * t // seq_len
    pick = lambda a: a.reshape(n_seq, seq_len, d)[:, seq_len - 1].reshape(1, n_seq, d)
    xs, scs, shs = pick(x), pick(sc), pick(sh)
    return pl.pallas_call(
        _hmod_kernel,
        grid=(1, 1),
        in_specs=[_tok_spec(n_seq, d), _const_spec(g), _tok_spec(n_seq, d), _tok_spec(n_seq, d)],
        out_specs=_tok_spec(n_seq, d),
        out_shape=jax.ShapeDtypeStruct((1, n_seq, d), F32),
        compiler_params=_params("parallel", "parallel"),
        name="last_rows",
    )(xs, g, scs, shs)[0]


def kernel(x_prompt, x_sample, c_prompt, c_sample, cache_k, cache_v, page_table, state_b_conv, state_b_ssm, state_c_shift, state_c_wkv, ln_mix, ln_mlp, w_ada, b_ada, w_up, w_down, a_w_qkv, a_q_norm, a_k_norm, a_logit_bias, a_w_o, b_w_in, b_conv, b_a_log, b_dt_bias, b_o_norm, b_w_o, c_mu, c_w_rkv, c_w0, c_w1, c_w2, c_a0, c_a1, c_a2, c_g1, c_g2, c_k_k, c_k_a, c_r_k, c_ln_g, c_ln_b, c_w_o):
    bp, t_p, d = x_prompt.shape
    bs, t_s, _ = x_sample.shape
    depth = ln_mix.shape[0]
    n_c = c_mu.shape[0]

    def lora_cols(w):
        return jnp.pad(w, ((0, 0), (0, 0), (0, LANES - w.shape[2]))).astype(BF16)

    def lora_rows(w):
        return jnp.pad(w, ((0, 0), (0, LANES - w.shape[1]), (0, 0))).astype(BF16)

    b_in = b_w_in.shape[2]
    b_in_pad = -(-b_in // LANES) * LANES
    W = {
        'ln_mix': ln_mix, 'ln_mlp': ln_mlp,
        'w_up': w_up.astype(BF16), 'w_down': w_down.astype(BF16),
        'a_w_qkv': a_w_qkv.astype(BF16), 'a_q_norm': a_q_norm, 'a_k_norm': a_k_norm,
        'a_logit_bias': a_logit_bias, 'a_w_o': a_w_o.astype(BF16),
        'b_w_in': jnp.pad(b_w_in, ((0, 0), (0, 0), (0, b_in_pad - b_in))).astype(BF16),
        'b_conv': b_conv, 'b_a_log': b_a_log, 'b_dt_bias': b_dt_bias, 'b_o_norm': b_o_norm,
        'b_w_o': b_w_o.astype(BF16),
        'c_r_k': c_r_k.reshape(n_c, d), 'c_ln_g': c_ln_g, 'c_ln_b': c_ln_b, 'c_w_o': c_w_o.astype(BF16),
    }
    w1, w2 = lora_cols(c_w1), lora_rows(c_w2)
    a1, a2 = lora_cols(c_a1), lora_rows(c_a2)
    g1, g2 = c_g1.astype(BF16), c_g2.astype(BF16)
    rkv = c_w_rkv.astype(BF16)
    W['c'] = [{'mu': c_mu[j], 'w_rkv': rkv[j], 'w1': w1[j], 'w2': w2[j], 'a1': a1[j], 'a2': a2[j],
               'g1': g1[j], 'g2': g2[j], 'w0': c_w0[j].reshape(1, d), 'a0': c_a0[j].reshape(1, d),
               'k_k': c_k_k[j].reshape(1, d), 'k_a': c_k_a[j].reshape(1, d)} for j in range(n_c)]

    mod_all = _ada(jnp.concatenate([c_prompt, c_sample], axis=0), w_ada, b_ada)
    mod_p = mod_all[:, :, :bp].reshape(depth, 6, bp, 1, d)
    mod_s = jnp.repeat(mod_all[:, :, bp:], t_s, axis=2).reshape(depth, 6, 1, bs * t_s, d)
    mods_p = [[mod_p[l, n] for n in range(6)] for l in range(depth)]
    mods_s = [[mod_s[l, n] for n in range(6)] for l in range(depth)]

    y_p, k_p, v_p, bconv_p, bssm_p, cshift_p, cwkv_p = _run_trunk(
        x_prompt, mods_p, W, 512, t_p, None, None)
    state = {'b_conv': state_b_conv, 'b_ssm': state_b_ssm, 'c_shift': state_c_shift,
             'c_wkv': state_c_wkv}
    y_s, k_s, v_s, bconv_s, bssm_s, cshift_s, cwkv_s = _run_trunk(
        x_sample.reshape(1, bs * t_s, d), mods_s, W, bs * t_s, t_s,
        (cache_k, cache_v, page_table), state)
    st = jnp.stack
    return (y_p, y_s.reshape(bs, t_s, d), k_p, v_p, k_s, v_s, st(bconv_p), st(bssm_p),
            st(bconv_s), st(bssm_s), st(cshift_p), st(cwkv_p), st(cshift_s), st(cwkv_s))
```

```python
import functools

import jax
import jax.numpy as jnp
from jax import lax
from jax.experimental import pallas as pl
from jax.experimental.pallas import tpu as pltpu

F32 = jnp.float32
BF16 = jnp.bfloat16

LANES = 128
MXU_DIM = 256
VMEM_LIMIT_BYTES = 56 * 1024 * 1024

NORM_EPS = 1e-6
L2_EPS = 1e-6
GN_EPS = 64e-5
LOG2E = 1.4426950408889634
A_HEAD_DIM = 64
B_HEAD_DIM = 128
C_HEAD_DIM = 64
B_CONV = 4
CHUNK = 64
CHAINS_PER_STEP = 16
HEADS_PER_STEP = 4
SB_QUERY_BLOCK = 512
SB_KEY_BLOCK = MXU_DIM
MLP_TOKEN_TILE = 1024


def _params(*sem):
    return pltpu.CompilerParams(dimension_semantics=sem, vmem_limit_bytes=VMEM_LIMIT_BYTES)


def _bdot(a, b):
    return jnp.dot(a.astype(BF16), b.astype(BF16), preferred_element_type=F32)


def _bdot_nt(a, b):
    return lax.dot_general(a.astype(BF16), b.astype(BF16), (((1,), (1,)), ((), ())),
                           preferred_element_type=F32)


def _bdot_tn(a, b):
    return lax.dot_general(a.astype(BF16), b.astype(BF16), (((0,), (0,)), ((), ())),
                           preferred_element_type=F32)


def _split(x):
    hi = x.astype(BF16)
    lo = (x - hi.astype(F32)).astype(BF16)
    return hi, lo


def _sum_dots(lhs, rhs):
    return jnp.dot(jnp.concatenate(lhs, axis=1), jnp.concatenate(rhs, axis=0),
                   preferred_element_type=F32)


def _dot_xe(x, e):
    hi, lo = _split(x)
    return _sum_dots([hi, lo], [e, e])


def _dot_ex(e, x):
    hi, lo = _split(x)
    return _sum_dots([e, e], [hi, lo])


def _dot3_parts(a_parts, b_parts):
    ah, al = a_parts
    bh, bl = b_parts
    return _sum_dots([ah, ah, al], [bh, bl, bh])


def _dot3(a, b):
    return _dot3_parts(_split(a), _split(b))


def _sigmoid(x):
    return 1.0 / (1.0 + jnp.exp(-x))


def _silu(x):
    return x * _sigmoid(x)


def _softplus(x):
    return jnp.maximum(x, 0.0) + jnp.log(1.0 + jnp.exp(-jnp.abs(x)))


def _neg_abs(x):
    bits = lax.bitcast_convert_type(x, jnp.uint32) | jnp.uint32(0x80000000)
    return lax.bitcast_convert_type(bits, F32)


def _sb_suffix(z, suffix2, mask):
    sp = jnp.maximum(z, 0.0) + jnp.log(1.0 + jnp.exp2(_neg_abs(z))) * LOG2E
    if mask is not None:
        sp = jnp.where(mask, sp, 0.0)
    hi, lo = _split(sp)
    return jnp.dot(jnp.concatenate([hi, lo], axis=1), suffix2, preferred_element_type=F32)


def _sb_local(z, s_incl, mask):
    w = jnp.exp2(z - s_incl)
    if mask is not None:
        w = jnp.where(mask, w, 0.0)
    return w.astype(BF16)


def _sb_weights(z, suffix2, mask):
    s_incl = _sb_suffix(z, suffix2, mask)
    return _sb_local(z, s_incl, mask), s_incl[:, 0:1]


def _normmod(x, g, sc, sh):
    ms = jnp.mean(x * x, axis=-1, keepdims=True)
    return x * lax.rsqrt(ms + NORM_EPS) * g * (1.0 + sc) + sh


def _group_matrix(n, group, value):
    r = lax.broadcasted_iota(jnp.int32, (n, n), 0) // group
    c = lax.broadcasted_iota(jnp.int32, (n, n), 1) // group
    return jnp.where(r == c, value, 0.0).astype(BF16)


def _group_reduce(x, gmat, split=False):
    n = gmat.shape[0]
    dot = _dot_xe if split else _bdot
    parts = [dot(x[:, s:s + n], gmat) for s in range(0, x.shape[1], n)]
    return parts[0] if len(parts) == 1 else jnp.concatenate(parts, axis=1)


def _tri_inverses(mats, n_steps):
    n = mats[0].shape[0]
    eye = (lax.broadcasted_iota(jnp.int32, (n, n), 0)
           == lax.broadcasted_iota(jnp.int32, (n, n), 1)).astype(F32)
    ps = [-a for a in mats]
    ts = [eye + p for p in ps]
    parts = [_split(p) for p in ps]
    for _ in range(n_steps):
        parts = [_split(_dot3_parts(s, s)) for s in parts]
        ts = [t + _dot3_parts(_split(t), s) for t, s in zip(ts, parts)]
    return ts


def _ada_kernel(c_ref, w_ref, b_ref, o_ref):
    ca = _silu(c_ref[...])
    o_ref[0, 0] = _bdot(ca, w_ref[0]) + b_ref[0]


def _ada(c_all, w_ada, b_ada):
    depth, d, n = w_ada.shape
    rows = c_all.shape[0]
    return pl.pallas_call(
        _ada_kernel,
        grid=(depth, n // d),
        in_specs=[pl.BlockSpec((rows, d), lambda l, j: (0, 0)),
                  pl.BlockSpec((1, d, d), lambda l, j: (l, 0, j)),
                  pl.BlockSpec((1, 1, d), lambda l, j: (l, 0, j))],
        out_specs=pl.BlockSpec((1, 1, rows, d), lambda l, j: (l, j, 0, 0)),
        out_shape=jax.ShapeDtypeStruct((depth, n // d, rows, d), F32),
        compiler_params=_params("parallel", "parallel"),
        name="ada",
    )(c_all, w_ada, b_ada.reshape(depth, 1, n))


def _tok_spec(tm, width):
    return pl.BlockSpec((1, tm, width), lambda b, i: (b, i, 0))


def _mod_spec(mod, tm):
    if mod.shape[1] == 1:
        return pl.BlockSpec((1, 1, mod.shape[2]), lambda b, i: (b, 0, 0))
    return pl.BlockSpec((1, tm, mod.shape[2]), lambda b, i: (b, i, 0))


def _const_spec(arr):
    nd = arr.ndim
    return pl.BlockSpec(arr.shape, lambda b, i: (0,) * nd, pipeline_mode=pl.Buffered(1))


def _prev8_spec(tm, width):
    blocks = tm // 8
    return pl.BlockSpec((1, 8, width), lambda b, i: (b, jnp.maximum(i * blocks - 1, 0), 0))


def _mlp_kernel(x_ref, g_ref, sc_ref, sh_ref, gt_ref, wu_ref, wd_ref, o_ref, *, n_split, tf):
    tm = x_ref.shape[1]
    ff = wu_ref.shape[2]
    rows = [slice(r * tm // n_split, (r + 1) * tm // n_split) for r in range(n_split)]

    def mod_rows(ref, sl):
        m = ref[0]
        return m if m.shape[0] == 1 else m[sl]

    hs = [_normmod(x_ref[0, sl, :], g_ref[...], mod_rows(sc_ref, sl), mod_rows(sh_ref, sl)).astype(BF16)
          for sl in rows]
    accs = [None] * n_split
    for f in range(0, ff, tf):
        for r in range(n_split):
            u = jnp.maximum(jnp.dot(hs[r], wu_ref[0, :, f:f + tf], preferred_element_type=F32), 0.0)
            part = jnp.dot((u * u).astype(BF16), wd_ref[0, f:f + tf, :], preferred_element_type=F32)
            accs[r] = part if accs[r] is None else accs[r] + part
    for r, sl in enumerate(rows):
        o_ref[0, sl, :] = x_ref[0, sl, :] + mod_rows(gt_ref, sl) * accs[r]


def _mlp(x, g, sc, sh, gt, w_up, w_down, layer, tm):
    bsz, t, d = x.shape
    ff = w_up.shape[2]
    resident = dict(pipeline_mode=pl.Buffered(1))
    return pl.pallas_call(
        functools.partial(_mlp_kernel, n_split=2, tf=1024),
        grid=(bsz, t // tm),
        in_specs=[_tok_spec(tm, d), _const_spec(g),
                  _mod_spec(sc, tm), _mod_spec(sh, tm), _mod_spec(gt, tm),
                  pl.BlockSpec((1, d, ff), lambda b, i: (layer, 0, 0), **resident),
                  pl.BlockSpec((1, ff, d), lambda b, i: (layer, 0, 0), **resident)],
        out_specs=_tok_spec(tm, d),
        out_shape=jax.ShapeDtypeStruct(x.shape, F32),
        compiler_params=_params("parallel", "parallel"),
        name="mlp",
    )(x, g, sc, sh, gt, w_up, w_down)


def _qkv_kernel(x_ref, g_ref, sc_ref, sh_ref, w_ref, qg_ref, kg_ref, *rest):
    q_ref, k_ref, v_ref = rest[-3:]
    d = x_ref.shape[2]
    h = _normmod(x_ref[0], g_ref[...], sc_ref[0], sh_ref[0]).astype(BF16)
    gmat = _group_matrix(MXU_DIM, A_HEAD_DIM, 1.0 / A_HEAD_DIM)
    q_scale = A_HEAD_DIM ** -0.5 * LOG2E
    for s in range(0, d, MXU_DIM):
        sl = slice(s, s + MXU_DIM)
        q = jnp.dot(h, w_ref[0, :, s:s + MXU_DIM], preferred_element_type=F32)
        q_ref[0, :, sl] = q * lax.rsqrt(_bdot(q * q, gmat) + NORM_EPS) * qg_ref[...] * q_scale
        k = jnp.dot(h, w_ref[0, :, d + s:d + s + MXU_DIM], preferred_element_type=F32)
        k = k * lax.rsqrt(_bdot(k * k, gmat) + NORM_EPS) * kg_ref[...]
        v = jnp.dot(h, w_ref[0, :, 2 * d + s:2 * d + s + MXU_DIM], preferred_element_type=F32)
        for slab in range(k_ref.shape[0]):
            k_ref[slab, 0, :, sl] = k
            v_ref[slab, 0, :, sl] = v


def _qkv(x, g, sc, sh, w_all, layer, q_gain, k_gain, kv_all, tm):
    bsz, t, d = x.shape
    n_layers = w_all.shape[0]
    qg = jnp.tile(q_gain, MXU_DIM // A_HEAD_DIM).reshape(1, MXU_DIM)
    kg = jnp.tile(k_gain, MXU_DIM // A_HEAD_DIM).reshape(1, MXU_DIM)
    kv_shape = jax.ShapeDtypeStruct((n_layers, bsz, t, d), F32)
    if kv_all is None:
        kv_spec = pl.BlockSpec((n_layers, 1, tm, d), lambda b, i: (0, b, i, 0))
    else:
        kv_spec = pl.BlockSpec((1, 1, tm, d), lambda b, i: (layer, b, i, 0))
    w_spec = pl.BlockSpec((1,) + w_all.shape[1:], lambda b, i: (layer, 0, 0),
                          pipeline_mode=pl.Buffered(1))
    aliased = [] if kv_all is None else list(kv_all)
    n_in = 7
    return pl.pallas_call(
        _qkv_kernel,
        grid=(bsz, t // tm),
        in_specs=[_tok_spec(tm, d), _const_spec(g), _mod_spec(sc, tm), _mod_spec(sh, tm),
                  w_spec, _const_spec(qg), _const_spec(kg)]
                 + [pl.BlockSpec(memory_space=pl.ANY)] * len(aliased),
        out_specs=[_tok_spec(tm, d), kv_spec, kv_spec],
        out_shape=[jax.ShapeDtypeStruct(x.shape, F32), kv_shape, kv_shape],
        input_output_aliases={n_in + n: 1 + n for n in range(len(aliased))},
        compiler_params=_params("parallel", "parallel"),
        name="qkv",
    )(x, g, sc, sh, w_all, qg, kg, *aliased)


def _sb_prompt_kernel(bias_ref, q_ref, k_ref, v_ref, o_ref, z_ref, s_ref, acc_ref, c_ref, *, qb, kb):
    hp = pl.program_id(1)
    i = pl.program_id(2)
    q = q_ref[0]
    lane = lax.broadcasted_iota(jnp.int32, (1, LANES), 1)
    row = lax.broadcasted_iota(jnp.int32, (kb, kb), 0)
    col = lax.broadcasted_iota(jnp.int32, (kb, kb), 1)
    suffix = (row >= col).astype(BF16)
    suffix2 = jnp.concatenate([suffix, suffix], axis=0)
    qm = [jnp.where(lane // A_HEAD_DIM == e, q, 0.0).astype(BF16) for e in range(2)]
    bias = [bias_ref[2 * hp + e] * LOG2E for e in range(2)]
    per_q = qb // kb
    n_blocks = per_q * (i + 1)

    def rows(t):
        return pl.ds(pl.multiple_of(jnp.maximum(n_blocks - 1 - t, 0) * kb, kb), kb)

    def first_row(t):
        return (per_q - 1 - t) * kb if isinstance(t, int) and t < per_q else 0

    def mask(t):
        if not isinstance(t, int) or t >= per_q:
            return None
        shape = (qb - first_row(t), kb)
        return lax.broadcasted_iota(jnp.int32, shape, 1) < lax.broadcasted_iota(jnp.int32, shape, 0)

    def logits(t, slot, e):
        r0 = first_row(t)
        z_ref[slot, e, r0:, :] = _bdot_nt(qm[e][r0:], k_ref[0, 0, rows(t), :]) + bias[e]

    def suffix_sums(t, slot, e):
        r0 = first_row(t)
        s_ref[slot, e, r0:, :] = _sb_suffix(z_ref[slot, e, r0:, :], suffix2, mask(t))

    def consume(t, slot, e):
        r0 = first_row(t)
        s_incl = s_ref[slot, e, r0:, :]
        p = _bdot(_sb_local(z_ref[slot, e, r0:, :], s_incl, mask(t)), v_ref[0, 0, rows(t), :])
        acc_ref[e, r0:, :] += jnp.exp2(-c_ref[e, r0:, :]) * p
        c_ref[e, r0:, :] += s_incl[:, 0:1]

    def trip(t, slot):
        for e in range(2):
            consume(t, slot, e)
            logits(t + 2, slot, e)
            suffix_sums(t + 1, 1 - slot, e)

    acc_ref[...] = jnp.zeros_like(acc_ref)
    c_ref[...] = jnp.zeros_like(c_ref)
    for e in range(2):
        logits(0, 0, e)
        logits(1, 1, e)
        suffix_sums(0, 0, e)
    for t in range(per_q):
        trip(t, t % 2)

    def pair(u, _):
        trip(2 * u + per_q, 0)
        trip(2 * u + per_q + 1, 1)
        return 0

    lax.fori_loop(0, (per_q // 2) * i, pair, 0)
    o_ref[0] = jnp.where(lane < A_HEAD_DIM, acc_ref[0], acc_ref[1])


def _sb_prompt(q, k_all, v_all, layer, logit_bias, qb, kb):
    bsz, t, d = q.shape
    n_pairs = d // LANES
    assert qb % (2 * kb) == 0 and t % qb == 0
    kv_spec = pl.BlockSpec((1, 1, t, LANES), lambda b, p, i, bias: (layer, b, 0, p))
    grid_spec = pltpu.PrefetchScalarGridSpec(
        num_scalar_prefetch=1,
        grid=(bsz, n_pairs, t // qb),
        in_specs=[pl.BlockSpec((1, qb, LANES), lambda b, p, i, bias: (b, i, p)), kv_spec, kv_spec],
        out_specs=pl.BlockSpec((1, qb, LANES), lambda b, p, i, bias: (b, i, p)),
        scratch_shapes=[pltpu.VMEM((2, 2, qb, kb), F32), pltpu.VMEM((2, 2, qb, kb), F32),
                        pltpu.VMEM((2, qb, LANES), F32), pltpu.VMEM((2, qb, 1), F32)],
    )
    return pl.pallas_call(
        functools.partial(_sb_prompt_kernel, qb=qb, kb=kb),
        grid_spec=grid_spec,
        out_shape=jax.ShapeDtypeStruct(q.shape, F32),
        compiler_params=_params("parallel", "parallel", "arbitrary"),
        name="sb_prompt",
    )(logit_bias, q, k_all, v_all)


def _sb_decode_kernel(pt_ref, q_ref, bias_ref, kn_ref, vn_ref, *rest, pages_per_step, page, t_new):
    k_refs = rest[:pages_per_step]
    v_refs = rest[pages_per_step:2 * pages_per_step]
    o_ref = rest[2 * pages_per_step]
    acc_ref, c_ref, qbd_ref = rest[2 * pages_per_step + 1:]
    s = pl.program_id(1)
    d = q_ref.shape[2]
    n_heads = d // A_HEAD_DIM
    n_rows = n_heads * t_new
    r_head = lax.broadcasted_iota(jnp.int32, (n_rows, d), 0) // t_new
    c_head = lax.broadcasted_iota(jnp.int32, (n_rows, d), 1) // A_HEAD_DIM

    @pl.when(s == 0)
    def _():
        tiled = jnp.concatenate([q_ref[0]] * n_heads, axis=0)
        qbd_ref[...] = jnp.where(r_head == c_head, tiled, 0.0).astype(BF16)

    qbd = qbd_ref[...]
    bias = bias_ref[...]
    row = lax.broadcasted_iota(jnp.int32, (page, page), 0)
    col = lax.broadcasted_iota(jnp.int32, (page, page), 1)
    suffix = (row >= col).astype(BF16)
    suffix2 = jnp.concatenate([suffix, suffix], axis=0)

    def accumulate(p, tot):
        acc_ref[...] += jnp.exp2(-c_ref[...]) * p
        c_ref[...] += tot

    @pl.when(s == 0)
    def _():
        acc_ref[...] = jnp.zeros_like(acc_ref)
        c_ref[...] = jnp.zeros_like(c_ref)
        pad = jnp.zeros((page - t_new, d), F32)
        k_pad = jnp.concatenate([kn_ref[0, 0], pad], axis=0).astype(BF16)
        v_pad = jnp.concatenate([vn_ref[0, 0], pad], axis=0).astype(BF16)
        q_t = lax.broadcasted_iota(jnp.int32, (n_rows, page), 0) % t_new
        key_i = lax.broadcasted_iota(jnp.int32, (n_rows, page), 1)
        w, tot = _sb_weights(_bdot_nt(qbd, k_pad) + bias, suffix2, key_i < q_t)
        accumulate(jnp.dot(w, v_pad, preferred_element_type=F32), tot)

    zs = [jnp.dot(qbd, k_refs[r][0, 0].astype(BF16), preferred_element_type=F32) + bias
          for r in range(pages_per_step)]
    ws = [_sb_weights(z, suffix2, None) for z in zs]
    ps = [_bdot_nt(w, v_refs[r][0, 0]) for r, (w, _) in enumerate(ws)]
    for p, (_, tot) in zip(ps, ws):
        accumulate(p, tot)

    @pl.when(s == pl.num_programs(1) - 1)
    def _():
        diag = jnp.where(r_head == c_head, acc_ref[...], 0.0)
        out = diag[0:t_new]
        for h in range(1, n_heads):
            out = out + diag[h * t_new:(h + 1) * t_new]
        o_ref[0] = out


def _sb_decode(q, k_all, v_all, new_layer, cache_k, cache_v, layer, page_table, logit_bias,
               pages_per_step):
    bsz, t_new, d = q.shape
    n_heads = d // A_HEAD_DIM
    n_layers, n_pool, page = cache_k.shape[:3]
    n_pages = page_table.shape[1]
    n_rows = n_heads * t_new
    ck = cache_k.transpose(0, 1, 3, 4, 2).reshape(n_layers, n_pool, d, page)
    cv = cache_v.transpose(0, 1, 3, 4, 2).reshape(n_layers, n_pool, d, page)
    bias_rows = jnp.broadcast_to(jnp.repeat(logit_bias * LOG2E, t_new)[:, None], (n_rows, page))
    n_steps = n_pages // pages_per_step

    def page_map(r):
        def index(b, s, pt):
            return (layer, pt[b, n_pages - 1 - (s * pages_per_step + r)], 0, 0)
        return index

    kv_specs = [pl.BlockSpec((1, 1, d, page), page_map(r)) for r in range(pages_per_step)]
    grid_spec = pltpu.PrefetchScalarGridSpec(
        num_scalar_prefetch=1,
        grid=(bsz, n_steps),
        in_specs=[pl.BlockSpec((1, t_new, d), lambda b, s, pt: (b, 0, 0)),
                  pl.BlockSpec((n_rows, page), lambda b, s, pt: (0, 0)),
                  pl.BlockSpec((1, 1, t_new, d), lambda b, s, pt: (new_layer, b, 0, 0)),
                  pl.BlockSpec((1, 1, t_new, d), lambda b, s, pt: (new_layer, b, 0, 0))]
                 + kv_specs + kv_specs,
        out_specs=pl.BlockSpec((1, t_new, d), lambda b, s, pt: (b, 0, 0)),
        scratch_shapes=[pltpu.VMEM((n_rows, d), F32), pltpu.VMEM((n_rows, 1), F32),
                        pltpu.VMEM((n_rows, d), BF16)],
    )
    return pl.pallas_call(
        functools.partial(_sb_decode_kernel, pages_per_step=pages_per_step, page=page, t_new=t_new),
        grid_spec=grid_spec,
        out_shape=jax.ShapeDtypeStruct(q.shape, F32),
        compiler_params=_params("parallel", "arbitrary"),
        name="sb_decode",
    )(page_table, q, bias_rows, k_all, v_all, *([ck] * pages_per_step), *([cv] * pages_per_step))


def _proj_res_kernel(a_ref, w_ref, x_ref, gt_ref, o_ref):
    o_ref[0] = x_ref[0] + gt_ref[0] * _bdot(a_ref[0], w_ref[...])


def _proj_res(a, w, x, gt, tm):
    bsz, t, d = x.shape
    return pl.pallas_call(
        _proj_res_kernel,
        grid=(bsz, t // tm),
        in_specs=[_tok_spec(tm, d), _const_spec(w), _tok_spec(tm, d), _mod_spec(gt, tm)],
        out_specs=_tok_spec(tm, d),
        out_shape=jax.ShapeDtypeStruct(x.shape, F32),
        compiler_params=_params("parallel", "parallel"),
        name="proj_res",
    )(a, w, x, gt)


def _gdn_out_kernel(o_ref_in, z_ref, on_ref, w_ref, x_ref, gt_ref, o_ref):
    o = o_ref_in[0]
    gmat = _group_matrix(MXU_DIM, B_HEAD_DIM, 1.0 / B_HEAD_DIM)
    ms = _group_reduce(o * o, gmat)
    a = o * lax.rsqrt(ms + NORM_EPS) * on_ref[...] * _silu(z_ref[0])
    o_ref[0] = x_ref[0] + gt_ref[0] * _bdot(a, w_ref[...])


def _gdn_out(o, z, o_norm, w, x, gt, tm):
    bsz, t, d = x.shape
    on = jnp.tile(o_norm, d // B_HEAD_DIM).reshape(1, d)
    return pl.pallas_call(
        _gdn_out_kernel,
        grid=(bsz, t // tm),
        in_specs=[_tok_spec(tm, d), _tok_spec(tm, d), _const_spec(on), _const_spec(w),
                  _tok_spec(tm, d), _mod_spec(gt, tm)],
        out_specs=_tok_spec(tm, d),
        out_shape=jax.ShapeDtypeStruct(x.shape, F32),
        compiler_params=_params("parallel", "parallel"),
        name="gdn_out",
    )(o, z, on, w, x, gt)


def _rwkv_out_kernel(y_ref, r_ref, k_ref, v_ref, gate_ref, rk_ref, lg_ref, lb_ref, w_ref,
                     x_ref, gt_ref, o_ref):
    y = y_ref[0]
    mean_mat = _group_matrix(MXU_DIM, C_HEAD_DIM, 1.0 / C_HEAD_DIM)
    sum_mat = _group_matrix(MXU_DIM, C_HEAD_DIM, 1.0)
    yc = y - _group_reduce(y, mean_mat, split=True)
    var = _group_reduce(yc * yc, mean_mat)
    yn = yc * lax.rsqrt(var + GN_EPS) * lg_ref[...] + lb_ref[...]
    bonus = _group_reduce(r_ref[0] * k_ref[0] * rk_ref[...], sum_mat) * v_ref[0]
    a = (yn + bonus) * gate_ref[0]
    o_ref[0] = x_ref[0] + gt_ref[0] * _bdot(a, w_ref[...])


def _rwkv_out(y, r, k, v, gate, r_k, ln_g, ln_b, w, x, gt, tm):
    bsz, t, d = x.shape
    rk = r_k.reshape(1, d)
    lg = ln_g.reshape(1, d)
    lb = ln_b.reshape(1, d)
    tok = _tok_spec(tm, d)
    return pl.pallas_call(
        _rwkv_out_kernel,
        grid=(bsz, t // tm),
        in_specs=[tok, tok, tok, tok, tok, _const_spec(rk), _const_spec(lg), _const_spec(lb),
                  _const_spec(w), tok, _mod_spec(gt, tm)],
        out_specs=tok,
        out_shape=jax.ShapeDtypeStruct(x.shape, F32),
        compiler_params=_params("parallel", "parallel"),
        name="rwkv_out",
    )(y, r, k, v, gate, rk, lg, lb, w, x, gt)


def _gdn_in_kernel(x_ref, g_ref, sc_ref, sh_ref, w_ref, xqkv_ref, z_ref, ba_ref):
    n_conv = xqkv_ref.shape[2]
    n_z = z_ref.shape[2]
    h = _normmod(x_ref[0], g_ref[...], sc_ref[0], sh_ref[0]).astype(BF16)
    for s in range(0, n_conv, MXU_DIM):
        xqkv_ref[0, :, s:s + MXU_DIM] = jnp.dot(h, w_ref[:, s:s + MXU_DIM],
                                                preferred_element_type=F32)
    for s in range(0, n_z, MXU_DIM):
        z_ref[0, :, s:s + MXU_DIM] = jnp.dot(h, w_ref[:, n_conv + s:n_conv + s + MXU_DIM],
                                             preferred_element_type=F32)
    ba_ref[0] = jnp.dot(h, w_ref[:, n_conv + n_z:], preferred_element_type=F32)


def _gdn_in(x, g, sc, sh, w_pad, n_conv, tm):
    bsz, t, d = x.shape
    return pl.pallas_call(
        _gdn_in_kernel,
        grid=(bsz, t // tm),
        in_specs=[_tok_spec(tm, d), _const_spec(g), _mod_spec(sc, tm), _mod_spec(sh, tm),
                  _const_spec(w_pad)],
        out_specs=[_tok_spec(tm, n_conv), _tok_spec(tm, d), _tok_spec(tm, LANES)],
        out_shape=[jax.ShapeDtypeStruct((bsz, t, n_conv), F32),
                   jax.ShapeDtypeStruct((bsz, t, d), F32),
                   jax.ShapeDtypeStruct((bsz, t, LANES), F32)],
        compiler_params=_params("parallel", "parallel"),
        name="gdn_in",
    )(x, g, sc, sh, w_pad)


def _gdn_conv_kernel(*refs, pre_shifted):
    if pre_shifted:
        x_ref, xs_ref, cw_ref, ba_ref, alog_ref, dtb_ref = refs[:6]
    else:
        x_ref, p8_ref, cw_ref, ba_ref, alog_ref, dtb_ref = refs[:6]
    q_ref, k_ref, kb_ref, vb_ref, g_ref = refs[6:]
    i = pl.program_id(1)
    tm = x_ref.shape[1]
    hd = q_ref.shape[2]
    n_heads = hd // B_HEAD_DIM
    x = x_ref[0]
    conv = x * cw_ref[B_CONV - 1:B_CONV, :]
    if pre_shifted:
        for s in range(1, B_CONV):
            conv = conv + xs_ref[s - 1, 0] * cw_ref[B_CONV - 1 - s:B_CONV - s, :]
    else:
        halo = jnp.where(i > 0, p8_ref[0], 0.0)
        full = jnp.concatenate([halo, x], axis=0)
        for s in range(1, B_CONV):
            conv = conv + pltpu.roll(full, s, 0)[8:] * cw_ref[B_CONV - 1 - s:B_CONV - s, :]
    act = _silu(conv)
    r = lax.broadcasted_iota(jnp.int32, (LANES, hd), 0)
    c_head = lax.broadcasted_iota(jnp.int32, (LANES, hd), 1) // B_HEAD_DIM
    e_b = (r == c_head).astype(BF16)
    e_a = (r == c_head + n_heads).astype(BF16)
    ba = ba_ref[0]
    beta = _sigmoid(_dot_xe(ba, e_b))
    g = -jnp.exp(alog_ref[...]) * _softplus(_dot_xe(ba, e_a) + dtb_ref[...])
    g_ref[0] = g
    ones = _group_matrix(MXU_DIM, B_HEAD_DIM, 1.0)
    aq = act[:, :hd]
    ak = act[:, hd:2 * hd]
    q_ref[0] = aq * lax.rsqrt(_group_reduce(aq * aq, ones) + L2_EPS) * (B_HEAD_DIM ** -0.5)
    k = ak * lax.rsqrt(_group_reduce(ak * ak, ones) + L2_EPS)
    k_ref[0] = k
    kb_ref[0] = k * beta
    vb_ref[0] = act[:, 2 * hd:] * beta


def _gdn_conv(xqkv, xs, conv_w, ba, a_log, dt_bias, hd, tm):
    bsz, t, n_conv = xqkv.shape
    alog = jnp.repeat(a_log, B_HEAD_DIM).reshape(1, hd)
    dtb = jnp.repeat(dt_bias, B_HEAD_DIM).reshape(1, hd)
    pre_shifted = xs is not None
    if pre_shifted:
        second = xs
        second_spec = pl.BlockSpec((B_CONV - 1, 1, tm, n_conv), lambda b, i: (0, b, i, 0))
    else:
        second = xqkv
        second_spec = _prev8_spec(tm, n_conv)
    shp = jax.ShapeDtypeStruct((bsz, t, hd), F32)
    return pl.pallas_call(
        functools.partial(_gdn_conv_kernel, pre_shifted=pre_shifted),
        grid=(bsz, t // tm),
        in_specs=[_tok_spec(tm, n_conv), second_spec, _const_spec(conv_w), _tok_spec(tm, LANES),
                  _const_spec(alog), _const_spec(dtb)],
        out_specs=[_tok_spec(tm, hd)] * 5,
        out_shape=[shp] * 5,
        compiler_params=_params("parallel", "parallel"),
        name="gdn_conv",
    )(xqkv, second, conv_w, ba, alog, dtb)


def _gdn_chunk_kernel(q_ref, k_ref, kb_ref, vb_ref, g_ref, s0_ref, o_ref, s_out_ref, s_ref, *,
                      n_heads, n_chunks):
    i = pl.program_id(2)
    c = CHUNK
    hw = B_HEAD_DIM

    @pl.when(i == 0)
    def _():
        s_ref[...] = s0_ref[0]

    row = lax.broadcasted_iota(jnp.int32, (c, c), 0)
    col = lax.broadcasted_iota(jnp.int32, (c, c), 1)
    lower = row >= col
    strict = row > col
    lower_b = lower.astype(BF16)
    ones_avg = jnp.full((c, LANES), 1.0 / LANES, BF16)

    def row_layout(g_cb):
        gh, gl = _split(g_cb)
        nt = lambda x: lax.dot_general(ones_avg, x, (((1,), (1,)), ((), ())),
                                       preferred_element_type=F32)
        return nt(gh) + nt(gl)

    sls = [(slice(n * c, (n + 1) * c), slice(h * hw, (h + 1) * hw))
           for h in range(n_heads) for n in range(n_chunks)]
    rng = range(len(sls))
    q = [q_ref[0, rs, ls] for rs, ls in sls]
    k = [k_ref[0, rs, ls] for rs, ls in sls]
    kb = [kb_ref[0, rs, ls] for rs, ls in sls]
    vb = [vb_ref[0, rs, ls] for rs, ls in sls]
    g_cb = [_dot_ex(lower_b, g_ref[0, rs, ls]) for rs, ls in sls]
    g_row = [row_layout(x) for x in g_cb]
    decay = [jnp.where(lower, jnp.exp(jnp.where(lower, g_cb[n][:, :c] - g_row[n], 0.0)), 0.0)
             for n in rng]
    a = [jnp.where(strict, decay[n] * _bdot_nt(kb[n], k[n]), 0.0) for n in rng]
    t_inv = _tri_inverses(a, 5)
    e_g = [jnp.exp(x) for x in g_cb]
    wu = [_dot3(t_inv[n], jnp.concatenate([kb[n] * e_g[n], vb[n]], axis=1)) for n in rng]
    qk = [decay[n] * _bdot_nt(q[n], k[n]) for n in rng]
    g_last = [x[c - 1:c, :] for x in g_cb]
    k_end = [jnp.exp(g_last[n] - g_cb[n]) * k[n] for n in rng]
    qw = [_bdot(qk[n], wu[n]) for n in rng]
    q_eff = [q[n] * e_g[n] - qw[n][:, :hw] for n in rng]
    o0 = [x[:, hw:] for x in qw]
    mn = [_bdot_tn(k_end[n], wu[n]) for n in rng]
    m0 = [x[:, :hw] for x in mn]
    n0 = [x[:, hw:] for x in mn]

    states = [s_ref[h] for h in range(n_heads)]
    for m in range(n_chunks):
        for h in range(n_heads):
            n = h * n_chunks + m
            rs, ls = sls[n]
            s = states[h]
            o_ref[0, rs, ls] = _bdot(q_eff[n], s) + o0[n]
            states[h] = jnp.exp(g_last[n]) * s - _bdot(m0[n], s) + n0[n]
    for h in range(n_heads):
        s_ref[h] = states[h]
        s_out_ref[0, h] = states[h]


def _gdn_chunk(q, k, kb, vb, g, s0, ct, heads_per_step):
    bsz, t, hd = q.shape
    hw = B_HEAD_DIM
    blk = pl.BlockSpec((1, ct, heads_per_step * hw), lambda b, h, i: (b, i, h))
    st = pl.BlockSpec((1, heads_per_step, hw, hw), lambda b, h, i: (b, h, 0, 0))
    return pl.pallas_call(
        functools.partial(_gdn_chunk_kernel, n_heads=heads_per_step, n_chunks=ct // CHUNK),
        grid=(bsz, hd // (heads_per_step * hw), t // ct),
        in_specs=[blk] * 5 + [st],
        out_specs=[blk, st],
        out_shape=[jax.ShapeDtypeStruct(q.shape, F32), jax.ShapeDtypeStruct(s0.shape, F32)],
        scratch_shapes=[pltpu.VMEM((heads_per_step, hw, hw), F32)],
        compiler_params=_params("parallel", "parallel", "arbitrary"),
        name="gdn_chunk",
    )(q, k, kb, vb, g, s0)


def _rwkv_in_kernel(*refs, seq_len, per_row_shift):
    (x_ref, p8_ref, g_ref, sc_ref, sh_ref, sh0_ref, mu_ref, wrkv_ref, w1_ref, w2_ref, a1_ref, a2_ref,
     g1_ref, g2_ref, w0_ref, a0_ref, kk_ref, ka_ref) = refs[:18]
    r_ref, lw_ref, k_ref, v_ref, kn_ref, bb_ref, gate_ref, hl_ref = refs[18:]
    i = pl.program_id(1)
    tm = x_ref.shape[1]
    g = g_ref[...]
    h = _normmod(x_ref[0], g, sc_ref[0], sh_ref[0])
    if per_row_shift:
        sc8, sh8 = sc_ref[0, 0:8], sh_ref[0, 0:8]
    else:
        sc8, sh8 = sc_ref[0], sh_ref[0]
    h8 = _normmod(p8_ref[0], g, sc8, sh8)
    prev = pltpu.roll(jnp.concatenate([h8, h], axis=0), 1, 0)[8:]
    t_glob = i * tm + lax.broadcasted_iota(jnp.int32, (tm, 1), 0)
    prev = jnp.where(t_glob % seq_len == 0, sh0_ref[0], prev)
    xx = prev - h
    mix = lambda n: (h + xx * mu_ref[n:n + 1, :]).astype(BF16)
    r = jnp.dot(mix(0), wrkv_ref[0], preferred_element_type=F32)
    k = jnp.dot(mix(1), wrkv_ref[1], preferred_element_type=F32)
    v_ref[0] = jnp.dot(mix(2), wrkv_ref[2], preferred_element_type=F32)
    r_ref[0] = r
    wl = w0_ref[...] + _bdot(jnp.tanh(jnp.dot(mix(3), w1_ref[...], preferred_element_type=F32)),
                             w2_ref[...])
    lw_ref[0] = -jnp.exp(-_softplus(-wl) - 0.5)
    a = _sigmoid(a0_ref[...] + _bdot(jnp.dot(mix(4), a1_ref[...], preferred_element_type=F32),
                                     a2_ref[...]))
    gate_ref[0] = _bdot(_sigmoid(jnp.dot(mix(5), g1_ref[...], preferred_element_type=F32)),
                        g2_ref[...])
    kx = k * kk_ref[...]
    ones = _group_matrix(MXU_DIM, C_HEAD_DIM, 1.0)
    kn = kx * lax.rsqrt(_group_reduce(kx * kx, ones) + L2_EPS)
    kn_ref[0] = kn
    bb_ref[0] = kn * a
    k_ref[0] = k * (1.0 + (a - 1.0) * ka_ref[...])
    hl_ref[0, 0] = h[tm - 8:tm]


def _rwkv_in(x, g, sc, sh, shift0, seq_len, P, tm):
    bsz, t, d = x.shape
    per_row = shift0.shape[1] != 1
    consts = [P['mu'], P['w_rkv'], P['w1'], P['w2'], P['a1'], P['a2'], P['g1'], P['g2'],
              P['w0'], P['a0'], P['k_k'], P['k_a']]
    shp = jax.ShapeDtypeStruct(x.shape, F32)
    n_t = t // tm
    outs = pl.pallas_call(
        functools.partial(_rwkv_in_kernel, seq_len=seq_len, per_row_shift=per_row),
        grid=(bsz, n_t),
        in_specs=[_tok_spec(tm, d), _prev8_spec(tm, d), _const_spec(g), _mod_spec(sc, tm),
                  _mod_spec(sh, tm), _mod_spec(shift0, tm)] + [_const_spec(c) for c in consts],
        out_specs=[_tok_spec(tm, d)] * 7 + [pl.BlockSpec((1, 1, 8, d), lambda b, i: (b, i, 0, 0))],
        out_shape=[shp] * 7 + [jax.ShapeDtypeStruct((bsz, n_t, 8, d), F32)],
        compiler_params=_params("parallel", "parallel"),
        name="rwkv_in",
    )(x, x, g, sc, sh, shift0, *consts)
    return outs


def _rwkv_scan_kernel(r_ref, lw_ref, k_ref, v_ref, kn_ref, bb_ref, s0_ref, y_ref, s_out_ref, s_ref,
                      *, n_pairs, n_chunks):
    i = pl.program_id(2)
    c = CHUNK
    n = 2 * c

    @pl.when(i == 0)
    def _():
        s_ref[...] = s0_ref[0]

    lane = lax.broadcasted_iota(jnp.int32, (1, LANES), 1)
    row = lax.broadcasted_iota(jnp.int32, (n, n), 0)
    col = lax.broadcasted_iota(jnp.int32, (n, n), 1)
    same = (row // c) == (col // c)
    incl = same & (row >= col)
    strict = same & (row > col)
    incl_b = incl.astype(BF16)

    def stack(x):
        return jnp.concatenate([jnp.where(lane < C_HEAD_DIM, x, 0.0),
                                jnp.where(lane >= C_HEAD_DIM, x, 0.0)], axis=0)

    sls = [(slice(m * c, (m + 1) * c), slice(p * LANES, (p + 1) * LANES))
           for p in range(n_pairs) for m in range(n_chunks)]
    rng = range(len(sls))
    lw = [stack(lw_ref[0, rs, ls]) for rs, ls in sls]
    lc = [_dot_ex(incl_b, x) for x in lw]
    lc_last = [x[c - 1:c, :] + x[n - 1:n, :] for x in lc]
    p_inv = [jnp.exp(-x) for x in lc]
    p_end = [jnp.exp(lc_last[m] - lc[m]) for m in rng]
    kn_t = [stack(kn_ref[0, sls[m][0], sls[m][1]]) * jnp.exp(lc[m] - lw[m]) for m in rng]
    k_s = [stack(k_ref[0, rs, ls]) for rs, ls in sls]
    b_s = [stack(bb_ref[0, rs, ls]) for rs, ls in sls]
    v_s = [stack(v_ref[0, rs, ls]) for rs, ls in sls]
    k_t = [k_s[m] * p_inv[m] for m in rng]
    b_t = [b_s[m] * p_inv[m] for m in rng]
    r_t = [stack(r_ref[0, sls[m][0], sls[m][1]]) * jnp.exp(lc[m]) for m in rng]
    bk_t = [jnp.concatenate([b_t[m], k_t[m]], axis=0) for m in rng]
    a_k = [_bdot_nt(kn_t[m], bk_t[m]) for m in rng]
    a_r = [_bdot_nt(r_t[m], bk_t[m]) for m in rng]
    a_kb = [jnp.where(strict, x[:, :n], 0.0) for x in a_k]
    a_kk = [jnp.where(strict, x[:, n:], 0.0) for x in a_k]
    a_rb = [jnp.where(incl, x[:, :n], 0.0) for x in a_r]
    a_rk = [jnp.where(incl, x[:, n:], 0.0) for x in a_r]
    t_inv = _tri_inverses(a_kb, 5)
    ku = [_dot3(t_inv[m], jnp.concatenate([kn_t[m], _bdot(a_kk[m], v_s[m])], axis=1))
          for m in rng]
    kt = [x[:, :LANES] for x in ku]
    u0 = [x[:, LANES:] for x in ku]
    k_e = [k_s[m] * p_end[m] for m in rng]
    b_e = [b_s[m] * p_end[m] for m in rng]
    g_mat = [_bdot_tn(kt[m], b_e[m]) for m in rng]
    n0 = [_bdot_tn(v_s[m], k_e[m]) - _bdot_tn(u0[m], b_e[m]) for m in rng]
    rb = [_bdot(a_rb[m], ku[m]) for m in rng]
    r_eff = [r_t[m] - rb[m][:, :LANES] for m in rng]
    y0 = [_bdot(a_rk[m], v_s[m]) - rb[m][:, LANES:] for m in rng]

    states = [s_ref[p] for p in range(n_pairs)]
    for j in range(n_chunks):
        for p in range(n_pairs):
            m = p * n_chunks + j
            rs, ls = sls[m]
            s = states[p]
            y = _bdot_nt(r_eff[m], s) + y0[m]
            y_ref[0, rs, ls] = y[:c] + y[c:]
            states[p] = s * jnp.exp(lc_last[m]) - _bdot(s, g_mat[m]) + n0[m]
    for p in range(n_pairs):
        s_ref[p] = states[p]
        s_out_ref[0, p] = states[p]


def _rwkv_scan(r, lw, k, v, kn, bb, s0_pairs, ct, pairs_per_step):
    bsz, t, d = r.shape
    blk = pl.BlockSpec((1, ct, pairs_per_step * LANES), lambda b, p, i: (b, i, p))
    st = pl.BlockSpec((1, pairs_per_step, LANES, LANES), lambda b, p, i: (b, p, 0, 0))
    return pl.pallas_call(
        functools.partial(_rwkv_scan_kernel, n_pairs=pairs_per_step, n_chunks=ct // CHUNK),
        grid=(bsz, d // (pairs_per_step * LANES), t // ct),
        in_specs=[blk] * 6 + [st],
        out_specs=[blk, st],
        out_shape=[jax.ShapeDtypeStruct(r.shape, F32), jax.ShapeDtypeStruct(s0_pairs.shape, F32)],
        scratch_shapes=[pltpu.VMEM((pairs_per_step, LANES, LANES), F32)],
        compiler_params=_params("parallel", "parallel", "arbitrary"),
        name="rwkv_scan",
    )(r, lw, k, v, kn, bb, s0_pairs)


def _chain_split(t_pad, n_heads):
    heads = min(n_heads, HEADS_PER_STEP)
    ct = min(t_pad, CHAINS_PER_STEP // heads * CHUNK)
    heads = max(heads, min(n_heads, CHAINS_PER_STEP * CHUNK // ct))
    return ct, heads


def _pad_time(a, t_pad):
    return jnp.pad(a, ((0, 0), (0, t_pad - a.shape[1]), (0, 0)))


def _pairs_from_heads(s):
    bsz, n_h, n, _ = s.shape
    sp = s.reshape(bsz, n_h // 2, 2, n, n)
    z = jnp.zeros_like(sp[:, :, 0])
    top = jnp.concatenate([sp[:, :, 0], z], axis=-1)
    bot = jnp.concatenate([z, sp[:, :, 1]], axis=-1)
    return jnp.concatenate([top, bot], axis=-2)


def _heads_from_pairs(sp):
    bsz, n_p, n2, _ = sp.shape
    n = n2 // 2
    return jnp.stack([sp[:, :, :n, :n], sp[:, :, n:, n:]], axis=2).reshape(bsz, 2 * n_p, n, n)


def _run_trunk(x, mods, W, tm, seq_len, cache, state):
    bsz, t, d = x.shape
    n_seq = bsz * t // seq_len
    depth = len(mods)
    kv_all, bconv, bssm, cshift, cwkv = None, [], [], [], []
    n_a = W['a_w_qkv'].shape[0]
    for l in range(depth):
        kind, j = l % 3, l // 3
        sh1, sc1, gt1, sh2, sc2, gt2 = mods[l]
        g_mix = W['ln_mix'][l].reshape(1, d)
        if kind == 0:
            q, k_all, v_all = _qkv(x, g_mix, sc1, sh1, W['a_w_qkv'], j, W['a_q_norm'][j],
                                   W['a_k_norm'][j], kv_all, tm)
            kv_all = (k_all, v_all)
            if cache is None:
                o = _sb_prompt(q, k_all, v_all, j, W['a_logit_bias'][j],
                               qb=SB_QUERY_BLOCK, kb=SB_KEY_BLOCK)
            else:
                cache_k, cache_v, page_table = cache
                o = _sb_decode(q.reshape(n_seq, seq_len, d), k_all.reshape(n_a, n_seq, seq_len, d),
                               v_all.reshape(n_a, n_seq, seq_len, d), j, cache_k, cache_v, j,
                               page_table, W['a_logit_bias'][j],
                               pages_per_step=min(16, page_table.shape[1])).reshape(bsz, t, d)
            x = _proj_res(o, W['a_w_o'][j], x, gt1, tm)
        elif kind == 1:
            n_conv = 3 * d
            xqkv, z, ba = _gdn_in(x, g_mix, sc1, sh1, W['b_w_in'][j], n_conv, tm)
            xq_seq = xqkv.reshape(n_seq, seq_len, n_conv)
            if state is None:
                xs = None
                bconv.append(xq_seq[:, seq_len - (B_CONV - 1):])
                s0 = jnp.zeros((n_seq, d // B_HEAD_DIM, B_HEAD_DIM, B_HEAD_DIM), F32)
            else:
                xc = jnp.concatenate([state['b_conv'][j], xq_seq], axis=1)
                xs = jnp.stack([xc[:, B_CONV - 1 - s:B_CONV - 1 - s + seq_len].reshape(bsz, t, n_conv)
                                for s in range(1, B_CONV)])
                bconv.append(xc[:, seq_len:])
                s0 = state['b_ssm'][j]
            q, k, kb, vb, g = _gdn_conv(xqkv, xs, W['b_conv'][j], ba, W['b_a_log'][j],
                                        W['b_dt_bias'][j], d, min(tm, 256))
            t_pad = -(-seq_len // CHUNK) * CHUNK
            seqs = [_pad_time(a.reshape(n_seq, seq_len, d), t_pad) for a in (q, k, kb, vb, g)]
            ct, heads = _chain_split(t_pad, d // B_HEAD_DIM)
            o, s_new = _gdn_chunk(*seqs, s0, ct=ct, heads_per_step=heads)
            bssm.append(s_new)
            o = o[:, :seq_len].reshape(bsz, t, d)
            x = _gdn_out(o, z, W['b_o_norm'][j], W['b_w_o'][j], x, gt1, tm)
        else:
            if state is None:
                shift0 = jnp.zeros((bsz, 1, d), F32)
                s0 = jnp.zeros((n_seq, d // C_HEAD_DIM, C_HEAD_DIM, C_HEAD_DIM), F32)
            else:
                shift0 = jnp.repeat(state['c_shift'][j], seq_len, axis=0).reshape(bsz, t, d)
                s0 = state['c_wkv'][j]
            tm_c = min(tm, 512)
            r, lw, k, v, kn, bb, gate, h_last = _rwkv_in(x, g_mix, sc1, sh1, shift0, seq_len,
                                                        W['c'][j], tm_c)
            if state is None:
                cshift.append(h_last[:, -1, 7])
            else:
                assert t == tm_c and seq_len == 8
                cshift.append(_last_rows(x, g_mix, sc1, sh1, seq_len))
            t_pad = -(-seq_len // CHUNK) * CHUNK
            seqs = [_pad_time(a.reshape(n_seq, seq_len, d), t_pad) for a in (r, lw, k, v, kn, bb)]
            ct, pairs = _chain_split(t_pad, d // LANES)
            y, s_new = _rwkv_scan(*seqs, _pairs_from_heads(s0), ct=ct, pairs_per_step=pairs)
            cwkv.append(_heads_from_pairs(s_new))
            y = y[:, :seq_len].reshape(bsz, t, d)
            x = _rwkv_out(y, r, k, v, gate, W['c_r_k'][j], W['c_ln_g'][j], W['c_ln_b'][j],
                          W['c_w_o'][j], x, gt1, tm)
        x = _mlp(x, W['ln_mlp'][l].reshape(1, d), sc2, sh2, gt2, W['w_up'], W['w_down'], l,
                 MLP_TOKEN_TILE if t % MLP_TOKEN_TILE == 0 else tm)
    heads = (n_a, n_seq, seq_len, d // A_HEAD_DIM, A_HEAD_DIM)
    return x, kv_all[0].reshape(heads), kv_all[1].reshape(heads), bconv, bssm, cshift, cwkv


def _hmod_kernel(x_ref, g_ref, sc_ref, sh_ref, o_ref):
    o_ref[0] = _normmod(x_ref[0], g_ref[...], sc_ref[0], sh_ref[0])


def _last_rows(x, g, sc, sh, seq_len):
    bsz, t, d = x.shape
    n_seq = bsz * t // seq_len
    pick = lambda a: a.reshape(n_seq, seq_len, d)[:, seq_len - 1].reshape(1, n_seq, d)
    xs, scs, shs = pick(x), pick(sc), pick(sh)
    return pl.pallas_call(
        _hmod_kernel,
        grid=(1, 1),
        in_specs=[_tok_spec(n_seq, d), _const_spec(g), _tok_spec(n_seq, d), _tok_spec(n_seq, d)],
        out_specs=_tok_spec(n_seq, d),
        out_shape=jax.ShapeDtypeStruct((1, n_seq, d), F32),
        compiler_params=_params("parallel", "parallel"),
        name="last_rows",
    )(xs, g, scs, shs)[0]


def kernel(x_prompt, x_sample, c_prompt, c_sample, cache_k, cache_v, page_table, state_b_conv, state_b_ssm, state_c_shift, state_c_wkv, ln_mix, ln_mlp, w_ada, b_ada, w_up, w_down, a_w_qkv, a_q_norm, a_k_norm, a_logit_bias, a_w_o, b_w_in, b_conv, b_a_log, b_dt_bias, b_o_norm, b_w_o, c_mu, c_w_rkv, c_w0, c_w1, c_w2, c_a0, c_a1, c_a2, c_g1, c_g2, c_k_k, c_k_a, c_r_k, c_ln_g, c_ln_b, c_w_o):
    bp, t_p, d = x_prompt.shape
    bs, t_s, _ = x_sample.shape
    depth = ln_mix.shape[0]
    n_c = c_mu.shape[0]

    def lora_cols(w):
        return jnp.pad(w, ((0, 0), (0, 0), (0, LANES - w.shape[2]))).astype(BF16)

    def lora_rows(w):
        return jnp.pad(w, ((0, 0), (0, LANES - w.shape[1]), (0, 0))).astype(BF16)

    b_in = b_w_in.shape[2]
    b_in_pad = -(-b_in // LANES) * LANES
    W = {
        'ln_mix': ln_mix, 'ln_mlp': ln_mlp,
        'w_up': w_up.astype(BF16), 'w_down': w_down.astype(BF16),
        'a_w_qkv': a_w_qkv.astype(BF16), 'a_q_norm': a_q_norm, 'a_k_norm': a_k_norm,
        'a_logit_bias': a_logit_bias, 'a_w_o': a_w_o.astype(BF16),
        'b_w_in': jnp.pad(b_w_in, ((0, 0), (0, 0), (0, b_in_pad - b_in))).astype(BF16),
        'b_conv': b_conv, 'b_a_log': b_a_log, 'b_dt_bias': b_dt_bias, 'b_o_norm': b_o_norm,
        'b_w_o': b_w_o.astype(BF16),
        'c_r_k': c_r_k.reshape(n_c, d), 'c_ln_g': c_ln_g, 'c_ln_b': c_ln_b, 'c_w_o': c_w_o.astype(BF16),
    }
    w1, w2 = lora_cols(c_w1), lora_rows(c_w2)
    a1, a2 = lora_cols(c_a1), lora_rows(c_a2)
    g1, g2 = c_g1.astype(BF16), c_g2.astype(BF16)
    rkv = c_w_rkv.astype(BF16)
    W['c'] = [{'mu': c_mu[j], 'w_rkv': rkv[j], 'w1': w1[j], 'w2': w2[j], 'a1': a1[j], 'a2': a2[j],
               'g1': g1[j], 'g2': g2[j], 'w0': c_w0[j].reshape(1, d), 'a0': c_a0[j].reshape(1, d),
               'k_k': c_k_k[j].reshape(1, d), 'k_a': c_k_a[j].reshape(1, d)} for j in range(n_c)]

    mod_all = _ada(jnp.concatenate([c_prompt, c_sample], axis=0), w_ada, b_ada)
    mod_p = mod_all[:, :, :bp].reshape(depth, 6, bp, 1, d)
    mod_s = jnp.repeat(mod_all[:, :, bp:], t_s, axis=2).reshape(depth, 6, 1, bs * t_s, d)
    mods_p = [[mod_p[l, n] for n in range(6)] for l in range(depth)]
    mods_s = [[mod_s[l, n] for n in range(6)] for l in range(depth)]

    y_p, k_p, v_p, bconv_p, bssm_p, cshift_p, cwkv_p = _run_trunk(
        x_prompt, mods_p, W, 512, t_p, None, None)
    state = {'b_conv': state_b_conv, 'b_ssm': state_b_ssm, 'c_shift': state_c_shift,
             'c_wkv': state_c_wkv}
    y_s, k_s, v_s, bconv_s, bssm_s, cshift_s, cwkv_s = _run_trunk(
        x_sample.reshape(1, bs * t_s, d), mods_s, W, bs * t_s, t_s,
        (cache_k, cache_v, page_table), state)
    st = jnp.stack
    return (y_p, y_s.reshape(bs, t_s, d), k_p, v_p, k_s, v_s, st(bconv_p), st(bssm_p),
            st(bconv_s), st(bssm_s), st(cshift_p), st(cwkv_p), st(cshift_s), st(cwkv_s))
```
